```python
import jax, jax.numpy as jnp
from jax import lax
import numpy as np

D_MODEL = 1024
BATCH = 4
SEQ = 8192
DEPTH = 2

GRID_W = 64
CTX_LEN = 256
EPS = 1e-6

N_Q_HEADS = 8
N_KV_HEADS = 2
HEAD_DIM = 64
ATTN_WIDTH = N_Q_HEADS * HEAD_DIM
KV_WIDTH = N_KV_HEADS * HEAD_DIM
QKV_WIDTH = ATTN_WIDTH + 2 * KV_WIDTH
Q_BLOCK = 128
ROPE_THETA = 10000.0

CHUNK = 128
GMLP_GROUPS = 4
GMLP_WIDTH = D_MODEL // 4

CONV_WIDTH = D_MODEL // 4
CONV_K = 31

MIX_WIDTH = ATTN_WIDTH + GMLP_WIDTH + CONV_WIDTH
IN_WIDTH = QKV_WIDTH + 2 * GMLP_WIDTH + 2 * CONV_WIDTH

PEER_HEADS = 8
N_KEYS = 128
N_EXPERTS = N_KEYS * N_KEYS
PEER_TOPK = 16
PEER_QDIM = 256
PEER_TOKEN_BLOCK = 128

ALPHA = (2 * DEPTH) ** 0.25
BETA = (8 * DEPTH) ** -0.25

kernel_name = "hybrid_parallel_heads_peer_dit_block"


def layer_norm(x, g=None, b=None):
    xf = x.astype(jnp.float32)
    mu = jnp.mean(xf, -1, keepdims=True)
    var = jnp.mean(jnp.square(xf - mu), -1, keepdims=True)
    y = (xf - mu) * lax.rsqrt(var + EPS)
    if g is not None:
        y = y * g + b
    return y.astype(x.dtype)


def rms_norm(x, g):
    xf = x.astype(jnp.float32)
    y = xf * lax.rsqrt(jnp.mean(jnp.square(xf), -1, keepdims=True) + EPS) * g
    return y.astype(x.dtype)


def rope_1d(x, pos):
    half = x.shape[-1] // 2
    freqs = ROPE_THETA ** (-jnp.arange(half, dtype=jnp.float32) / half)
    ang = pos.astype(jnp.float32)[:, None] * freqs
    cos = jnp.cos(ang)[None, :, None, :]
    sin = jnp.sin(ang)[None, :, None, :]
    x1 = x[..., :half].astype(jnp.float32)
    x2 = x[..., half:].astype(jnp.float32)
    return jnp.concatenate([x1 * cos - x2 * sin, x1 * sin + x2 * cos], -1).astype(x.dtype)


def axial_rope(x, rows, cols):
    h = x.shape[-1] // 2
    return jnp.concatenate([rope_1d(x[..., :h], rows), rope_1d(x[..., h:], cols)], -1)


def modulation(cond, w_mod, b_mod):
    m = jax.nn.silu(cond) @ w_mod + b_mod
    return jnp.split(m, 6, axis=-1)


def modulate(x, shift, scale):
    return layer_norm(x) * (1.0 + scale) + shift


def attn_queries(p, q_g, rows=None, cols=None):
    B, L, _ = p.shape
    q = rms_norm(p[..., :ATTN_WIDTH].reshape(B, L, N_Q_HEADS, HEAD_DIM), q_g)
    return q if rows is None else axial_rope(q, rows, cols)


def attn_keys_values(p, k_g, rows=None, cols=None):
    B, L, _ = p.shape
    k = rms_norm(p[..., ATTN_WIDTH:ATTN_WIDTH + KV_WIDTH].reshape(B, L, N_KV_HEADS, HEAD_DIM), k_g)
    v = p[..., ATTN_WIDTH + KV_WIDTH:QKV_WIDTH].reshape(B, L, N_KV_HEADS, HEAD_DIM)
    if rows is not None:
        k = axial_rope(k, rows, cols)
    return k, v


def gqa_attend(q, k, v):
    B, Lq = q.shape[:2]
    qg = q.reshape(B, Lq, N_KV_HEADS, N_Q_HEADS // N_KV_HEADS, HEAD_DIM)
    s = jnp.einsum('bqgrd,bkgd->bgrqk', qg, k, preferred_element_type=jnp.float32) * (HEAD_DIM ** -0.5)
    pr = jax.nn.softmax(s, axis=-1).astype(v.dtype)
    o = jnp.einsum('bgrqk,bkgd->bqgrd', pr, v)
    return o.reshape(B, Lq, ATTN_WIDTH)


def blocked_attention(q, k, v):
    B, L = q.shape[:2]
    nb = L // Q_BLOCK
    qb = jnp.moveaxis(q.reshape(B, nb, Q_BLOCK, N_Q_HEADS, HEAD_DIM), 1, 0)
    ob = lax.map(lambda qi: gqa_attend(qi, k, v), qb)
    return jnp.moveaxis(ob, 0, 1).reshape(B, L, ATTN_WIDTH)


def chunk_gmlp(z, ln_g, ln_b, w_s, b_s):
    B, L, _ = z.shape
    z = jax.nn.gelu(z)
    u, v = jnp.split(z, 2, axis=-1)
    v = layer_norm(v, ln_g, ln_b)
    v = v.reshape(B, L // CHUNK, CHUNK, GMLP_GROUPS, GMLP_WIDTH // GMLP_GROUPS)
    s = jnp.einsum('gpq,bnqgc->bnpgc', w_s, v) + b_s.T[None, None, :, :, None]
    return u * s.reshape(B, L, GMLP_WIDTH)


def conformer_conv(z, w_dw, b_dw, ln_g, ln_b, w_pw, b_pw):
    a, g = jnp.split(z, 2, axis=-1)
    y = a * jax.nn.sigmoid(g)
    y = lax.conv_general_dilated(
        y, w_dw[:, None, :], window_strides=(1,),
        padding=[(CONV_K // 2, CONV_K // 2)],
        dimension_numbers=('NWC', 'WIO', 'NWC'),
        feature_group_count=CONV_WIDTH) + b_dw
    y = jax.nn.silu(layer_norm(y, ln_g, ln_b))
    return y @ w_pw + b_pw


def token_mixers(p, attn_o, gmlp_ln_g, gmlp_ln_b, gmlp_w_s, gmlp_b_s,
                 conv_w_dw, conv_b_dw, conv_ln_g, conv_ln_b, conv_w_pw, conv_b_pw, w_out):
    zg = p[..., QKV_WIDTH:QKV_WIDTH + 2 * GMLP_WIDTH]
    zc = p[..., QKV_WIDTH + 2 * GMLP_WIDTH:]
    y_g = chunk_gmlp(zg, gmlp_ln_g, gmlp_ln_b, gmlp_w_s, gmlp_b_s)
    y_c = conformer_conv(zc, conv_w_dw, conv_b_dw, conv_ln_g, conv_ln_b, conv_w_pw, conv_b_pw)
    return jnp.concatenate([attn_o, y_g, y_c], axis=-1) @ w_out


def peer_ffn(h, w_q, sub_keys, u_tab, v_tab):
    B, L, D = h.shape
    hb = h.reshape(-1, PEER_TOKEN_BLOCK, D)

    def block(xb):
        T = xb.shape[0]
        q = (xb @ w_q).reshape(T, PEER_HEADS, 2, PEER_QDIM // 2)
        s = jnp.einsum('thpd,hpkd->thpk', q, sub_keys, preferred_element_type=jnp.float32)
        s_top, i_top = lax.top_k(s, PEER_TOPK)
        cand = s_top[:, :, 0, :, None] + s_top[:, :, 1, None, :]
        cand_idx = i_top[:, :, 0, :, None] * N_KEYS + i_top[:, :, 1, None, :]
        best, pos = lax.top_k(cand.reshape(T, PEER_HEADS, PEER_TOPK * PEER_TOPK), PEER_TOPK)
        experts = jnp.take_along_axis(cand_idx.reshape(T, PEER_HEADS, -1), pos, axis=-1)
        gate = jax.nn.softmax(best, axis=-1)
        act = jax.nn.gelu(jnp.einsum('thkd,td->thk', u_tab[experts], xb,
                                     preferred_element_type=jnp.float32))
        w = (gate * act).astype(xb.dtype)
        return jnp.einsum('thk,thkd->td', w, v_tab[experts])

    return lax.map(block, hb).reshape(B, L, D)


def setup_inputs(seed: int = 0) -> dict:
    key = jax.random.key(seed)
    ks = jax.random.split(key, 28)
    D = D_MODEL

    def nrm(i, shape, s):
        return jax.random.normal(ks[i], shape, jnp.float32) * s

    def gain(i, shape):
        return 1.0 + nrm(i, shape, 0.01)

    return {
        "x": nrm(0, (BATCH, SEQ, D), 1.0),
        "c": nrm(1, (BATCH, D), 1.0),
        "ctx": nrm(2, (BATCH, CTX_LEN, D), 1.0),
        "c_ctx": nrm(3, (D,), 1.0),
        "w_mod": nrm(4, (DEPTH, D, 6 * D), 0.5 * D ** -0.5),
        "b_mod": nrm(5, (DEPTH, 6 * D), 0.01),
        "w_in": nrm(6, (DEPTH, D, IN_WIDTH), D ** -0.5),
        "q_norm_g": gain(7, (DEPTH, HEAD_DIM)),
        "k_norm_g": gain(8, (DEPTH, HEAD_DIM)),
        "gmlp_ln_g": gain(9, (DEPTH, GMLP_WIDTH)),
        "gmlp_ln_b": nrm(10, (DEPTH, GMLP_WIDTH), 0.01),
        "gmlp_w_s": nrm(11, (DEPTH, GMLP_GROUPS, CHUNK, CHUNK), CHUNK ** -0.5),
        "gmlp_b_s": gain(12, (DEPTH, GMLP_GROUPS, CHUNK)),
        "conv_w_dw": nrm(13, (DEPTH, CONV_K, CONV_WIDTH), CONV_K ** -0.5),
        "conv_b_dw": nrm(14, (DEPTH, CONV_WIDTH), 0.01),
        "conv_ln_g": gain(15, (DEPTH, CONV_WIDTH)),
        "conv_ln_b": nrm(16, (DEPTH, CONV_WIDTH), 0.01),
        "conv_w_pw": nrm(17, (DEPTH, CONV_WIDTH, CONV_WIDTH), CONV_WIDTH ** -0.5),
        "conv_b_pw": nrm(18, (DEPTH, CONV_WIDTH), 0.01),
        "w_out": nrm(19, (DEPTH, MIX_WIDTH, D), BETA * MIX_WIDTH ** -0.5),
        "ln1_g": gain(20, (DEPTH, D)),
        "ln1_b": nrm(21, (DEPTH, D), 0.01),
        "ln2_g": gain(22, (DEPTH, D)),
        "ln2_b": nrm(23, (DEPTH, D), 0.01),
        "peer_w_q": nrm(24, (DEPTH, D, PEER_HEADS * PEER_QDIM), D ** -0.5),
        "peer_sub_keys": nrm(25, (DEPTH, PEER_HEADS, 2, N_KEYS, PEER_QDIM // 2), (PEER_QDIM // 2) ** -0.5),
        "peer_u": nrm(26, (DEPTH, N_EXPERTS, D), D ** -0.5),
        "peer_v": nrm(27, (DEPTH, N_EXPERTS, D), BETA),
    }


def reference(x, c, ctx, c_ctx, w_mod, b_mod, w_in, q_norm_g, k_norm_g,
              gmlp_ln_g, gmlp_ln_b, gmlp_w_s, gmlp_b_s,
              conv_w_dw, conv_b_dw, conv_ln_g, conv_ln_b, conv_w_pw, conv_b_pw,
              w_out, ln1_g, ln1_b, ln2_g, ln2_b,
              peer_w_q, peer_sub_keys, peer_u, peer_v):
    B, L, _ = x.shape
    rows = L // GRID_W
    row_ids = jnp.repeat(jnp.arange(rows, dtype=jnp.int32), GRID_W)
    col_ids = jnp.arange(rows * GRID_W, dtype=jnp.int32) % GRID_W

    x_lat, x_ctx = x, ctx
    for l in range(DEPTH):
        last = l == DEPTH - 1
        sh1, sc1, g1, sh2, sc2, g2 = modulation(c[:, None, :], w_mod[l], b_mod[l])
        csh1, csc1, cg1, csh2, csc2, cg2 = modulation(c_ctx, w_mod[l], b_mod[l])
        mix_params = (gmlp_ln_g[l], gmlp_ln_b[l], gmlp_w_s[l], gmlp_b_s[l],
                      conv_w_dw[l], conv_b_dw[l], conv_ln_g[l], conv_ln_b[l],
                      conv_w_pw[l], conv_b_pw[l], w_out[l])

        p_lat = modulate(x_lat, sh1, sc1) @ w_in[l]
        p_ctx = modulate(x_ctx, csh1, csc1) @ w_in[l]
        k_c, v_c = attn_keys_values(p_ctx, k_norm_g[l])
        k_l, v_l = attn_keys_values(p_lat, k_norm_g[l], row_ids, col_ids)
        q_l = attn_queries(p_lat, q_norm_g[l], row_ids, col_ids)
        k_all = jnp.concatenate([k_c, k_l], axis=1)
        v_all = jnp.concatenate([v_c, v_l], axis=1)
        o_lat = blocked_attention(q_l, k_all, v_all)
        y_lat = token_mixers(p_lat, o_lat, *mix_params)
        x_lat_new = layer_norm(ALPHA * x_lat + g1 * y_lat, ln1_g[l], ln1_b[l])

        f_lat = peer_ffn(modulate(x_lat_new, sh2, sc2), peer_w_q[l], peer_sub_keys[l], peer_u[l], peer_v[l])
        x_lat_new = layer_norm(ALPHA * x_lat_new + g2 * f_lat, ln2_g[l], ln2_b[l])

        if not last:
            q_c = attn_queries(p_ctx, q_norm_g[l])
            o_ctx = gqa_attend(q_c, k_c, v_c)
            y_ctx = token_mixers(p_ctx, o_ctx, *mix_params)
            x_ctx = layer_norm(ALPHA * x_ctx + cg1 * y_ctx, ln1_g[l], ln1_b[l])
            f_ctx = peer_ffn(modulate(x_ctx, csh2, csc2), peer_w_q[l], peer_sub_keys[l], peer_u[l], peer_v[l])
            x_ctx = layer_norm(ALPHA * x_ctx + cg2 * f_ctx, ln2_g[l], ln2_b[l])
        x_lat = x_lat_new
    return x_lat
```

```python
import functools
import math

import jax
import jax.numpy as jnp
import numpy as np
from jax import lax
from jax.experimental import pallas as pl
from jax.experimental.pallas import tpu as pltpu

F32 = jnp.float32
BF16 = jnp.bfloat16

EPS = 1e-6
GRID_W = 64
N_Q_HEADS = 8
N_KV_HEADS = 2
Q_PER_KV = N_Q_HEADS // N_KV_HEADS
HEAD_DIM = 64
ATTN_WIDTH = N_Q_HEADS * HEAD_DIM
KV_WIDTH = N_KV_HEADS * HEAD_DIM
QKV_WIDTH = ATTN_WIDTH + 2 * KV_WIDTH
ROPE_THETA = 10000.0
CHUNK = 128
GMLP_GROUPS = 4
CONV_K = 31
CONV_HALO = 16
PEER_HEADS = 8
N_KEYS = 128
PEER_TOPK = 16
PEER_HALF = 128

VMEM_LIMIT = 56 * 1024 * 1024


def _cparams(sem):
    return pltpu.CompilerParams(dimension_semantics=sem, vmem_limit_bytes=VMEM_LIMIT)


def _gelu_tanh(x):
    return 0.5 * x * (1.0 + jnp.tanh(math.sqrt(2.0 / math.pi) * (x + 0.044715 * (x * x * x))))


def _ln(x):
    mu = jnp.mean(x, axis=-1, keepdims=True)
    xc = x - mu
    var = jnp.mean(xc * xc, axis=-1, keepdims=True)
    return xc * lax.rsqrt(var + EPS)


def _mod_kernel(c_ref, w_ref, b_ref, o_ref):
    c = c_ref[...]
    s = c * jax.nn.sigmoid(c)
    o_ref[...] = jnp.dot(s, w_ref[...], preferred_element_type=F32,
                         precision=lax.Precision.HIGHEST) + b_ref[...]


def _modulation(cond, w_mod, b_mod):
    R, D = cond.shape
    N = w_mod.shape[1]
    tn = 1024
    return pl.pallas_call(
        _mod_kernel,
        grid=(N // tn,),
        in_specs=[pl.BlockSpec((R, D), lambda j: (0, 0)),
                  pl.BlockSpec((D, tn), lambda j: (0, j)),
                  pl.BlockSpec((1, tn), lambda j: (0, j))],
        out_specs=pl.BlockSpec((R, tn), lambda j: (0, j)),
        out_shape=jax.ShapeDtypeStruct((R, N), F32),
        compiler_params=_cparams(("arbitrary",)),
        name="modulation",
    )(cond, w_mod, b_mod.reshape(1, N))


def _group_mean_sq(t, gmat):
    sq = t * t
    hi = sq.astype(BF16)
    lo = (sq - hi.astype(F32)).astype(BF16)
    return (jnp.dot(hi, gmat, preferred_element_type=F32)
            + jnp.dot(lo, gmat, preferred_element_type=F32))


def _rope(t, cos, sin_lo, sin_hi):
    w = t.shape[-1]
    up = pltpu.roll(t, w - 16, axis=1)
    dn = pltpu.roll(t, 16, axis=1)
    return t * cos + up * sin_lo + dn * sin_hi


def _inproj_kernel(x_ref, sh_ref, sc_ref, w_ref, gq_ref, gk_ref, cos_ref, slo_ref, shi_ref, gmat_ref,
                   q_ref, kt_ref, v_ref, zg_ref, zc_ref):
    h = _ln(x_ref[0]) * (1.0 + sc_ref[0]) + sh_ref[0]
    p = jnp.dot(h.astype(BF16), w_ref[...], preferred_element_type=F32)
    q = p[:, :ATTN_WIDTH]
    k = p[:, ATTN_WIDTH:ATTN_WIDTH + KV_WIDTH]
    v = p[:, ATTN_WIDTH + KV_WIDTH:QKV_WIDTH]
    gm_w = (p.shape[1] - QKV_WIDTH) // 2
    zg_ref[0] = p[:, QKV_WIDTH:QKV_WIDTH + gm_w]
    zc_ref[0] = p[:, QKV_WIDTH + gm_w:]
    v_ref[0] = v.astype(BF16)

    gmat = gmat_ref[...]
    cos, slo, shi = cos_ref[...], slo_ref[...], shi_ref[...]
    qn = q * lax.rsqrt(_group_mean_sq(q, gmat) + EPS) * gq_ref[...]
    rep = ATTN_WIDTH // KV_WIDTH
    qr = _rope(qn, jnp.concatenate([cos] * rep, axis=1), jnp.concatenate([slo] * rep, axis=1),
               jnp.concatenate([shi] * rep, axis=1))
    qr = (qr * (HEAD_DIM ** -0.5)).astype(BF16)
    for hh in range(N_Q_HEADS):
        q_ref[0, hh] = qr[:, hh * HEAD_DIM:(hh + 1) * HEAD_DIM]

    kn = k * lax.rsqrt(_group_mean_sq(k, gmat[:KV_WIDTH, :KV_WIDTH]) + EPS) * gk_ref[...]
    kt = _rope(kn, cos, slo, shi).T
    for g in range(N_KV_HEADS):
        kt_ref[0, g] = kt[g * HEAD_DIM:(g + 1) * HEAD_DIM, :].astype(BF16)


def _inproj(x, sh, sc, w_in_bf, gq, gk, cos, slo, shi, gmat, *, tile):
    B, L, D = x.shape
    W = w_in_bf.shape[1]
    gm_w = (W - QKV_WIDTH) // 2
    nt = L // tile
    bmap = lambda b, i: (b, 0, 0)
    cmap = lambda b, i: (0, 0)
    return pl.pallas_call(
        _inproj_kernel,
        grid=(B, nt),
        in_specs=[pl.BlockSpec((1, tile, D), lambda b, i: (b, i, 0)),
                  pl.BlockSpec((1, 1, D), bmap), pl.BlockSpec((1, 1, D), bmap),
                  pl.BlockSpec((D, W), cmap),
                  pl.BlockSpec((1, ATTN_WIDTH), cmap), pl.BlockSpec((1, KV_WIDTH), cmap),
                  pl.BlockSpec((tile, KV_WIDTH), lambda b, i: (i, 0)),
                  pl.BlockSpec((tile, KV_WIDTH), lambda b, i: (i, 0)),
                  pl.BlockSpec((tile, KV_WIDTH), lambda b, i: (i, 0)),
                  pl.BlockSpec((ATTN_WIDTH, ATTN_WIDTH), cmap)],
        out_specs=[pl.BlockSpec((1, N_Q_HEADS, tile, HEAD_DIM), lambda b, i: (b, 0, i, 0)),
                   pl.BlockSpec((1, N_KV_HEADS, HEAD_DIM, tile), lambda b, i: (b, 0, 0, i)),
                   pl.BlockSpec((1, tile, KV_WIDTH), lambda b, i: (b, i, 0)),
                   pl.BlockSpec((1, tile, gm_w), lambda b, i: (b, i, 0)),
                   pl.BlockSpec((1, tile, gm_w), lambda b, i: (b, i, 0))],
        out_shape=[jax.ShapeDtypeStruct((B, N_Q_HEADS, L, HEAD_DIM), BF16),
                   jax.ShapeDtypeStruct((B, N_KV_HEADS, HEAD_DIM, L), BF16),
                   jax.ShapeDtypeStruct((B, L, KV_WIDTH), BF16),
                   jax.ShapeDtypeStruct((B, L, gm_w), F32),
                   jax.ShapeDtypeStruct((B, L, gm_w), F32)],
        compiler_params=_cparams(("arbitrary", "arbitrary")),
        name="inproj",
    )(x, sh, sc, w_in_bf, gq, gk, cos, slo, shi, gmat)


def _attn_kernel(q_ref, kt_ref, v_ref, o_ref, *, tq, tk, nk):
    m_rows = Q_PER_KV * tq
    for g in range(N_KV_HEADS):
        qg = q_ref[0, Q_PER_KV * g:Q_PER_KV * (g + 1)].reshape(m_rows, HEAD_DIM)

        def body(c, carry):
            m, l, acc = carry
            off = pl.multiple_of(c * tk, tk)
            s = jnp.dot(qg, kt_ref[0, g, :, pl.ds(off, tk)], preferred_element_type=F32)
            m_new = jnp.maximum(m, jnp.max(s, axis=-1, keepdims=True))
            alpha = jnp.exp(m - m_new)
            p = jnp.exp(s - m_new)
            l = alpha * l + jnp.sum(p, axis=-1, keepdims=True)
            acc = alpha * acc + jnp.dot(p.astype(BF16), v_ref[0, pl.ds(off, tk), :],
                                        preferred_element_type=F32)
            return m_new, l, acc

        m0 = jnp.full((m_rows, 1), -jnp.inf, F32)
        l0 = jnp.zeros((m_rows, 1), F32)
        a0 = jnp.zeros((m_rows, KV_WIDTH), F32)
        _, l, acc = lax.fori_loop(0, nk, body, (m0, l0, a0))
        og = acc[:, g * HEAD_DIM:(g + 1) * HEAD_DIM] / l
        for r in range(Q_PER_KV):
            hh = Q_PER_KV * g + r
            o_ref[0, :, hh * HEAD_DIM:(hh + 1) * HEAD_DIM] = og[r * tq:(r + 1) * tq].astype(o_ref.dtype)


def _attention(q, kt, v, *, tq, tk):
    B, _, L, _ = q.shape
    LK = kt.shape[-1]
    return pl.pallas_call(
        functools.partial(_attn_kernel, tq=tq, tk=tk, nk=LK // tk),
        grid=(B, L // tq),
        in_specs=[pl.BlockSpec((1, N_Q_HEADS, tq, HEAD_DIM), lambda b, i: (b, 0, i, 0)),
                  pl.BlockSpec((1, N_KV_HEADS, HEAD_DIM, LK), lambda b, i: (b, 0, 0, 0)),
                  pl.BlockSpec((1, LK, KV_WIDTH), lambda b, i: (b, 0, 0))],
        out_specs=pl.BlockSpec((1, tq, ATTN_WIDTH), lambda b, i: (b, i, 0)),
        out_shape=jax.ShapeDtypeStruct((B, L, ATTN_WIDTH), BF16),
        compiler_params=_cparams(("arbitrary", "arbitrary")),
        name="attention",
    )(q, kt, v)


def _mix_kernel(x_ref, o_ref, zg_ref, zc_ref, zp_ref, zn_ref, g1_ref, sh2_ref, sc2_ref,
                gln_g_ref, gln_b_ref, ws_ref, bs_ref, wdw_ref, bdw_ref, cln_g_ref, cln_b_ref,
                wpw_ref, bpw_ref, wout_ref, ln1g_ref, ln1b_ref,
                x1_ref, h2_ref, ybuf_ref, *, tile, alpha):
    i = pl.program_id(1)
    nt = pl.num_programs(1)
    gw = zg_ref.shape[-1] // 2
    cw = zc_ref.shape[-1] // 2

    z = _gelu_tanh(zg_ref[0])
    u, v = z[:, :gw], z[:, gw:]
    vn = (_ln(v) * gln_g_ref[...] + gln_b_ref[...]).astype(BF16)
    grp = lax.broadcasted_iota(jnp.int32, (CHUNK, gw), 1) // (gw // GMLP_GROUPS)
    sg = []
    for n in range(tile // CHUNK):
        vc = vn[n * CHUNK:(n + 1) * CHUNK]
        s = jnp.zeros((CHUNK, gw), F32)
        for g in range(GMLP_GROUPS):
            s = jnp.where(grp == g, jnp.dot(ws_ref[g], vc, preferred_element_type=F32), s)
        sg.append(s + bs_ref[...])
    y_g = u * jnp.concatenate(sg, axis=0)

    def glu(zz):
        return zz[:, :cw] * jax.nn.sigmoid(zz[:, cw:])

    ybuf_ref[0:CONV_HALO] = glu(zp_ref[0]) * (i > 0).astype(F32)
    ybuf_ref[CONV_HALO:CONV_HALO + tile] = glu(zc_ref[0])
    ybuf_ref[CONV_HALO + tile:2 * CONV_HALO + tile] = glu(zn_ref[0]) * (i < nt - 1).astype(F32)
    base = CONV_HALO - CONV_K // 2
    conv = jnp.zeros((tile, cw), F32)
    for kk in range(CONV_K):
        conv = conv + ybuf_ref[base + kk:base + kk + tile, :] * wdw_ref[kk:kk + 1, :]
    conv = conv + bdw_ref[...]
    t = _ln(conv) * cln_g_ref[...] + cln_b_ref[...]
    t = t * jax.nn.sigmoid(t)
    y_c = jnp.dot(t.astype(BF16), wpw_ref[...], preferred_element_type=F32) + bpw_ref[...]

    aw = o_ref.shape[-1]
    y = (jnp.dot(o_ref[0], wout_ref[0:aw, :], preferred_element_type=F32)
         + jnp.dot(y_g.astype(BF16), wout_ref[aw:aw + gw, :], preferred_element_type=F32)
         + jnp.dot(y_c.astype(BF16), wout_ref[aw + gw:, :], preferred_element_type=F32))
    x1 = _ln(alpha * x_ref[0] + g1_ref[0] * y) * ln1g_ref[...] + ln1b_ref[...]
    x1_ref[0] = x1
    h2_ref[0] = _ln(x1) * (1.0 + sc2_ref[0]) + sh2_ref[0]


def _mixers(x, o, zg, zc, g1, sh2, sc2, prm, *, tile, alpha):
    B, L, D = x.shape
    nt = L // tile
    hb = tile // CONV_HALO
    nhb = L // CONV_HALO
    gw2, cw2 = zg.shape[-1], zc.shape[-1]
    tmap = lambda b, i: (b, i, 0)
    bmap = lambda b, i: (b, 0, 0)
    c2 = lambda b, i: (0, 0)
    c3 = lambda b, i: (0, 0, 0)
    in_specs = [pl.BlockSpec((1, tile, D), tmap),
                pl.BlockSpec((1, tile, o.shape[-1]), tmap),
                pl.BlockSpec((1, tile, gw2), tmap),
                pl.BlockSpec((1, tile, cw2), tmap),
                pl.BlockSpec((1, CONV_HALO, cw2), lambda b, i: (b, jnp.maximum(i * hb - 1, 0), 0)),
                pl.BlockSpec((1, CONV_HALO, cw2), lambda b, i: (b, jnp.minimum((i + 1) * hb, nhb - 1), 0)),
                pl.BlockSpec((1, 1, D), bmap), pl.BlockSpec((1, 1, D), bmap), pl.BlockSpec((1, 1, D), bmap)]
    for a in prm:
        in_specs.append(pl.BlockSpec(a.shape, c2 if a.ndim == 2 else c3))
    return pl.pallas_call(
        functools.partial(_mix_kernel, tile=tile, alpha=alpha),
        grid=(B, nt),
        in_specs=in_specs,
        out_specs=[pl.BlockSpec((1, tile, D), tmap), pl.BlockSpec((1, tile, D), tmap)],
        out_shape=[jax.ShapeDtypeStruct((B, L, D), F32), jax.ShapeDtypeStruct((B, L, D), F32)],
        scratch_shapes=[pltpu.VMEM((tile + 2 * CONV_HALO, cw2 // 2), F32)],
        compiler_params=_cparams(("arbitrary", "arbitrary")),
        name="mixers",
    )(x, o, zg, zc, zc, zc, g1, sh2, sc2, *prm)


def _oddeven_merge_sort_pairs(n):
    pairs = []
    p = 1
    while p < n:
        k = p
        while k >= 1:
            for j in range(k % p, n - k, 2 * k):
                for i in range(min(k, n - j - k)):
                    if (i + j) // (2 * p) == (i + j + k) // (2 * p):
                        pairs.append((i + j, i + j + k))
            k //= 2
        p *= 2
    return pairs


_SORT16 = _oddeven_merge_sort_pairs(16)
_BITONIC16 = [(i, i + d) for d in (8, 4, 2, 1) for i in range(16) if (i & d) == 0]


def _before(a, ia, b, ib):
    return (a > b) | ((a == b) & (ia < ib))


def _cmpx(vals, keys, pay, i, j):
    keep = _before(vals[i], keys[i], vals[j], keys[j])
    vals[i], vals[j] = jnp.where(keep, vals[i], vals[j]), jnp.where(keep, vals[j], vals[i])
    keys[i], keys[j] = jnp.where(keep, keys[i], keys[j]), jnp.where(keep, keys[j], keys[i])
    if pay is not None:
        pay[i], pay[j] = jnp.where(keep, pay[i], pay[j]), jnp.where(keep, pay[j], pay[i])


def _top16_merge(av, ak, ap, bv, bk, bp):
    n = len(av)
    ov, ok, op = [], [], []
    for i in range(n):
        keep = _before(av[i], ak[i], bv[n - 1 - i], bk[n - 1 - i])
        ov.append(jnp.where(keep, av[i], bv[n - 1 - i]))
        ok.append(jnp.where(keep, ak[i], bk[n - 1 - i]))
        if ap is not None:
            op.append(jnp.where(keep, ap[i], bp[n - 1 - i]))
    return ov, ok, (op if ap is not None else None)


def _stage2_candidates():
    return [(a, b) for a in range(PEER_TOPK) for b in range(PEER_TOPK) if (a + 1) * (b + 1) <= PEER_TOPK]


def _retr_kernel(h_ref, wq_ref, keys_ref, gate_ref, eidx_ref, q_s, sv_s, si_s, *, tile):
    q_s[...] = jnp.dot(h_ref[...].astype(BF16), wq_ref[...], preferred_element_type=F32)
    sub = lax.broadcasted_iota(jnp.int32, (8, tile), 0)
    sv_s[...] = jnp.zeros(sv_s.shape, F32)
    si_s[...] = jnp.zeros(si_s.shape, jnp.int32)

    def stage1(hp, _):
        qh = q_s[:, pl.ds(pl.multiple_of(hp * PEER_HALF, PEER_HALF), PEER_HALF)].astype(BF16)
        st = lax.dot_general(keys_ref[hp], qh, (((1,), (1,)), ((), ())),
                             preferred_element_type=F32)
        st = st.reshape(N_KEYS // 8, 8, tile)
        vals = [st[i] for i in range(16)]
        keys = [sub + 8 * i for i in range(16)]
        for (i, j) in _SORT16:
            _cmpx(vals, keys, None, i, j)
        for shift in (4, 2, 1):
            bv = [pltpu.roll(t, shift, axis=0) for t in vals]
            bk = [pltpu.roll(t, shift, axis=0) for t in keys]
            vals, keys, _ = _top16_merge(vals, keys, None, bv, bk, None)
            for (i, j) in _BITONIC16:
                _cmpx(vals, keys, None, i, j)
        head = hp // 2
        half = hp % 2
        for i in range(16):
            sv_s[half, i] = jnp.where(sub == head, vals[i], sv_s[half, i])
            si_s[half, i] = jnp.where(sub == head, keys[i], si_s[half, i])
        return 0

    lax.fori_loop(0, 2 * PEER_HEADS, stage1, 0)

    cands = _stage2_candidates()
    n_pad = 64
    vals, poss, exps = [], [], []
    for (a, b) in cands:
        vals.append(sv_s[0, a] + sv_s[1, b])
        poss.append(jnp.full((8, tile), a * PEER_TOPK + b, jnp.int32))
        exps.append(si_s[0, a] * N_KEYS + si_s[1, b])
    for t in range(n_pad - len(cands)):
        vals.append(jnp.full((8, tile), -jnp.inf, F32))
        poss.append(jnp.full((8, tile), PEER_TOPK * PEER_TOPK + t, jnp.int32))
        exps.append(jnp.zeros((8, tile), jnp.int32))
    groups = []
    for g in range(n_pad // 16):
        gv, gk, gp = vals[16 * g:16 * g + 16], poss[16 * g:16 * g + 16], exps[16 * g:16 * g + 16]
        for (i, j) in _SORT16:
            _cmpx(gv, gk, gp, i, j)
        groups.append((gv, gk, gp))
    while len(groups) > 1:
        nxt = []
        for g in range(0, len(groups), 2):
            mv, mk, mp = _top16_merge(*groups[g], *groups[g + 1])
            if len(groups) > 2:
                for (i, j) in _BITONIC16:
                    _cmpx(mv, mk, mp, i, j)
            nxt.append((mv, mk, mp))
        groups = nxt
    best, _, experts = groups[0]

    mx = functools.reduce(jnp.maximum, best)
    ex = [jnp.exp(b - mx) for b in best]
    inv = 1.0 / functools.reduce(lambda a, b: a + b, ex)
    for kk in range(PEER_TOPK):
        gate_ref[kk] = ex[kk] * inv
        eidx_ref[kk] = experts[kk]


def _retrieval(h2, wq_bf, keys_bf, *, tile):
    N, D = h2.shape
    QW = wq_bf.shape[1]
    return pl.pallas_call(
        functools.partial(_retr_kernel, tile=tile),
        grid=(N // tile,),
        in_specs=[pl.BlockSpec((tile, D), lambda i: (i, 0)),
                  pl.BlockSpec((D, QW), lambda i: (0, 0)),
                  pl.BlockSpec(keys_bf.shape, lambda i: (0, 0, 0))],
        out_specs=[pl.BlockSpec((PEER_TOPK, PEER_HEADS, tile), lambda i: (0, 0, i)),
                   pl.BlockSpec((PEER_TOPK, PEER_HEADS, tile), lambda i: (0, 0, i))],
        out_shape=[jax.ShapeDtypeStruct((PEER_TOPK, PEER_HEADS, N), F32),
                   jax.ShapeDtypeStruct((PEER_TOPK, PEER_HEADS, N), jnp.int32)],
        scratch_shapes=[pltpu.VMEM((tile, QW), F32),
                        pltpu.VMEM((2, PEER_TOPK, PEER_HEADS, tile), F32),
                        pltpu.VMEM((2, PEER_TOPK, PEER_HEADS, tile), jnp.int32)],
        compiler_params=_cparams(("arbitrary",)),
        name="retrieval",
    )(h2, wq_bf, keys_bf)


def _expert_kernel(idx0_ref, idxn_ref, tab_ref, h_ref, gate_ref, x1_ref, g2_ref, lng_ref, lnb_ref,
                   o_ref, buf_ref, sem_ref, *, gtok, alpha):
    i = pl.program_id(0)
    n = pl.num_programs(0)
    rows = gtok * PEER_HEADS * PEER_TOPK
    slot = i % 2

    def issue(idx_ref, s):
        def body(r, _):
            e = idx_ref[0, 0, r]
            pltpu.make_async_copy(tab_ref.at[pl.ds(e, 1), :], buf_ref.at[s, pl.ds(r, 1), :],
                                  sem_ref.at[s]).start()
            return 0
        lax.fori_loop(0, rows, body, 0, unroll=8)

    @pl.when(i == 0)
    def _():
        issue(idx0_ref, 0)

    @pl.when(i + 1 < n)
    def _():
        issue(idxn_ref, 1 - slot)

    pltpu.make_async_copy(tab_ref.at[pl.ds(0, rows), :], buf_ref.at[slot], sem_ref.at[slot]).wait()

    per_tok = PEER_HEADS * PEER_TOPK
    gates = gate_ref[0]
    outs = []
    for j in range(gtok):
        r = buf_ref[slot, j * per_tok:(j + 1) * per_tok, :]
        u = pltpu.bitcast(r & jnp.uint32(0xFFFF0000), F32)
        v = pltpu.bitcast(r << 16, F32)
        d = jnp.sum(u * h_ref[j:j + 1, :], axis=-1, keepdims=True)
        w = gates[:, j:j + 1] * _gelu_tanh(d)
        outs.append(jnp.sum(v * w, axis=0, keepdims=True))
    f = jnp.concatenate(outs, axis=0)
    o_ref[...] = _ln(alpha * x1_ref[...] + g2_ref[0] * f) * lng_ref[...] + lnb_ref[...]


def _experts(eidx, gate, tab, h2, x1, g2, lng, lnb, *, gtok, seq, alpha):
    N, D = h2.shape
    S = N // gtok
    rows = gtok * PEER_HEADS * PEER_TOPK
    steps_per_batch = seq // gtok
    smem = pltpu.MemorySpace.SMEM
    return pl.pallas_call(
        functools.partial(_expert_kernel, gtok=gtok, alpha=alpha),
        grid=(S,),
        in_specs=[pl.BlockSpec((1, 1, rows), lambda i: (0, 0, 0), memory_space=smem),
                  pl.BlockSpec((1, 1, rows), lambda i: (jnp.minimum(i + 1, S - 1), 0, 0), memory_space=smem),
                  pl.BlockSpec(memory_space=pl.ANY),
                  pl.BlockSpec((gtok, D), lambda i: (i, 0)),
                  pl.BlockSpec((1, PEER_HEADS * PEER_TOPK, gtok), lambda i: (i, 0, 0)),
                  pl.BlockSpec((gtok, D), lambda i: (i, 0)),
                  pl.BlockSpec((1, 1, D), lambda i: (i // steps_per_batch, 0, 0)),
                  pl.BlockSpec((1, D), lambda i: (0, 0)),
                  pl.BlockSpec((1, D), lambda i: (0, 0))],
        out_specs=pl.BlockSpec((gtok, D), lambda i: (i, 0)),
        out_shape=jax.ShapeDtypeStruct((N, D), F32),
        scratch_shapes=[pltpu.VMEM((2, rows, D), jnp.uint32),
                        pltpu.SemaphoreType.DMA((2,))],
        compiler_params=_cparams(("arbitrary",)),
        name="experts",
    )(eidx, eidx, tab, h2, gate, x1, g2, lng, lnb)


def _rope_tables(L):
    half = HEAD_DIM // 4
    t = np.arange(L)
    freqs = ROPE_THETA ** (-np.arange(half, dtype=np.float32) / half)
    lane = np.arange(HEAD_DIM)
    pos = np.where(lane[None, :] < HEAD_DIM // 2, (t // GRID_W)[:, None], (t % GRID_W)[:, None]).astype(np.float32)
    ang = pos * freqs[lane % half][None, :].astype(np.float32)
    cos, sin = np.cos(ang), np.sin(ang)
    first = ((lane % (2 * half)) < half)[None, :]
    slo = np.where(first, -sin, 0.0)
    shi = np.where(first, 0.0, sin)
    tile = lambda a: jnp.asarray(np.tile(a, (1, N_KV_HEADS)), F32)
    return tile(cos), tile(slo), tile(shi)


def _pack_tables(u, v):
    ub = lax.bitcast_convert_type(u.astype(BF16), jnp.uint16).astype(jnp.uint32)
    vb = lax.bitcast_convert_type(v.astype(BF16), jnp.uint16).astype(jnp.uint32)
    return (ub << 16) | vb


def _pick(n, prefs):
    for p in prefs:
        if n % p == 0:
            return p
    return n


def kernel(x, c, ctx, c_ctx, w_mod, b_mod, w_in, q_norm_g, k_norm_g, gmlp_ln_g, gmlp_ln_b, gmlp_w_s, gmlp_b_s,
           conv_w_dw, conv_b_dw, conv_ln_g, conv_ln_b, conv_w_pw, conv_b_pw, w_out, ln1_g, ln1_b, ln2_g, ln2_b,
           peer_w_q, peer_sub_keys, peer_u, peer_v):
    B, L, D = x.shape
    LC = ctx.shape[1]
    depth = w_mod.shape[0]
    alpha = (2 * depth) ** 0.25
    gw = gmlp_ln_g.shape[-1]

    cos, slo, shi = _rope_tables(L)
    ones_c = jnp.ones((LC, KV_WIDTH), F32)
    zeros_c = jnp.zeros((LC, KV_WIDTH), F32)
    gidx = np.arange(ATTN_WIDTH) // HEAD_DIM
    gmat = jnp.asarray((gidx[:, None] == gidx[None, :]) / HEAD_DIM, BF16)

    cond = jnp.concatenate([c, c_ctx[None, :], jnp.zeros((-(B + 1) % 8, D), F32)], axis=0)

    t_in = _pick(L, (512, 256, 128))
    t_mix = _pick(L, (256, 128))
    t_inc = _pick(LC, (512, 256, 128))
    t_mixc = _pick(LC, (256, 128))
    tk_lat = _pick(L + LC, (768, 512, 384, 256, 128))
    tk_ctx = _pick(LC, (512, 256, 128))
    gtok = 8

    def peer(h2, x1, g2, l, seq, wq_bf, keys_bf, tab):
        n = h2.shape[0] * h2.shape[1]
        gate, eidx = _retrieval(h2.reshape(n, D), wq_bf, keys_bf, tile=_pick(n, (256, 128)))
        rows = PEER_TOPK * PEER_HEADS
        eidx = eidx.reshape(rows, n).T.reshape(n // gtok, 1, gtok * rows)
        gate = gate.reshape(rows, n // gtok, gtok).transpose(1, 0, 2)
        out = _experts(eidx, gate, tab, h2.reshape(n, D), x1.reshape(n, D), g2,
                       ln2_g[l].reshape(1, D), ln2_b[l].reshape(1, D), gtok=gtok, seq=seq, alpha=alpha)
        return out.reshape(h2.shape)

    x_lat, x_ctx = x, ctx
    for l in range(depth):
        last = l == depth - 1
        mod = _modulation(cond, w_mod[l], b_mod[l])
        m_lat = [mod[:B, k * D:(k + 1) * D].reshape(B, 1, D) for k in range(6)]
        m_ctx = [jnp.broadcast_to(mod[B:B + 1, k * D:(k + 1) * D].reshape(1, 1, D), (B, 1, D)) for k in range(6)]

        w_in_bf = w_in[l].astype(BF16)
        gq = jnp.tile(q_norm_g[l], N_Q_HEADS).reshape(1, ATTN_WIDTH)
        gk = jnp.tile(k_norm_g[l], N_KV_HEADS).reshape(1, KV_WIDTH)
        mix_prm = (gmlp_ln_g[l].reshape(1, gw), gmlp_ln_b[l].reshape(1, gw),
                   gmlp_w_s[l].astype(BF16),
                   jnp.repeat(gmlp_b_s[l].T, gw // GMLP_GROUPS, axis=1),
                   conv_w_dw[l], conv_b_dw[l].reshape(1, -1), conv_ln_g[l].reshape(1, -1),
                   conv_ln_b[l].reshape(1, -1), conv_w_pw[l].astype(BF16), conv_b_pw[l].reshape(1, -1),
                   w_out[l].astype(BF16), ln1_g[l].reshape(1, D), ln1_b[l].reshape(1, D))
        wq_bf = peer_w_q[l].astype(BF16)
        keys_bf = peer_sub_keys[l].reshape(2 * PEER_HEADS, N_KEYS, PEER_HALF).astype(BF16)
        tab = _pack_tables(peer_u[l], peer_v[l])

        q_l, kt_l, v_l, zg_l, zc_l = _inproj(x_lat, m_lat[0], m_lat[1], w_in_bf, gq, gk, cos, slo, shi, gmat,
                                              tile=t_in)
        q_c, kt_c, v_c, zg_c, zc_c = _inproj(x_ctx, m_ctx[0], m_ctx[1], w_in_bf, gq, gk, ones_c, zeros_c, zeros_c,
                                              gmat, tile=t_inc)
        kt_all = jnp.concatenate([kt_c, kt_l], axis=3)
        v_all = jnp.concatenate([v_c, v_l], axis=1)
        o_lat = _attention(q_l, kt_all, v_all, tq=128, tk=tk_lat)
        x1, h2 = _mixers(x_lat, o_lat, zg_l, zc_l, m_lat[2], m_lat[3], m_lat[4], mix_prm, tile=t_mix, alpha=alpha)
        x_lat_new = peer(h2, x1, m_lat[5], l, L, wq_bf, keys_bf, tab)

        if not last:
            o_ctx = _attention(q_c, kt_c, v_c, tq=128, tk=tk_ctx)
            x1c, h2c = _mixers(x_ctx, o_ctx, zg_c, zc_c, m_ctx[2], m_ctx[3], m_ctx[4], mix_prm, tile=t_mixc,
                               alpha=alpha)
            x_ctx = peer(h2c, x1c, m_ctx[5], l, LC, wq_bf, keys_bf, tab)
        x_lat = x_lat_new
    return x_lat
```

```python
import functools
import math

import jax
import jax.numpy as jnp
import numpy as np
from jax import lax
from jax.experimental import pallas as pl
from jax.experimental.pallas import tpu as pltpu

F32 = jnp.float32
BF16 = jnp.bfloat16

EPS = 1e-6
GRID_W = 64
N_Q_HEADS = 8
N_KV_HEADS = 2
Q_PER_KV = N_Q_HEADS // N_KV_HEADS
HEAD_DIM = 64
ATTN_WIDTH = N_Q_HEADS * HEAD_DIM
KV_WIDTH = N_KV_HEADS * HEAD_DIM
QKV_WIDTH = ATTN_WIDTH + 2 * KV_WIDTH
ROPE_THETA = 10000.0
CHUNK = 128
GMLP_GROUPS = 4
CONV_K = 31
CONV_HALO = 16
PEER_HEADS = 8
N_KEYS = 128
PEER_TOPK = 16
PEER_HALF = 128

VMEM_LIMIT = 56 * 1024 * 1024


def _cparams(sem):
    return pltpu.CompilerParams(dimension_semantics=sem, vmem_limit_bytes=VMEM_LIMIT)


def _gelu_tanh(x):
    return 0.5 * x * (1.0 + jnp.tanh(math.sqrt(2.0 / math.pi) * (x + 0.044715 * (x * x * x))))


def _ln(x):
    mu = jnp.mean(x, axis=-1, keepdims=True)
    xc = x - mu
    var = jnp.mean(xc * xc, axis=-1, keepdims=True)
    return xc * lax.rsqrt(var + EPS)


def _mod_kernel(c_ref, w_ref, b_ref, o_ref):
    c = c_ref[...]
    s = c * jax.nn.sigmoid(c)
    o_ref[...] = jnp.dot(s, w_ref[...], preferred_element_type=F32,
                         precision=lax.Precision.HIGHEST) + b_ref[...]


def _modulation(cond, w_mod, b_mod):
    R, D = cond.shape
    N = w_mod.shape[1]
    tn = 1024
    return pl.pallas_call(
        _mod_kernel,
        grid=(N // tn,),
        in_specs=[pl.BlockSpec((R, D), lambda j: (0, 0)),
                  pl.BlockSpec((D, tn), lambda j: (0, j)),
                  pl.BlockSpec((1, tn), lambda j: (0, j))],
        out_specs=pl.BlockSpec((R, tn), lambda j: (0, j)),
        out_shape=jax.ShapeDtypeStruct((R, N), F32),
        compiler_params=_cparams(("arbitrary",)),
        name="modulation",
    )(cond, w_mod, b_mod.reshape(1, N))


def _group_mean_sq(t, gmat):
    sq = t * t
    hi = sq.astype(BF16)
    lo = (sq - hi.astype(F32)).astype(BF16)
    return (jnp.dot(hi, gmat, preferred_element_type=F32)
            + jnp.dot(lo, gmat, preferred_element_type=F32))


def _rope(t, cos, sin_lo, sin_hi):
    w = t.shape[-1]
    up = pltpu.roll(t, w - 16, axis=1)
    dn = pltpu.roll(t, 16, axis=1)
    return t * cos + up * sin_lo + dn * sin_hi


def _inproj_kernel(x_ref, sh_ref, sc_ref, w_ref, gq_ref, gk_ref, cos_ref, slo_ref, shi_ref, gmat_ref,
                   q_ref, kt_ref, v_ref, zg_ref, zc_ref):
    h = _ln(x_ref[0]) * (1.0 + sc_ref[0]) + sh_ref[0]
    p = jnp.dot(h.astype(BF16), w_ref[...], preferred_element_type=F32)
    q = p[:, :ATTN_WIDTH]
    k = p[:, ATTN_WIDTH:ATTN_WIDTH + KV_WIDTH]
    v = p[:, ATTN_WIDTH + KV_WIDTH:QKV_WIDTH]
    gm_w = (p.shape[1] - QKV_WIDTH) // 2
    zg_ref[0] = p[:, QKV_WIDTH:QKV_WIDTH + gm_w]
    zc_ref[0] = p[:, QKV_WIDTH + gm_w:]
    v_ref[0] = v.astype(BF16)

    gmat = gmat_ref[...]
    cos, slo, shi = cos_ref[...], slo_ref[...], shi_ref[...]
    qn = q * lax.rsqrt(_group_mean_sq(q, gmat) + EPS) * gq_ref[...]
    rep = ATTN_WIDTH // KV_WIDTH
    qr = _rope(qn, jnp.concatenate([cos] * rep, axis=1), jnp.concatenate([slo] * rep, axis=1),
               jnp.concatenate([shi] * rep, axis=1))
    qr = (qr * (HEAD_DIM ** -0.5 * math.log2(math.e))).astype(BF16)
    for hh in range(N_Q_HEADS):
        q_ref[0, hh] = qr[:, hh * HEAD_DIM:(hh + 1) * HEAD_DIM]

    kn = k * lax.rsqrt(_group_mean_sq(k, gmat[:KV_WIDTH, :KV_WIDTH]) + EPS) * gk_ref[...]
    kt = _rope(kn, cos, slo, shi).T
    for g in range(N_KV_HEADS):
        kt_ref[0, g] = kt[g * HEAD_DIM:(g + 1) * HEAD_DIM, :].astype(BF16)


def _inproj(x, sh, sc, w_in_bf, gq, gk, cos, slo, shi, gmat, *, tile):
    B, L, D = x.shape
    W = w_in_bf.shape[1]
    gm_w = (W - QKV_WIDTH) // 2
    nt = L // tile
    bmap = lambda b, i: (b, 0, 0)
    cmap = lambda b, i: (0, 0)
    return pl.pallas_call(
        _inproj_kernel,
        grid=(B, nt),
        in_specs=[pl.BlockSpec((1, tile, D), lambda b, i: (b, i, 0)),
                  pl.BlockSpec((1, 1, D), bmap), pl.BlockSpec((1, 1, D), bmap),
                  pl.BlockSpec((D, W), cmap),
                  pl.BlockSpec((1, ATTN_WIDTH), cmap), pl.BlockSpec((1, KV_WIDTH), cmap),
                  pl.BlockSpec((tile, KV_WIDTH), lambda b, i: (i, 0)),
                  pl.BlockSpec((tile, KV_WIDTH), lambda b, i: (i, 0)),
                  pl.BlockSpec((tile, KV_WIDTH), lambda b, i: (i, 0)),
                  pl.BlockSpec((ATTN_WIDTH, ATTN_WIDTH), cmap)],
        out_specs=[pl.BlockSpec((1, N_Q_HEADS, tile, HEAD_DIM), lambda b, i: (b, 0, i, 0)),
                   pl.BlockSpec((1, N_KV_HEADS, HEAD_DIM, tile), lambda b, i: (b, 0, 0, i)),
                   pl.BlockSpec((1, tile, KV_WIDTH), lambda b, i: (b, i, 0)),
                   pl.BlockSpec((1, tile, gm_w), lambda b, i: (b, i, 0)),
                   pl.BlockSpec((1, tile, gm_w), lambda b, i: (b, i, 0))],
        out_shape=[jax.ShapeDtypeStruct((B, N_Q_HEADS, L, HEAD_DIM), BF16),
                   jax.ShapeDtypeStruct((B, N_KV_HEADS, HEAD_DIM, L), BF16),
                   jax.ShapeDtypeStruct((B, L, KV_WIDTH), BF16),
                   jax.ShapeDtypeStruct((B, L, gm_w), F32),
                   jax.ShapeDtypeStruct((B, L, gm_w), F32)],
        compiler_params=_cparams(("arbitrary", "arbitrary")),
        name="inproj",
    )(x, sh, sc, w_in_bf, gq, gk, cos, slo, shi, gmat)


def _attn_kernel(q_ref, kt_ref, vx_ref, o_ref, *, tq, tk, nk):
    m_rows = Q_PER_KV * tq
    qs = [q_ref[0, Q_PER_KV * g:Q_PER_KV * (g + 1)].reshape(m_rows, HEAD_DIM) for g in range(N_KV_HEADS)]

    def body(c, carry):
        off = pl.multiple_of(c * tk, tk)
        out = []
        for g in range(N_KV_HEADS):
            m, acc = carry[g]
            s = jnp.dot(qs[g], kt_ref[0, g, :, pl.ds(off, tk)], preferred_element_type=F32)
            m_new = jnp.maximum(m, jnp.max(s, axis=-1, keepdims=True))
            p = jnp.exp2(s - m_new).astype(BF16)
            acc = jnp.exp2(m - m_new) * acc + jnp.dot(p, vx_ref[0, g, pl.ds(off, tk), :],
                                                      preferred_element_type=F32)
            out.append((m_new, acc))
        return tuple(out)

    init = tuple((jnp.full((m_rows, 1), -1e30, F32), jnp.zeros((m_rows, 2 * HEAD_DIM), F32))
                 for _ in range(N_KV_HEADS))
    res = lax.fori_loop(0, nk, body, init)
    for g in range(N_KV_HEADS):
        acc = res[g][1]
        og = acc / pltpu.roll(acc, HEAD_DIM, axis=1)
        for r in range(Q_PER_KV):
            hh = Q_PER_KV * g + r
            o_ref[0, :, hh * HEAD_DIM:(hh + 1) * HEAD_DIM] = og[r * tq:(r + 1) * tq, :HEAD_DIM].astype(o_ref.dtype)


def _attention(q, kt, v, *, tq, tk):
    B, _, L, _ = q.shape
    LK = kt.shape[-1]
    ones = jnp.ones((B, LK, HEAD_DIM), v.dtype)
    vx = jnp.stack([jnp.concatenate([v[..., g * HEAD_DIM:(g + 1) * HEAD_DIM], ones], axis=-1)
                    for g in range(N_KV_HEADS)], axis=1)
    return pl.pallas_call(
        functools.partial(_attn_kernel, tq=tq, tk=tk, nk=LK // tk),
        grid=(B, L // tq),
        in_specs=[pl.BlockSpec((1, N_Q_HEADS, tq, HEAD_DIM), lambda b, i: (b, 0, i, 0)),
                  pl.BlockSpec((1, N_KV_HEADS, HEAD_DIM, LK), lambda b, i: (b, 0, 0, 0)),
                  pl.BlockSpec((1, N_KV_HEADS, LK, 2 * HEAD_DIM), lambda b, i: (b, 0, 0, 0))],
        out_specs=pl.BlockSpec((1, tq, ATTN_WIDTH), lambda b, i: (b, i, 0)),
        out_shape=jax.ShapeDtypeStruct((B, L, ATTN_WIDTH), BF16),
        compiler_params=_cparams(("arbitrary", "arbitrary")),
        name="attention",
    )(q, kt, vx)


def _mix_kernel(x_ref, o_ref, zg_ref, zc_ref, zp_ref, zn_ref, g1_ref, sh2_ref, sc2_ref,
                gln_g_ref, gln_b_ref, ws_ref, bs_ref, wdw_ref, bdw_ref, cln_g_ref, cln_b_ref,
                wpw_ref, bpw_ref, wout_ref, ln1g_ref, ln1b_ref,
                x1_ref, h2_ref, ybuf_ref, *, tile, alpha):
    i = pl.program_id(1)
    nt = pl.num_programs(1)
    gw = zg_ref.shape[-1] // 2
    cw = zc_ref.shape[-1] // 2

    z = _gelu_tanh(zg_ref[0])
    u, v = z[:, :gw], z[:, gw:]
    vn = (_ln(v) * gln_g_ref[...] + gln_b_ref[...]).astype(BF16)
    grp = lax.broadcasted_iota(jnp.int32, (CHUNK, gw), 1) // (gw // GMLP_GROUPS)
    sg = []
    for n in range(tile // CHUNK):
        vc = vn[n * CHUNK:(n + 1) * CHUNK]
        s = jnp.zeros((CHUNK, gw), F32)
        for g in range(GMLP_GROUPS):
            s = jnp.where(grp == g, jnp.dot(ws_ref[g], vc, preferred_element_type=F32), s)
        sg.append(s + bs_ref[...])
    y_g = u * jnp.concatenate(sg, axis=0)

    def glu(zz):
        return zz[:, :cw] * jax.nn.sigmoid(zz[:, cw:])

    ybuf_ref[0:CONV_HALO] = glu(zp_ref[0]) * (i > 0).astype(F32)
    ybuf_ref[CONV_HALO:CONV_HALO + tile] = glu(zc_ref[0])
    ybuf_ref[CONV_HALO + tile:2 * CONV_HALO + tile] = glu(zn_ref[0]) * (i < nt - 1).astype(F32)
    base = CONV_HALO - CONV_K // 2
    conv = jnp.zeros((tile, cw), F32)
    for kk in range(CONV_K):
        conv = conv + ybuf_ref[base + kk:base + kk + tile, :] * wdw_ref[kk:kk + 1, :]
    conv = conv + bdw_ref[...]
    t = _ln(conv) * cln_g_ref[...] + cln_b_ref[...]
    t = t * jax.nn.sigmoid(t)
    y_c = jnp.dot(t.astype(BF16), wpw_ref[...], preferred_element_type=F32) + bpw_ref[...]

    aw = o_ref.shape[-1]
    y = (jnp.dot(o_ref[0], wout_ref[0:aw, :], preferred_element_type=F32)
         + jnp.dot(y_g.astype(BF16), wout_ref[aw:aw + gw, :], preferred_element_type=F32)
         + jnp.dot(y_c.astype(BF16), wout_ref[aw + gw:, :], preferred_element_type=F32))
    x1 = _ln(alpha * x_ref[0] + g1_ref[0] * y) * ln1g_ref[...] + ln1b_ref[...]
    x1_ref[0] = x1
    h2_ref[0] = _ln(x1) * (1.0 + sc2_ref[0]) + sh2_ref[0]


def _mixers(x, o, zg, zc, g1, sh2, sc2, prm, *, tile, alpha):
    B, L, D = x.shape
    nt = L // tile
    hb = tile // CONV_HALO
    nhb = L // CONV_HALO
    gw2, cw2 = zg.shape[-1], zc.shape[-1]
    tmap = lambda b, i: (b, i, 0)
    bmap = lambda b, i: (b, 0, 0)
    c2 = lambda b, i: (0, 0)
    c3 = lambda b, i: (0, 0, 0)
    in_specs = [pl.BlockSpec((1, tile, D), tmap),
                pl.BlockSpec((1, tile, o.shape[-1]), tmap),
                pl.BlockSpec((1, tile, gw2), tmap),
                pl.BlockSpec((1, tile, cw2), tmap),
                pl.BlockSpec((1, CONV_HALO, cw2), lambda b, i: (b, jnp.maximum(i * hb - 1, 0), 0)),
                pl.BlockSpec((1, CONV_HALO, cw2), lambda b, i: (b, jnp.minimum((i + 1) * hb, nhb - 1), 0)),
                pl.BlockSpec((1, 1, D), bmap), pl.BlockSpec((1, 1, D), bmap), pl.BlockSpec((1, 1, D), bmap)]
    for a in prm:
        in_specs.append(pl.BlockSpec(a.shape, c2 if a.ndim == 2 else c3))
    return pl.pallas_call(
        functools.partial(_mix_kernel, tile=tile, alpha=alpha),
        grid=(B, nt),
        in_specs=in_specs,
        out_specs=[pl.BlockSpec((1, tile, D), tmap), pl.BlockSpec((1, tile, D), tmap)],
        out_shape=[jax.ShapeDtypeStruct((B, L, D), F32), jax.ShapeDtypeStruct((B, L, D), F32)],
        scratch_shapes=[pltpu.VMEM((tile + 2 * CONV_HALO, cw2 // 2), F32)],
        compiler_params=_cparams(("arbitrary", "arbitrary")),
        name="mixers",
    )(x, o, zg, zc, zc, zc, g1, sh2, sc2, *prm)


def _oddeven_merge_sort_pairs(n):
    pairs = []
    p = 1
    while p < n:
        k = p
        while k >= 1:
            for j in range(k % p, n - k, 2 * k):
                for i in range(min(k, n - j - k)):
                    if (i + j) // (2 * p) == (i + j + k) // (2 * p):
                        pairs.append((i + j, i + j + k))
            k //= 2
        p *= 2
    return pairs


_SORT16 = _oddeven_merge_sort_pairs(16)
_BITONIC16 = [(i, i + d) for d in (8, 4, 2, 1) for i in range(16) if (i & d) == 0]


def _before(a, ia, b, ib):
    return (a > b) | ((a == b) & (ia < ib))


def _cmpx(vals, keys, pay, i, j):
    keep = _before(vals[i], keys[i], vals[j], keys[j])
    vals[i], vals[j] = jnp.where(keep, vals[i], vals[j]), jnp.where(keep, vals[j], vals[i])
    keys[i], keys[j] = jnp.where(keep, keys[i], keys[j]), jnp.where(keep, keys[j], keys[i])
    if pay is not None:
        pay[i], pay[j] = jnp.where(keep, pay[i], pay[j]), jnp.where(keep, pay[j], pay[i])


def _top16_merge(av, ak, ap, bv, bk, bp):
    n = len(av)
    ov, ok, op = [], [], []
    for i in range(n):
        keep = _before(av[i], ak[i], bv[n - 1 - i], bk[n - 1 - i])
        ov.append(jnp.where(keep, av[i], bv[n - 1 - i]))
        ok.append(jnp.where(keep, ak[i], bk[n - 1 - i]))
        if ap is not None:
            op.append(jnp.where(keep, ap[i], bp[n - 1 - i]))
    return ov, ok, (op if ap is not None else None)


def _stage2_candidates():
    return [(a, b) for a in range(PEER_TOPK) for b in range(PEER_TOPK) if (a + 1) * (b + 1) <= PEER_TOPK]


def _retr_kernel(h_ref, wq_ref, keys_ref, gate_ref, eidx_ref, q_s, sv_s, si_s, *, tile):
    q_s[...] = jnp.dot(h_ref[...].astype(BF16), wq_ref[...], preferred_element_type=F32)
    sub = lax.broadcasted_iota(jnp.int32, (8, tile), 0)
    sv_s[...] = jnp.zeros(sv_s.shape, F32)
    si_s[...] = jnp.zeros(si_s.shape, jnp.int32)

    def stage1(hp, _):
        qh = q_s[:, pl.ds(pl.multiple_of(hp * PEER_HALF, PEER_HALF), PEER_HALF)].astype(BF16)
        st = lax.dot_general(keys_ref[hp], qh, (((1,), (1,)), ((), ())),
                             preferred_element_type=F32)
        st = st.reshape(N_KEYS // 8, 8, tile)
        vals = [st[i] for i in range(16)]
        keys = [sub + 8 * i for i in range(16)]
        for (i, j) in _SORT16:
            _cmpx(vals, keys, None, i, j)
        for shift in (4, 2, 1):
            bv = [pltpu.roll(t, shift, axis=0) for t in vals]
            bk = [pltpu.roll(t, shift, axis=0) for t in keys]
            vals, keys, _ = _top16_merge(vals, keys, None, bv, bk, None)
            for (i, j) in _BITONIC16:
                _cmpx(vals, keys, None, i, j)
        head = hp // 2
        half = hp % 2
        for i in range(16):
            sv_s[half, i] = jnp.where(sub == head, vals[i], sv_s[half, i])
            si_s[half, i] = jnp.where(sub == head, keys[i], si_s[half, i])
        return 0

    lax.fori_loop(0, 2 * PEER_HEADS, stage1, 0)

    cands = _stage2_candidates()
    n_pad = 64
    vals, poss, exps = [], [], []
    for (a, b) in cands:
        vals.append(sv_s[0, a] + sv_s[1, b])
        poss.append(jnp.full((8, tile), a * PEER_TOPK + b, jnp.int32))
        exps.append(si_s[0, a] * N_KEYS + si_s[1, b])
    for t in range(n_pad - len(cands)):
        vals.append(jnp.full((8, tile), -jnp.inf, F32))
        poss.append(jnp.full((8, tile), PEER_TOPK * PEER_TOPK + t, jnp.int32))
        exps.append(jnp.zeros((8, tile), jnp.int32))
    groups = []
    for g in range(n_pad // 16):
        gv, gk, gp = vals[16 * g:16 * g + 16], poss[16 * g:16 * g + 16], exps[16 * g:16 * g + 16]
        for (i, j) in _SORT16:
            _cmpx(gv, gk, gp, i, j)
        groups.append((gv, gk, gp))
    while len(groups) > 1:
        nxt = []
        for g in range(0, len(groups), 2):
            mv, mk, mp = _top16_merge(*groups[g], *groups[g + 1])
            if len(groups) > 2:
                for (i, j) in _BITONIC16:
                    _cmpx(mv, mk, mp, i, j)
            nxt.append((mv, mk, mp))
        groups = nxt
    best, _, experts = groups[0]

    mx = functools.reduce(jnp.maximum, best)
    ex = [jnp.exp(b - mx) for b in best]
    inv = 1.0 / functools.reduce(lambda a, b: a + b, ex)
    for kk in range(PEER_TOPK):
        gate_ref[kk] = ex[kk] * inv
        eidx_ref[kk] = experts[kk]


def _retrieval(h2, wq_bf, keys_bf, *, tile):
    N, D = h2.shape
    QW = wq_bf.shape[1]
    return pl.pallas_call(
        functools.partial(_retr_kernel, tile=tile),
        grid=(N // tile,),
        in_specs=[pl.BlockSpec((tile, D), lambda i: (i, 0)),
                  pl.BlockSpec((D, QW), lambda i: (0, 0)),
                  pl.BlockSpec(keys_bf.shape, lambda i: (0, 0, 0))],
        out_specs=[pl.BlockSpec((PEER_TOPK, PEER_HEADS, tile), lambda i: (0, 0, i)),
                   pl.BlockSpec((PEER_TOPK, PEER_HEADS, tile), lambda i: (0, 0, i))],
        out_shape=[jax.ShapeDtypeStruct((PEER_TOPK, PEER_HEADS, N), F32),
                   jax.ShapeDtypeStruct((PEER_TOPK, PEER_HEADS, N), jnp.int32)],
        scratch_shapes=[pltpu.VMEM((tile, QW), F32),
                        pltpu.VMEM((2, PEER_TOPK, PEER_HEADS, tile), F32),
                        pltpu.VMEM((2, PEER_TOPK, PEER_HEADS, tile), jnp.int32)],
        compiler_params=_cparams(("arbitrary",)),
        name="retrieval",
    )(h2, wq_bf, keys_bf)


def _expert_kernel(idx0_ref, idxn_ref, tab_ref, h_ref, gate_ref, x1_ref, g2_ref, lng_ref, lnb_ref,
                   o_ref, buf_a, buf_b, gcol_ref, f_ref, sem_ref, *, gtok, alpha):
    i = pl.program_id(0)
    n_steps = pl.num_programs(0)
    per_tok = PEER_HEADS * PEER_TOPK
    d_model = h_ref.shape[-1]

    def start_token(idx_ref, buf, sem_row, j):
        for g in range(per_tok // 8):
            for t in range(8):
                e = idx_ref[0, 0, j * per_tok + g * 8 + t]
                pltpu.make_async_copy(tab_ref.at[e], buf.at[j, g, pl.ds(t, 1), :],
                                      sem_ref.at[sem_row, j]).start(priority=t % 2)

    def wait_token(buf, other, sem_row, j):
        pltpu.make_async_copy(other.at[j], buf.at[j], sem_ref.at[sem_row, j]).wait()

    def token(j, cur, nxt, cur_row):
        wait_token(cur, nxt, cur_row, j)
        start_token(idxn_ref, nxt, 1 - cur_row, j)
        r = cur[j].reshape(per_tok, d_model)
        u = pltpu.bitcast(r & jnp.uint32(0xFFFF0000), F32)
        v = pltpu.bitcast(r << 16, F32)
        d = jnp.sum(u * h_ref[pl.ds(j, 1), :], axis=-1, keepdims=True)
        w = gcol_ref[j] * _gelu_tanh(d)
        f_ref[pl.ds(j, 1), :] = jnp.sum(v * jnp.concatenate([w] * (d_model // 128), axis=1),
                                        axis=0, keepdims=True)
        return 0

    @pl.when(i == 0)
    def _():
        def prime(j, _):
            start_token(idx0_ref, buf_a, 0, j)
            return 0
        lax.fori_loop(0, gtok, prime, 0)

    gates = gate_ref[0]
    for j in range(gtok):
        gcol_ref[j] = jnp.broadcast_to(gates[:, j:j + 1], (per_tok, 128))

    @pl.when(i % 2 == 0)
    def _():
        lax.fori_loop(0, gtok, lambda j, c: token(j, buf_a, buf_b, 0), 0)

    @pl.when(i % 2 == 1)
    def _():
        lax.fori_loop(0, gtok, lambda j, c: token(j, buf_b, buf_a, 1), 0)

    @pl.when((i == n_steps - 1) & (i % 2 == 0))
    def _():
        lax.fori_loop(0, gtok, lambda j, c: (wait_token(buf_b, buf_a, 1, j), 0)[1], 0)

    @pl.when((i == n_steps - 1) & (i % 2 == 1))
    def _():
        lax.fori_loop(0, gtok, lambda j, c: (wait_token(buf_a, buf_b, 0, j), 0)[1], 0)

    o_ref[...] = _ln(alpha * x1_ref[...] + g2_ref[0] * f_ref[...]) * lng_ref[...] + lnb_ref[...]


def _experts(eidx, gate, tab, h2, x1, g2, lng, lnb, *, gtok, seq, alpha):
    N, D = h2.shape
    S = N // gtok
    per_tok = PEER_HEADS * PEER_TOPK
    rows = gtok * per_tok
    steps_per_batch = seq // gtok
    smem = pltpu.MemorySpace.SMEM
    return pl.pallas_call(
        functools.partial(_expert_kernel, gtok=gtok, alpha=alpha),
        grid=(S,),
        in_specs=[pl.BlockSpec((1, 1, rows), lambda i: (0, 0, 0), memory_space=smem),
                  pl.BlockSpec((1, 1, rows), lambda i: (jnp.minimum(i + 1, S - 1), 0, 0), memory_space=smem),
                  pl.BlockSpec(memory_space=pl.ANY),
                  pl.BlockSpec((gtok, D), lambda i: (i, 0)),
                  pl.BlockSpec((1, PEER_HEADS * PEER_TOPK, gtok), lambda i: (i, 0, 0)),
                  pl.BlockSpec((gtok, D), lambda i: (i, 0)),
                  pl.BlockSpec((1, 1, D), lambda i: (i // steps_per_batch, 0, 0)),
                  pl.BlockSpec((1, D), lambda i: (0, 0)),
                  pl.BlockSpec((1, D), lambda i: (0, 0))],
        out_specs=pl.BlockSpec((gtok, D), lambda i: (i, 0)),
        out_shape=jax.ShapeDtypeStruct((N, D), F32),
        scratch_shapes=[pltpu.VMEM((gtok, per_tok // 8, 8, D), jnp.uint32),
                        pltpu.VMEM((gtok, per_tok // 8, 8, D), jnp.uint32),
                        pltpu.VMEM((gtok, per_tok, 128), F32),
                        pltpu.VMEM((gtok, D), F32),
                        pltpu.SemaphoreType.DMA((2, gtok))],
        compiler_params=_cparams(("arbitrary",)),
        name="experts",
    )(eidx, eidx, tab, h2, gate, x1, g2, lng, lnb)


def _rope_tables(L):
    half = HEAD_DIM // 4
    t = np.arange(L)
    freqs = ROPE_THETA ** (-np.arange(half, dtype=np.float32) / half)
    lane = np.arange(HEAD_DIM)
    pos = np.where(lane[None, :] < HEAD_DIM // 2, (t // GRID_W)[:, None], (t % GRID_W)[:, None]).astype(np.float32)
    ang = pos * freqs[lane % half][None, :].astype(np.float32)
    cos, sin = np.cos(ang), np.sin(ang)
    first = ((lane % (2 * half)) < half)[None, :]
    slo = np.where(first, -sin, 0.0)
    shi = np.where(first, 0.0, sin)
    tile = lambda a: jnp.asarray(np.tile(a, (1, N_KV_HEADS)), F32)
    return tile(cos), tile(slo), tile(shi)


def _pack_tables(u, v):
    ub = lax.bitcast_convert_type(u.astype(BF16), jnp.uint16).astype(jnp.uint32)
    vb = lax.bitcast_convert_type(v.astype(BF16), jnp.uint16).astype(jnp.uint32)
    return (ub << 16) | vb


def _pick(n, prefs):
    for p in prefs:
        if n % p == 0:
            return p
    return n


def kernel(x, c, ctx, c_ctx, w_mod, b_mod, w_in, q_norm_g, k_norm_g, gmlp_ln_g, gmlp_ln_b, gmlp_w_s, gmlp_b_s,
           conv_w_dw, conv_b_dw, conv_ln_g, conv_ln_b, conv_w_pw, conv_b_pw, w_out, ln1_g, ln1_b, ln2_g, ln2_b,
           peer_w_q, peer_sub_keys, peer_u, peer_v):
    B, L, D = x.shape
    LC = ctx.shape[1]
    depth = w_mod.shape[0]
    alpha = (2 * depth) ** 0.25
    gw = gmlp_ln_g.shape[-1]

    cos, slo, shi = _rope_tables(L)
    ones_c = jnp.ones((LC, KV_WIDTH), F32)
    zeros_c = jnp.zeros((LC, KV_WIDTH), F32)
    gidx = np.arange(ATTN_WIDTH) // HEAD_DIM
    gmat = jnp.asarray((gidx[:, None] == gidx[None, :]) / HEAD_DIM, BF16)

    cond = jnp.concatenate([c, c_ctx[None, :], jnp.zeros((-(B + 1) % 8, D), F32)], axis=0)

    t_in = _pick(L, (512, 256, 128))
    t_mix = _pick(L, (256, 128))
    t_inc = _pick(LC, (512, 256, 128))
    t_mixc = _pick(LC, (256, 128))
    tk_lat = _pick(L + LC, (768, 512, 384, 256, 128))
    tk_ctx = _pick(LC, (512, 256, 128))
    gtok = 16

    def peer(h2, x1, g2, l, seq, wq_bf, keys_bf, tab):
        n = h2.shape[0] * h2.shape[1]
        gate, eidx = _retrieval(h2.reshape(n, D), wq_bf, keys_bf, tile=_pick(n, (256, 128)))
        rows = PEER_TOPK * PEER_HEADS
        eidx = eidx.reshape(rows, n).T.reshape(n // gtok, 1, gtok * rows)
        gate = gate.reshape(rows, n // gtok, gtok).transpose(1, 0, 2)
        out = _experts(eidx, gate, tab, h2.reshape(n, D), x1.reshape(n, D), g2,
                       ln2_g[l].reshape(1, D), ln2_b[l].reshape(1, D), gtok=gtok, seq=seq, alpha=alpha)
        return out.reshape(h2.shape)

    x_lat, x_ctx = x, ctx
    for l in range(depth):
        last = l == depth - 1
        mod = _modulation(cond, w_mod[l], b_mod[l])
        m_lat = [mod[:B, k * D:(k + 1) * D].reshape(B, 1, D) for k in range(6)]
        m_ctx = [jnp.broadcast_to(mod[B:B + 1, k * D:(k + 1) * D].reshape(1, 1, D), (B, 1, D)) for k in range(6)]

        w_in_bf = w_in[l].astype(BF16)
        gq = jnp.tile(q_norm_g[l], N_Q_HEADS).reshape(1, ATTN_WIDTH)
        gk = jnp.tile(k_norm_g[l], N_KV_HEADS).reshape(1, KV_WIDTH)
        mix_prm = (gmlp_ln_g[l].reshape(1, gw), gmlp_ln_b[l].reshape(1, gw),
                   gmlp_w_s[l].astype(BF16),
                   jnp.repeat(gmlp_b_s[l].T, gw // GMLP_GROUPS, axis=1),
                   conv_w_dw[l], conv_b_dw[l].reshape(1, -1), conv_ln_g[l].reshape(1, -1),
                   conv_ln_b[l].reshape(1, -1), conv_w_pw[l].astype(BF16), conv_b_pw[l].reshape(1, -1),
                   w_out[l].astype(BF16), ln1_g[l].reshape(1, D), ln1_b[l].reshape(1, D))
        wq_bf = peer_w_q[l].astype(BF16)
        keys_bf = peer_sub_keys[l].reshape(2 * PEER_HEADS, N_KEYS, PEER_HALF).astype(BF16)
        tab = _pack_tables(peer_u[l], peer_v[l]).reshape(-1, 1, D)

        q_l, kt_l, v_l, zg_l, zc_l = _inproj(x_lat, m_lat[0], m_lat[1], w_in_bf, gq, gk, cos, slo, shi, gmat,
                                              tile=t_in)
        q_c, kt_c, v_c, zg_c, zc_c = _inproj(x_ctx, m_ctx[0], m_ctx[1], w_in_bf, gq, gk, ones_c, zeros_c, zeros_c,
                                              gmat, tile=t_inc)
        kt_all = jnp.concatenate([kt_c, kt_l], axis=3)
        v_all = jnp.concatenate([v_c, v_l], axis=1)
        o_lat = _attention(q_l, kt_all, v_all, tq=128, tk=tk_lat)
        x1, h2 = _mixers(x_lat, o_lat, zg_l, zc_l, m_lat[2], m_lat[3], m_lat[4], mix_prm, tile=t_mix, alpha=alpha)
        x_lat_new = peer(h2, x1, m_lat[5], l, L, wq_bf, keys_bf, tab)

        if not last:
            o_ctx = _attention(q_c, kt_c, v_c, tq=128, tk=tk_ctx)
            x1c, h2c = _mixers(x_ctx, o_ctx, zg_c, zc_c, m_ctx[2], m_ctx[3], m_ctx[4], mix_prm, tile=t_mixc,
                               alpha=alpha)
            x_ctx = peer(h2c, x1c, m_ctx[5], l, LC, wq_bf, keys_bf, tab)
        x_lat = x_lat_new
    return x_lat
```

```python
import functools
import math

import jax
import jax.numpy as jnp
import numpy as np
from jax import lax
from jax.experimental import pallas as pl
from jax.experimental.pallas import tpu as pltpu

F32 = jnp.float32
BF16 = jnp.bfloat16

EPS = 1e-6
GRID_W = 64
N_Q_HEADS = 8
N_KV_HEADS = 2
Q_PER_KV = N_Q_HEADS // N_KV_HEADS
HEAD_DIM = 64
ATTN_WIDTH = N_Q_HEADS * HEAD_DIM
KV_WIDTH = N_KV_HEADS * HEAD_DIM
QKV_WIDTH = ATTN_WIDTH + 2 * KV_WIDTH
ROPE_THETA = 10000.0
CHUNK = 128
GMLP_GROUPS = 4
CONV_K = 31
CONV_HALO = 16
PEER_HEADS = 8
N_KEYS = 128
PEER_TOPK = 16
PEER_HALF = 128

VMEM_LIMIT = 56 * 1024 * 1024


def _cparams(sem):
    return pltpu.CompilerParams(dimension_semantics=sem, vmem_limit_bytes=VMEM_LIMIT)


def _gelu_tanh(x):
    return 0.5 * x * (1.0 + jnp.tanh(math.sqrt(2.0 / math.pi) * (x + 0.044715 * (x * x * x))))


def _ln(x):
    mu = jnp.mean(x, axis=-1, keepdims=True)
    xc = x - mu
    var = jnp.mean(xc * xc, axis=-1, keepdims=True)
    return xc * lax.rsqrt(var + EPS)


def _mod_kernel(c_ref, w_ref, b_ref, o_ref):
    c = c_ref[...]
    s = c * jax.nn.sigmoid(c)
    o_ref[...] = jnp.dot(s, w_ref[...], preferred_element_type=F32,
                         precision=lax.Precision.HIGHEST) + b_ref[...]


def _modulation(cond, w_mod, b_mod):
    R, D = cond.shape
    N = w_mod.shape[1]
    tn = 1024
    return pl.pallas_call(
        _mod_kernel,
        grid=(N // tn,),
        in_specs=[pl.BlockSpec((R, D), lambda j: (0, 0)),
                  pl.BlockSpec((D, tn), lambda j: (0, j)),
                  pl.BlockSpec((1, tn), lambda j: (0, j))],
        out_specs=pl.BlockSpec((R, tn), lambda j: (0, j)),
        out_shape=jax.ShapeDtypeStruct((R, N), F32),
        compiler_params=_cparams(("arbitrary",)),
        name="modulation",
    )(cond, w_mod, b_mod.reshape(1, N))


def _group_mean_sq(t, gmat):
    sq = t * t
    hi = sq.astype(BF16)
    lo = (sq - hi.astype(F32)).astype(BF16)
    return (jnp.dot(hi, gmat, preferred_element_type=F32)
            + jnp.dot(lo, gmat, preferred_element_type=F32))


def _rope(t, cos, sin_lo, sin_hi):
    w = t.shape[-1]
    up = pltpu.roll(t, w - 16, axis=1)
    dn = pltpu.roll(t, 16, axis=1)
    return t * cos + up * sin_lo + dn * sin_hi


def _inproj_kernel(x_ref, sh_ref, sc_ref, w_ref, gq_ref, gk_ref, cos_ref, slo_ref, shi_ref, gmat_ref,
                   q_ref, kt_ref, v_ref, zg_ref, zc_ref):
    h = _ln(x_ref[0]) * (1.0 + sc_ref[0]) + sh_ref[0]
    p = jnp.dot(h.astype(BF16), w_ref[...], preferred_element_type=F32)
    q = p[:, :ATTN_WIDTH]
    k = p[:, ATTN_WIDTH:ATTN_WIDTH + KV_WIDTH]
    v = p[:, ATTN_WIDTH + KV_WIDTH:QKV_WIDTH]
    gm_w = (p.shape[1] - QKV_WIDTH) // 2
    zg_ref[0] = p[:, QKV_WIDTH:QKV_WIDTH + gm_w]
    zc_ref[0] = p[:, QKV_WIDTH + gm_w:]
    v_ref[0] = v.astype(BF16)

    gmat = gmat_ref[...]
    cos, slo, shi = cos_ref[...], slo_ref[...], shi_ref[...]
    qn = q * lax.rsqrt(_group_mean_sq(q, gmat) + EPS) * gq_ref[...]
    rep = ATTN_WIDTH // KV_WIDTH
    qr = _rope(qn, jnp.concatenate([cos] * rep, axis=1), jnp.concatenate([slo] * rep, axis=1),
               jnp.concatenate([shi] * rep, axis=1))
    qr = (qr * (HEAD_DIM ** -0.5 * math.log2(math.e))).astype(BF16)
    for hh in range(N_Q_HEADS):
        q_ref[0, hh] = qr[:, hh * HEAD_DIM:(hh + 1) * HEAD_DIM]

    kn = k * lax.rsqrt(_group_mean_sq(k, gmat[:KV_WIDTH, :KV_WIDTH]) + EPS) * gk_ref[...]
    kt = _rope(kn, cos, slo, shi).T
    for g in range(N_KV_HEADS):
        kt_ref[0, g] = kt[g * HEAD_DIM:(g + 1) * HEAD_DIM, :].astype(BF16)


def _inproj(x, sh, sc, w_in_bf, gq, gk, cos, slo, shi, gmat, *, tile):
    B, L, D = x.shape
    W = w_in_bf.shape[1]
    gm_w = (W - QKV_WIDTH) // 2
    nt = L // tile
    bmap = lambda b, i: (b, 0, 0)
    cmap = lambda b, i: (0, 0)
    return pl.pallas_call(
        _inproj_kernel,
        grid=(B, nt),
        in_specs=[pl.BlockSpec((1, tile, D), lambda b, i: (b, i, 0)),
                  pl.BlockSpec((1, 1, D), bmap), pl.BlockSpec((1, 1, D), bmap),
                  pl.BlockSpec((D, W), cmap),
                  pl.BlockSpec((1, ATTN_WIDTH), cmap), pl.BlockSpec((1, KV_WIDTH), cmap),
                  pl.BlockSpec((tile, KV_WIDTH), lambda b, i: (i, 0)),
                  pl.BlockSpec((tile, KV_WIDTH), lambda b, i: (i, 0)),
                  pl.BlockSpec((tile, KV_WIDTH), lambda b, i: (i, 0)),
                  pl.BlockSpec((ATTN_WIDTH, ATTN_WIDTH), cmap)],
        out_specs=[pl.BlockSpec((1, N_Q_HEADS, tile, HEAD_DIM), lambda b, i: (b, 0, i, 0)),
                   pl.BlockSpec((1, N_KV_HEADS, HEAD_DIM, tile), lambda b, i: (b, 0, 0, i)),
                   pl.BlockSpec((1, tile, KV_WIDTH), lambda b, i: (b, i, 0)),
                   pl.BlockSpec((1, tile, gm_w), lambda b, i: (b, i, 0)),
                   pl.BlockSpec((1, tile, gm_w), lambda b, i: (b, i, 0))],
        out_shape=[jax.ShapeDtypeStruct((B, N_Q_HEADS, L, HEAD_DIM), BF16),
                   jax.ShapeDtypeStruct((B, N_KV_HEADS, HEAD_DIM, L), BF16),
                   jax.ShapeDtypeStruct((B, L, KV_WIDTH), BF16),
                   jax.ShapeDtypeStruct((B, L, gm_w), F32),
                   jax.ShapeDtypeStruct((B, L, gm_w), F32)],
        compiler_params=_cparams(("arbitrary", "arbitrary")),
        name="inproj",
    )(x, sh, sc, w_in_bf, gq, gk, cos, slo, shi, gmat)


def _attn_kernel(q_ref, kt_ref, vx_ref, o_ref, p_ref, *, tq, tk, nk):
    m_rows = Q_PER_KV * tq
    qs = [q_ref[0, Q_PER_KV * g:Q_PER_KV * (g + 1)].reshape(m_rows, HEAD_DIM) for g in range(N_KV_HEADS)]

    def pv(c, par, g):
        off = pl.multiple_of(c * tk, tk)
        return jnp.dot(p_ref[par, g], vx_ref[0, g, pl.ds(off, tk), :], preferred_element_type=F32)

    def step(c, par, carry):
        off = pl.multiple_of(c * tk, tk)
        out = []
        for g in range(N_KV_HEADS):
            m, acc = carry[g]
            s = jnp.dot(qs[g], kt_ref[0, g, :, pl.ds(off, tk)], preferred_element_type=F32)
            m_new = jnp.maximum(m, jnp.max(s, axis=-1, keepdims=True))
            p_ref[1 - par, g] = jnp.exp2(s - m_new).astype(BF16)
            if acc is not None:
                acc = jnp.exp2(m - m_new) * (acc + pv(c - 1, par, g))
            out.append((m_new, acc))
        return tuple(out)

    first = step(0, 1, tuple((jnp.full((m_rows, 1), -1e30, F32), None) for _ in range(N_KV_HEADS)))
    carry = tuple((m, jnp.zeros((m_rows, 2 * HEAD_DIM), F32)) for m, _ in first)

    n_pairs = (nk - 1) // 2
    def pair(t, carry):
        carry = step(2 * t + 1, 0, carry)
        return step(2 * t + 2, 1, carry)
    carry = lax.fori_loop(0, n_pairs, pair, carry)
    par = 0
    if (nk - 1) % 2:
        carry = step(nk - 1, 0, carry)
        par = 1

    for g in range(N_KV_HEADS):
        acc = carry[g][1] + pv(nk - 1, par, g)
        og = acc / pltpu.roll(acc, HEAD_DIM, axis=1)
        for r in range(Q_PER_KV):
            hh = Q_PER_KV * g + r
            o_ref[0, :, hh * HEAD_DIM:(hh + 1) * HEAD_DIM] = og[r * tq:(r + 1) * tq, :HEAD_DIM].astype(o_ref.dtype)


def _attention(q, kt, v, *, tq, tk):
    B, _, L, _ = q.shape
    LK = kt.shape[-1]
    ones = jnp.ones((B, LK, HEAD_DIM), v.dtype)
    vx = jnp.stack([jnp.concatenate([v[..., g * HEAD_DIM:(g + 1) * HEAD_DIM], ones], axis=-1)
                    for g in range(N_KV_HEADS)], axis=1)
    return pl.pallas_call(
        functools.partial(_attn_kernel, tq=tq, tk=tk, nk=LK // tk),
        grid=(B, L // tq),
        in_specs=[pl.BlockSpec((1, N_Q_HEADS, tq, HEAD_DIM), lambda b, i: (b, 0, i, 0)),
                  pl.BlockSpec((1, N_KV_HEADS, HEAD_DIM, LK), lambda b, i: (b, 0, 0, 0)),
                  pl.BlockSpec((1, N_KV_HEADS, LK, 2 * HEAD_DIM), lambda b, i: (b, 0, 0, 0))],
        out_specs=pl.BlockSpec((1, tq, ATTN_WIDTH), lambda b, i: (b, i, 0)),
        out_shape=jax.ShapeDtypeStruct((B, L, ATTN_WIDTH), BF16),
        scratch_shapes=[pltpu.VMEM((2, N_KV_HEADS, Q_PER_KV * tq, tk), BF16)],
        compiler_params=_cparams(("arbitrary", "arbitrary")),
        name="attention",
    )(q, kt, vx)


def _mix_kernel(x_ref, o_ref, zg_ref, zc_ref, zp_ref, zn_ref, g1_ref, sh2_ref, sc2_ref,
                gln_g_ref, gln_b_ref, ws_ref, bs_ref, wdw_ref, bdw_ref, cln_g_ref, cln_b_ref,
                wpw_ref, bpw_ref, wout_ref, ln1g_ref, ln1b_ref,
                x1_ref, h2_ref, ybuf_ref, *, tile, alpha):
    i = pl.program_id(1)
    nt = pl.num_programs(1)
    gw = zg_ref.shape[-1] // 2
    cw = zc_ref.shape[-1] // 2

    z = _gelu_tanh(zg_ref[0])
    u, v = z[:, :gw], z[:, gw:]
    vn = (_ln(v) * gln_g_ref[...] + gln_b_ref[...]).astype(BF16)
    grp = lax.broadcasted_iota(jnp.int32, (CHUNK, gw), 1) // (gw // GMLP_GROUPS)
    sg = []
    for n in range(tile // CHUNK):
        vc = vn[n * CHUNK:(n + 1) * CHUNK]
        s = jnp.zeros((CHUNK, gw), F32)
        for g in range(GMLP_GROUPS):
            s = jnp.where(grp == g, jnp.dot(ws_ref[g], vc, preferred_element_type=F32), s)
        sg.append(s + bs_ref[...])
    y_g = u * jnp.concatenate(sg, axis=0)

    def glu(zz):
        return zz[:, :cw] * jax.nn.sigmoid(zz[:, cw:])

    ybuf_ref[0:CONV_HALO] = glu(zp_ref[0]) * (i > 0).astype(F32)
    ybuf_ref[CONV_HALO:CONV_HALO + tile] = glu(zc_ref[0])
    ybuf_ref[CONV_HALO + tile:2 * CONV_HALO + tile] = glu(zn_ref[0]) * (i < nt - 1).astype(F32)
    base = CONV_HALO - CONV_K // 2
    conv = jnp.zeros((tile, cw), F32)
    for kk in range(CONV_K):
        conv = conv + ybuf_ref[base + kk:base + kk + tile, :] * wdw_ref[kk:kk + 1, :]
    conv = conv + bdw_ref[...]
    t = _ln(conv) * cln_g_ref[...] + cln_b_ref[...]
    t = t * jax.nn.sigmoid(t)
    y_c = jnp.dot(t.astype(BF16), wpw_ref[...], preferred_element_type=F32) + bpw_ref[...]

    aw = o_ref.shape[-1]
    y = (jnp.dot(o_ref[0], wout_ref[0:aw, :], preferred_element_type=F32)
         + jnp.dot(y_g.astype(BF16), wout_ref[aw:aw + gw, :], preferred_element_type=F32)
         + jnp.dot(y_c.astype(BF16), wout_ref[aw + gw:, :], preferred_element_type=F32))
    x1 = _ln(alpha * x_ref[0] + g1_ref[0] * y) * ln1g_ref[...] + ln1b_ref[...]
    x1_ref[0] = x1
    h2_ref[0] = _ln(x1) * (1.0 + sc2_ref[0]) + sh2_ref[0]


def _mixers(x, o, zg, zc, g1, sh2, sc2, prm, *, tile, alpha):
    B, L, D = x.shape
    nt = L // tile
    hb = tile // CONV_HALO
    nhb = L // CONV_HALO
    gw2, cw2 = zg.shape[-1], zc.shape[-1]
    tmap = lambda b, i: (b, i, 0)
    bmap = lambda b, i: (b, 0, 0)
    c2 = lambda b, i: (0, 0)
    c3 = lambda b, i: (0, 0, 0)
    in_specs = [pl.BlockSpec((1, tile, D), tmap),
                pl.BlockSpec((1, tile, o.shape[-1]), tmap),
                pl.BlockSpec((1, tile, gw2), tmap),
                pl.BlockSpec((1, tile, cw2), tmap),
                pl.BlockSpec((1, CONV_HALO, cw2), lambda b, i: (b, jnp.maximum(i * hb - 1, 0), 0)),
                pl.BlockSpec((1, CONV_HALO, cw2), lambda b, i: (b, jnp.minimum((i + 1) * hb, nhb - 1), 0)),
                pl.BlockSpec((1, 1, D), bmap), pl.BlockSpec((1, 1, D), bmap), pl.BlockSpec((1, 1, D), bmap)]
    for a in prm:
        in_specs.append(pl.BlockSpec(a.shape, c2 if a.ndim == 2 else c3))
    return pl.pallas_call(
        functools.partial(_mix_kernel, tile=tile, alpha=alpha),
        grid=(B, nt),
        in_specs=in_specs,
        out_specs=[pl.BlockSpec((1, tile, D), tmap), pl.BlockSpec((1, tile, D), tmap)],
        out_shape=[jax.ShapeDtypeStruct((B, L, D), F32), jax.ShapeDtypeStruct((B, L, D), F32)],
        scratch_shapes=[pltpu.VMEM((tile + 2 * CONV_HALO, cw2 // 2), F32)],
        compiler_params=_cparams(("arbitrary", "arbitrary")),
        name="mixers",
    )(x, o, zg, zc, zc, zc, g1, sh2, sc2, *prm)


def _oddeven_merge_sort_pairs(n):
    pairs = []
    p = 1
    while p < n:
        k = p
        while k >= 1:
            for j in range(k % p, n - k, 2 * k):
                for i in range(min(k, n - j - k)):
                    if (i + j) // (2 * p) == (i + j + k) // (2 * p):
                        pairs.append((i + j, i + j + k))
            k //= 2
        p *= 2
    return pairs


_SORT16 = _oddeven_merge_sort_pairs(16)
_BITONIC16 = [(i, i + d) for d in (8, 4, 2, 1) for i in range(16) if (i & d) == 0]


def _before(a, ia, b, ib):
    return (a > b) | ((a == b) & (ia < ib))


def _cmpx(vals, keys, pay, i, j):
    keep = _before(vals[i], keys[i], vals[j], keys[j])
    vals[i], vals[j] = jnp.where(keep, vals[i], vals[j]), jnp.where(keep, vals[j], vals[i])
    keys[i], keys[j] = jnp.where(keep, keys[i], keys[j]), jnp.where(keep, keys[j], keys[i])
    if pay is not None:
        pay[i], pay[j] = jnp.where(keep, pay[i], pay[j]), jnp.where(keep, pay[j], pay[i])


def _top16_merge(av, ak, ap, bv, bk, bp):
    n = len(av)
    ov, ok, op = [], [], []
    for i in range(n):
        keep = _before(av[i], ak[i], bv[n - 1 - i], bk[n - 1 - i])
        ov.append(jnp.where(keep, av[i], bv[n - 1 - i]))
        ok.append(jnp.where(keep, ak[i], bk[n - 1 - i]))
        if ap is not None:
            op.append(jnp.where(keep, ap[i], bp[n - 1 - i]))
    return ov, ok, (op if ap is not None else None)


def _stage2_candidates():
    return [(a, b) for a in range(PEER_TOPK) for b in range(PEER_TOPK) if (a + 1) * (b + 1) <= PEER_TOPK]


def _retr_kernel(h_ref, wq_ref, keys_ref, gate_ref, eidx_ref, q_s, sv_s, si_s, *, tile):
    q_s[...] = jnp.dot(h_ref[...].astype(BF16), wq_ref[...], preferred_element_type=F32)
    sub = lax.broadcasted_iota(jnp.int32, (8, tile), 0)
    sv_s[...] = jnp.zeros(sv_s.shape, F32)
    si_s[...] = jnp.zeros(si_s.shape, jnp.int32)

    def stage1(hp, _):
        qh = q_s[:, pl.ds(pl.multiple_of(hp * PEER_HALF, PEER_HALF), PEER_HALF)].astype(BF16)
        st = lax.dot_general(keys_ref[hp], qh, (((1,), (1,)), ((), ())),
                             preferred_element_type=F32)
        st = st.reshape(N_KEYS // 8, 8, tile)
        vals = [st[i] for i in range(16)]
        keys = [sub + 8 * i for i in range(16)]
        for (i, j) in _SORT16:
            _cmpx(vals, keys, None, i, j)
        for shift in (4, 2, 1):
            bv = [pltpu.roll(t, shift, axis=0) for t in vals]
            bk = [pltpu.roll(t, shift, axis=0) for t in keys]
            vals, keys, _ = _top16_merge(vals, keys, None, bv, bk, None)
            for (i, j) in _BITONIC16:
                _cmpx(vals, keys, None, i, j)
        head = hp // 2
        half = hp % 2
        for i in range(16):
            sv_s[half, i] = jnp.where(sub == head, vals[i], sv_s[half, i])
            si_s[half, i] = jnp.where(sub == head, keys[i], si_s[half, i])
        return 0

    lax.fori_loop(0, 2 * PEER_HEADS, stage1, 0)

    cands = _stage2_candidates()
    n_pad = 64
    vals, poss, exps = [], [], []
    for (a, b) in cands:
        vals.append(sv_s[0, a] + sv_s[1, b])
        poss.append(jnp.full((8, tile), a * PEER_TOPK + b, jnp.int32))
        exps.append(si_s[0, a] * N_KEYS + si_s[1, b])
    for t in range(n_pad - len(cands)):
        vals.append(jnp.full((8, tile), -jnp.inf, F32))
        poss.append(jnp.full((8, tile), PEER_TOPK * PEER_TOPK + t, jnp.int32))
        exps.append(jnp.zeros((8, tile), jnp.int32))
    groups = []
    for g in range(n_pad // 16):
        gv, gk, gp = vals[16 * g:16 * g + 16], poss[16 * g:16 * g + 16], exps[16 * g:16 * g + 16]
        for (i, j) in _SORT16:
            _cmpx(gv, gk, gp, i, j)
        groups.append((gv, gk, gp))
    while len(groups) > 1:
        nxt = []
        for g in range(0, len(groups), 2):
            mv, mk, mp = _top16_merge(*groups[g], *groups[g + 1])
            if len(groups) > 2:
                for (i, j) in _BITONIC16:
                    _cmpx(mv, mk, mp, i, j)
            nxt.append((mv, mk, mp))
        groups = nxt
    best, _, experts = groups[0]

    mx = functools.reduce(jnp.maximum, best)
    ex = [jnp.exp(b - mx) for b in best]
    inv = 1.0 / functools.reduce(lambda a, b: a + b, ex)
    gate_ref[...] = jnp.concatenate([e * inv for e in ex], axis=0).T
    eidx_ref[...] = jnp.concatenate(experts, axis=0).T


def _retrieval(h2, wq_bf, keys_bf, *, tile):
    N, D = h2.shape
    QW = wq_bf.shape[1]
    return pl.pallas_call(
        functools.partial(_retr_kernel, tile=tile),
        grid=(N // tile,),
        in_specs=[pl.BlockSpec((tile, D), lambda i: (i, 0)),
                  pl.BlockSpec((D, QW), lambda i: (0, 0)),
                  pl.BlockSpec(keys_bf.shape, lambda i: (0, 0, 0))],
        out_specs=[pl.BlockSpec((tile, PEER_TOPK * PEER_HEADS), lambda i: (i, 0)),
                   pl.BlockSpec((tile, PEER_TOPK * PEER_HEADS), lambda i: (i, 0))],
        out_shape=[jax.ShapeDtypeStruct((N, PEER_TOPK * PEER_HEADS), F32),
                   jax.ShapeDtypeStruct((N, PEER_TOPK * PEER_HEADS), jnp.int32)],
        scratch_shapes=[pltpu.VMEM((tile, QW), F32),
                        pltpu.VMEM((2, PEER_TOPK, PEER_HEADS, tile), F32),
                        pltpu.VMEM((2, PEER_TOPK, PEER_HEADS, tile), jnp.int32)],
        compiler_params=_cparams(("arbitrary",)),
        name="retrieval",
    )(h2, wq_bf, keys_bf)


def _expert_kernel(idx0_ref, idxn_ref, tab_ref, h_ref, gate_ref, x1_ref, g2_ref, lng_ref, lnb_ref,
                   o_ref, buf_a, buf_b, f_ref, sem_ref, *, gtok, alpha):
    i = pl.program_id(0)
    n_steps = pl.num_programs(0)
    per_tok = PEER_HEADS * PEER_TOPK
    d_model = h_ref.shape[-1]

    def start_token(idx_ref, buf, sem_row, j):
        for g in range(per_tok // 8):
            for t in range(8):
                e = idx_ref[0, 0, j * per_tok + g * 8 + t]
                pltpu.make_async_copy(tab_ref.at[e], buf.at[j, g, pl.ds(t, 1), :],
                                      sem_ref.at[sem_row, j]).start(priority=t % 2)

    def wait_token(buf, other, sem_row, j):
        pltpu.make_async_copy(other.at[j], buf.at[j], sem_ref.at[sem_row, j]).wait()

    def token(j, cur, nxt, cur_row):
        wait_token(cur, nxt, cur_row, j)
        start_token(idxn_ref, nxt, 1 - cur_row, j)
        r = cur[j].reshape(per_tok, d_model)
        u = pltpu.bitcast(r & jnp.uint32(0xFFFF0000), F32)
        v = pltpu.bitcast(r << 16, F32)
        d = jnp.sum(u * h_ref[j:j + 1, :], axis=-1, keepdims=True)
        w = jnp.broadcast_to(gate_t[:, j:j + 1] * _gelu_tanh(d), (per_tok, d_model))
        f_ref[j:j + 1, :] = jnp.sum(v * w, axis=0, keepdims=True)

    @pl.when(i == 0)
    def _():
        def prime(j, _):
            start_token(idx0_ref, buf_a, 0, j)
            return 0
        lax.fori_loop(0, gtok, prime, 0)

    gate_t = gate_ref[...].T

    @pl.when(i % 2 == 0)
    def _():
        for j in range(gtok):
            token(j, buf_a, buf_b, 0)

    @pl.when(i % 2 == 1)
    def _():
        for j in range(gtok):
            token(j, buf_b, buf_a, 1)

    @pl.when((i == n_steps - 1) & (i % 2 == 0))
    def _():
        lax.fori_loop(0, gtok, lambda j, c: (wait_token(buf_b, buf_a, 1, j), 0)[1], 0)

    @pl.when((i == n_steps - 1) & (i % 2 == 1))
    def _():
        lax.fori_loop(0, gtok, lambda j, c: (wait_token(buf_a, buf_b, 0, j), 0)[1], 0)

    o_ref[...] = _ln(alpha * x1_ref[...] + g2_ref[0] * f_ref[...]) * lng_ref[...] + lnb_ref[...]


def _experts(eidx, gate, tab, h2, x1, g2, lng, lnb, *, gtok, seq, alpha):
    N, D = h2.shape
    S = N // gtok
    per_tok = PEER_HEADS * PEER_TOPK
    rows = gtok * per_tok
    steps_per_batch = seq // gtok
    smem = pltpu.MemorySpace.SMEM
    return pl.pallas_call(
        functools.partial(_expert_kernel, gtok=gtok, alpha=alpha),
        grid=(S,),
        in_specs=[pl.BlockSpec((1, 1, rows), lambda i: (0, 0, 0), memory_space=smem),
                  pl.BlockSpec((1, 1, rows), lambda i: (jnp.minimum(i + 1, S - 1), 0, 0), memory_space=smem),
                  pl.BlockSpec(memory_space=pl.ANY),
                  pl.BlockSpec((gtok, D), lambda i: (i, 0)),
                  pl.BlockSpec((gtok, PEER_HEADS * PEER_TOPK), lambda i: (i, 0)),
                  pl.BlockSpec((gtok, D), lambda i: (i, 0)),
                  pl.BlockSpec((1, 1, D), lambda i: (i // steps_per_batch, 0, 0)),
                  pl.BlockSpec((1, D), lambda i: (0, 0)),
                  pl.BlockSpec((1, D), lambda i: (0, 0))],
        out_specs=pl.BlockSpec((gtok, D), lambda i: (i, 0)),
        out_shape=jax.ShapeDtypeStruct((N, D), F32),
        scratch_shapes=[pltpu.VMEM((gtok, per_tok // 8, 8, D), jnp.uint32),
                        pltpu.VMEM((gtok, per_tok // 8, 8, D), jnp.uint32),
                        pltpu.VMEM((gtok, D), F32),
                        pltpu.SemaphoreType.DMA((2, gtok))],
        compiler_params=_cparams(("arbitrary",)),
        name="experts",
    )(eidx, eidx, tab, h2, gate, x1, g2, lng, lnb)


def _rope_tables(L):
    half = HEAD_DIM // 4
    t = np.arange(L)
    freqs = ROPE_THETA ** (-np.arange(half, dtype=np.float32) / half)
    lane = np.arange(HEAD_DIM)
    pos = np.where(lane[None, :] < HEAD_DIM // 2, (t // GRID_W)[:, None], (t % GRID_W)[:, None]).astype(np.float32)
    ang = pos * freqs[lane % half][None, :].astype(np.float32)
    cos, sin = np.cos(ang), np.sin(ang)
    first = ((lane % (2 * half)) < half)[None, :]
    slo = np.where(first, -sin, 0.0)
    shi = np.where(first, 0.0, sin)
    tile = lambda a: jnp.asarray(np.tile(a, (1, N_KV_HEADS)), F32)
    return tile(cos), tile(slo), tile(shi)


def _pack_tables(u, v):
    ub = lax.bitcast_convert_type(u.astype(BF16), jnp.uint16).astype(jnp.uint32)
    vb = lax.bitcast_convert_type(v.astype(BF16), jnp.uint16).astype(jnp.uint32)
    return (ub << 16) | vb


def _pick(n, prefs):
    for p in prefs:
        if n % p == 0:
            return p
    return n


def kernel(x, c, ctx, c_ctx, w_mod, b_mod, w_in, q_norm_g, k_norm_g, gmlp_ln_g, gmlp_ln_b, gmlp_w_s, gmlp_b_s,
           conv_w_dw, conv_b_dw, conv_ln_g, conv_ln_b, conv_w_pw, conv_b_pw, w_out, ln1_g, ln1_b, ln2_g, ln2_b,
           peer_w_q, peer_sub_keys, peer_u, peer_v):
    B, L, D = x.shape
    LC = ctx.shape[1]
    depth = w_mod.shape[0]
    alpha = (2 * depth) ** 0.25
    gw = gmlp_ln_g.shape[-1]

    cos, slo, shi = _rope_tables(L)
    ones_c = jnp.ones((LC, KV_WIDTH), F32)
    zeros_c = jnp.zeros((LC, KV_WIDTH), F32)
    gidx = np.arange(ATTN_WIDTH) // HEAD_DIM
    gmat = jnp.asarray((gidx[:, None] == gidx[None, :]) / HEAD_DIM, BF16)

    cond = jnp.concatenate([c, c_ctx[None, :], jnp.zeros((-(B + 1) % 8, D), F32)], axis=0)

    t_in = _pick(L, (512, 256, 128))
    t_mix = _pick(L, (256, 128))
    t_inc = _pick(LC, (512, 256, 128))
    t_mixc = _pick(LC, (256, 128))
    tk_lat = _pick(L + LC, (768, 512, 384, 256, 128))
    tk_ctx = _pick(LC, (512, 256, 128))
    gtok = 16

    def peer(h2, x1, g2, l, seq, wq_bf, keys_bf, tab):
        n = h2.shape[0] * h2.shape[1]
        gate, eidx = _retrieval(h2.reshape(n, D), wq_bf, keys_bf, tile=_pick(n, (512, 256, 128)))
        out = _experts(eidx.reshape(n // gtok, 1, -1), gate, tab, h2.reshape(n, D), x1.reshape(n, D), g2,
                       ln2_g[l].reshape(1, D), ln2_b[l].reshape(1, D), gtok=gtok, seq=seq, alpha=alpha)
        return out.reshape(h2.shape)

    x_lat, x_ctx = x, ctx
    for l in range(depth):
        last = l == depth - 1
        mod = _modulation(cond, w_mod[l], b_mod[l])
        m_lat = [mod[:B, k * D:(k + 1) * D].reshape(B, 1, D) for k in range(6)]
        m_ctx = [jnp.broadcast_to(mod[B:B + 1, k * D:(k + 1) * D].reshape(1, 1, D), (B, 1, D)) for k in range(6)]

        w_in_bf = w_in[l].astype(BF16)
        gq = jnp.tile(q_norm_g[l], N_Q_HEADS).reshape(1, ATTN_WIDTH)
        gk = jnp.tile(k_norm_g[l], N_KV_HEADS).reshape(1, KV_WIDTH)
        mix_prm = (gmlp_ln_g[l].reshape(1, gw), gmlp_ln_b[l].reshape(1, gw),
                   gmlp_w_s[l].astype(BF16),
                   jnp.repeat(gmlp_b_s[l].T, gw // GMLP_GROUPS, axis=1),
                   conv_w_dw[l], conv_b_dw[l].reshape(1, -1), conv_ln_g[l].reshape(1, -1),
                   conv_ln_b[l].reshape(1, -1), conv_w_pw[l].astype(BF16), conv_b_pw[l].reshape(1, -1),
                   w_out[l].astype(BF16), ln1_g[l].reshape(1, D), ln1_b[l].reshape(1, D))
        wq_bf = peer_w_q[l].astype(BF16)
        keys_bf = peer_sub_keys[l].reshape(2 * PEER_HEADS, N_KEYS, PEER_HALF).astype(BF16)
        tab = _pack_tables(peer_u[l], peer_v[l]).reshape(-1, 1, D)

        q_l, kt_l, v_l, zg_l, zc_l = _inproj(x_lat, m_lat[0], m_lat[1], w_in_bf, gq, gk, cos, slo, shi, gmat,
                                              tile=t_in)
        q_c, kt_c, v_c, zg_c, zc_c = _inproj(x_ctx, m_ctx[0], m_ctx[1], w_in_bf, gq, gk, ones_c, zeros_c, zeros_c,
                                              gmat, tile=t_inc)
        kt_all = jnp.concatenate([kt_c, kt_l], axis=3)
        v_all = jnp.concatenate([v_c, v_l], axis=1)
        o_lat = _attention(q_l, kt_all, v_all, tq=128, tk=tk_lat)
        x1, h2 = _mixers(x_lat, o_lat, zg_l, zc_l, m_lat[2], m_lat[3], m_lat[4], mix_prm, tile=t_mix, alpha=alpha)
        x_lat_new = peer(h2, x1, m_lat[5], l, L, wq_bf, keys_bf, tab)

        if not last:
            o_ctx = _attention(q_c, kt_c, v_c, tq=128, tk=tk_ctx)
            x1c, h2c = _mixers(x_ctx, o_ctx, zg_c, zc_c, m_ctx[2], m_ctx[3], m_ctx[4], mix_prm, tile=t_mixc,
                               alpha=alpha)
            x_ctx = peer(h2c, x1c, m_ctx[5], l, LC, wq_bf, keys_bf, tab)
        x_lat = x_lat_new
    return x_lat
```

```python
import functools
import math

import jax
import jax.numpy as jnp
import numpy as np
from jax import lax
from jax.experimental import pallas as pl
from jax.experimental.pallas import tpu as pltpu

F32 = jnp.float32
BF16 = jnp.bfloat16

EPS = 1e-6
GRID_W = 64
N_Q_HEADS = 8
N_KV_HEADS = 2
Q_PER_KV = N_Q_HEADS // N_KV_HEADS
HEAD_DIM = 64
ATTN_WIDTH = N_Q_HEADS * HEAD_DIM
KV_WIDTH = N_KV_HEADS * HEAD_DIM
QKV_WIDTH = ATTN_WIDTH + 2 * KV_WIDTH
ROPE_THETA = 10000.0
CHUNK = 128
GMLP_GROUPS = 4
CONV_K = 31
CONV_HALO = 16
PEER_HEADS = 8
N_KEYS = 128
PEER_TOPK = 16
PEER_HALF = 128

VMEM_LIMIT = 56 * 1024 * 1024


def _cparams(sem):
    return pltpu.CompilerParams(dimension_semantics=sem, vmem_limit_bytes=VMEM_LIMIT)


def _gelu_tanh(x):
    return 0.5 * x * (1.0 + jnp.tanh(math.sqrt(2.0 / math.pi) * (x + 0.044715 * (x * x * x))))


def _ln(x):
    mu = jnp.mean(x, axis=-1, keepdims=True)
    xc = x - mu
    var = jnp.mean(xc * xc, axis=-1, keepdims=True)
    return xc * lax.rsqrt(var + EPS)


def _mod_kernel(c_ref, w_ref, b_ref, o_ref):
    c = c_ref[...]
    s = c * jax.nn.sigmoid(c)
    o_ref[...] = jnp.dot(s, w_ref[...], preferred_element_type=F32,
                         precision=lax.Precision.HIGHEST) + b_ref[...]


def _modulation(cond, w_mod, b_mod):
    R, D = cond.shape
    N = w_mod.shape[1]
    tn = 1024
    return pl.pallas_call(
        _mod_kernel,
        grid=(N // tn,),
        in_specs=[pl.BlockSpec((R, D), lambda j: (0, 0)),
                  pl.BlockSpec((D, tn), lambda j: (0, j)),
                  pl.BlockSpec((1, tn), lambda j: (0, j))],
        out_specs=pl.BlockSpec((R, tn), lambda j: (0, j)),
        out_shape=jax.ShapeDtypeStruct((R, N), F32),
        compiler_params=_cparams(("arbitrary",)),
        name="modulation",
    )(cond, w_mod, b_mod.reshape(1, N))


def _group_mean_sq(t, gmat):
    sq = t * t
    hi = sq.astype(BF16)
    lo = (sq - hi.astype(F32)).astype(BF16)
    return (jnp.dot(hi, gmat, preferred_element_type=F32)
            + jnp.dot(lo, gmat, preferred_element_type=F32))


def _rope(t, cos, sin_lo, sin_hi):
    w = t.shape[-1]
    up = pltpu.roll(t, w - 16, axis=1)
    dn = pltpu.roll(t, 16, axis=1)
    return t * cos + up * sin_lo + dn * sin_hi


def _inproj_kernel(x_ref, sh_ref, sc_ref, w_ref, gq_ref, gk_ref, cos_ref, slo_ref, shi_ref, gmat_ref,
                   q_ref, kt_ref, v_ref, zg_ref, zc_ref):
    h = _ln(x_ref[0]) * (1.0 + sc_ref[0]) + sh_ref[0]
    p = jnp.dot(h.astype(BF16), w_ref[...], preferred_element_type=F32)
    q = p[:, :ATTN_WIDTH]
    k = p[:, ATTN_WIDTH:ATTN_WIDTH + KV_WIDTH]
    v = p[:, ATTN_WIDTH + KV_WIDTH:QKV_WIDTH]
    gm_w = (p.shape[1] - QKV_WIDTH) // 2
    zg_ref[0] = p[:, QKV_WIDTH:QKV_WIDTH + gm_w]
    zc_ref[0] = p[:, QKV_WIDTH + gm_w:]
    v_ref[0] = v.astype(BF16)

    gmat = gmat_ref[...]
    cos, slo, shi = cos_ref[...], slo_ref[...], shi_ref[...]
    qn = q * lax.rsqrt(_group_mean_sq(q, gmat) + EPS) * gq_ref[...]
    rep = ATTN_WIDTH // KV_WIDTH
    qr = _rope(qn, jnp.concatenate([cos] * rep, axis=1), jnp.concatenate([slo] * rep, axis=1),
               jnp.concatenate([shi] * rep, axis=1))
    qr = (qr * (HEAD_DIM ** -0.5 * math.log2(math.e))).astype(BF16)
    for hh in range(N_Q_HEADS):
        q_ref[0, hh] = qr[:, hh * HEAD_DIM:(hh + 1) * HEAD_DIM]

    kn = k * lax.rsqrt(_group_mean_sq(k, gmat[:KV_WIDTH, :KV_WIDTH]) + EPS) * gk_ref[...]
    kt = _rope(kn, cos, slo, shi).T
    for g in range(N_KV_HEADS):
        kt_ref[0, g] = kt[g * HEAD_DIM:(g + 1) * HEAD_DIM, :].astype(BF16)


def _inproj(x, sh, sc, w_in_bf, gq, gk, cos, slo, shi, gmat, *, tile):
    B, L, D = x.shape
    W = w_in_bf.shape[1]
    gm_w = (W - QKV_WIDTH) // 2
    nt = L // tile
    bmap = lambda b, i: (b, 0, 0)
    cmap = lambda b, i: (0, 0)
    return pl.pallas_call(
        _inproj_kernel,
        grid=(B, nt),
        in_specs=[pl.BlockSpec((1, tile, D), lambda b, i: (b, i, 0)),
                  pl.BlockSpec((1, 1, D), bmap), pl.BlockSpec((1, 1, D), bmap),
                  pl.BlockSpec((D, W), cmap),
                  pl.BlockSpec((1, ATTN_WIDTH), cmap), pl.BlockSpec((1, KV_WIDTH), cmap),
                  pl.BlockSpec((tile, KV_WIDTH), lambda b, i: (i, 0)),
                  pl.BlockSpec((tile, KV_WIDTH), lambda b, i: (i, 0)),
                  pl.BlockSpec((tile, KV_WIDTH), lambda b, i: (i, 0)),
                  pl.BlockSpec((ATTN_WIDTH, ATTN_WIDTH), cmap)],
        out_specs=[pl.BlockSpec((1, N_Q_HEADS, tile, HEAD_DIM), lambda b, i: (b, 0, i, 0)),
                   pl.BlockSpec((1, N_KV_HEADS, HEAD_DIM, tile), lambda b, i: (b, 0, 0, i)),
                   pl.BlockSpec((1, tile, KV_WIDTH), lambda b, i: (b, i, 0)),
                   pl.BlockSpec((1, tile, gm_w), lambda b, i: (b, i, 0)),
                   pl.BlockSpec((1, tile, gm_w), lambda b, i: (b, i, 0))],
        out_shape=[jax.ShapeDtypeStruct((B, N_Q_HEADS, L, HEAD_DIM), BF16),
                   jax.ShapeDtypeStruct((B, N_KV_HEADS, HEAD_DIM, L), BF16),
                   jax.ShapeDtypeStruct((B, L, KV_WIDTH), BF16),
                   jax.ShapeDtypeStruct((B, L, gm_w), F32),
                   jax.ShapeDtypeStruct((B, L, gm_w), F32)],
        compiler_params=_cparams(("arbitrary", "arbitrary")),
        name="inproj",
    )(x, sh, sc, w_in_bf, gq, gk, cos, slo, shi, gmat)


def _attn_kernel(q_ref, kt_ref, vx_ref, o_ref, p_ref, *, tq, tk, nk):
    m_rows = Q_PER_KV * tq
    qs = [q_ref[0, Q_PER_KV * g:Q_PER_KV * (g + 1)].reshape(m_rows, HEAD_DIM) for g in range(N_KV_HEADS)]

    def pv(c, par, g):
        off = pl.multiple_of(c * tk, tk)
        return jnp.dot(p_ref[par, g], vx_ref[0, g, pl.ds(off, tk), :], preferred_element_type=F32)

    def step(c, par, carry):
        off = pl.multiple_of(c * tk, tk)
        out = []
        for g in range(N_KV_HEADS):
            m, acc = carry[g]
            s = jnp.dot(qs[g], kt_ref[0, g, :, pl.ds(off, tk)], preferred_element_type=F32)
            m_new = jnp.maximum(m, jnp.max(s, axis=-1, keepdims=True))
            p_ref[1 - par, g] = jnp.exp2(s - m_new).astype(BF16)
            if acc is not None:
                acc = jnp.exp2(m - m_new) * (acc + pv(c - 1, par, g))
            out.append((m_new, acc))
        return tuple(out)

    first = step(0, 1, tuple((jnp.full((m_rows, 1), -1e30, F32), None) for _ in range(N_KV_HEADS)))
    carry = tuple((m, jnp.zeros((m_rows, 2 * HEAD_DIM), F32)) for m, _ in first)

    n_pairs = (nk - 1) // 2
    def pair(t, carry):
        carry = step(2 * t + 1, 0, carry)
        return step(2 * t + 2, 1, carry)
    carry = lax.fori_loop(0, n_pairs, pair, carry)
    par = 0
    if (nk - 1) % 2:
        carry = step(nk - 1, 0, carry)
        par = 1

    for g in range(N_KV_HEADS):
        acc = carry[g][1] + pv(nk - 1, par, g)
        og = acc / pltpu.roll(acc, HEAD_DIM, axis=1)
        for r in range(Q_PER_KV):
            hh = Q_PER_KV * g + r
            o_ref[0, :, hh * HEAD_DIM:(hh + 1) * HEAD_DIM] = og[r * tq:(r + 1) * tq, :HEAD_DIM].astype(o_ref.dtype)


def _attention(q, kt, v, *, tq, tk):
    B, _, L, _ = q.shape
    LK = kt.shape[-1]
    ones = jnp.ones((B, LK, HEAD_DIM), v.dtype)
    vx = jnp.stack([jnp.concatenate([v[..., g * HEAD_DIM:(g + 1) * HEAD_DIM], ones], axis=-1)
                    for g in range(N_KV_HEADS)], axis=1)
    return pl.pallas_call(
        functools.partial(_attn_kernel, tq=tq, tk=tk, nk=LK // tk),
        grid=(B, L // tq),
        in_specs=[pl.BlockSpec((1, N_Q_HEADS, tq, HEAD_DIM), lambda b, i: (b, 0, i, 0)),
                  pl.BlockSpec((1, N_KV_HEADS, HEAD_DIM, LK), lambda b, i: (b, 0, 0, 0)),
                  pl.BlockSpec((1, N_KV_HEADS, LK, 2 * HEAD_DIM), lambda b, i: (b, 0, 0, 0))],
        out_specs=pl.BlockSpec((1, tq, ATTN_WIDTH), lambda b, i: (b, i, 0)),
        out_shape=jax.ShapeDtypeStruct((B, L, ATTN_WIDTH), BF16),
        scratch_shapes=[pltpu.VMEM((2, N_KV_HEADS, Q_PER_KV * tq, tk), BF16)],
        compiler_params=_cparams(("arbitrary", "arbitrary")),
        name="attention",
    )(q, kt, vx)


def _mix_kernel(x_ref, o_ref, zg_ref, zc_ref, zp_ref, zn_ref, g1_ref, sh2_ref, sc2_ref,
                gln_g_ref, gln_b_ref, ws_ref, bs_ref, wdw_ref, bdw_ref, cln_g_ref, cln_b_ref,
                wpw_ref, bpw_ref, wout_ref, ln1g_ref, ln1b_ref,
                x1_ref, h2_ref, ybuf_ref, *, tile, alpha):
    i = pl.program_id(1)
    nt = pl.num_programs(1)
    gw = zg_ref.shape[-1] // 2
    cw = zc_ref.shape[-1] // 2

    z = _gelu_tanh(zg_ref[0])
    u, v = z[:, :gw], z[:, gw:]
    vn = (_ln(v) * gln_g_ref[...] + gln_b_ref[...]).astype(BF16)
    grp = lax.broadcasted_iota(jnp.int32, (CHUNK, gw), 1) // (gw // GMLP_GROUPS)
    sg = []
    for n in range(tile // CHUNK):
        vc = vn[n * CHUNK:(n + 1) * CHUNK]
        s = jnp.zeros((CHUNK, gw), F32)
        for g in range(GMLP_GROUPS):
            s = jnp.where(grp == g, jnp.dot(ws_ref[g], vc, preferred_element_type=F32), s)
        sg.append(s + bs_ref[...])
    y_g = u * jnp.concatenate(sg, axis=0)

    def glu(zz):
        return zz[:, :cw] * jax.nn.sigmoid(zz[:, cw:])

    ybuf_ref[0:CONV_HALO] = glu(zp_ref[0]) * (i > 0).astype(F32)
    ybuf_ref[CONV_HALO:CONV_HALO + tile] = glu(zc_ref[0])
    ybuf_ref[CONV_HALO + tile:2 * CONV_HALO + tile] = glu(zn_ref[0]) * (i < nt - 1).astype(F32)
    base = CONV_HALO - CONV_K // 2
    conv = jnp.zeros((tile, cw), F32)
    for kk in range(CONV_K):
        conv = conv + ybuf_ref[base + kk:base + kk + tile, :] * wdw_ref[kk:kk + 1, :]
    conv = conv + bdw_ref[...]
    t = _ln(conv) * cln_g_ref[...] + cln_b_ref[...]
    t = t * jax.nn.sigmoid(t)
    y_c = jnp.dot(t.astype(BF16), wpw_ref[...], preferred_element_type=F32) + bpw_ref[...]

    aw = o_ref.shape[-1]
    y = (jnp.dot(o_ref[0], wout_ref[0:aw, :], preferred_element_type=F32)
         + jnp.dot(y_g.astype(BF16), wout_ref[aw:aw + gw, :], preferred_element_type=F32)
         + jnp.dot(y_c.astype(BF16), wout_ref[aw + gw:, :], preferred_element_type=F32))
    x1 = _ln(alpha * x_ref[0] + g1_ref[0] * y) * ln1g_ref[...] + ln1b_ref[...]
    x1_ref[0] = x1
    h2_ref[0] = _ln(x1) * (1.0 + sc2_ref[0]) + sh2_ref[0]


def _mixers(x, o, zg, zc, g1, sh2, sc2, prm, *, tile, alpha):
    B, L, D = x.shape
    nt = L // tile
    hb = tile // CONV_HALO
    nhb = L // CONV_HALO
    gw2, cw2 = zg.shape[-1], zc.shape[-1]
    tmap = lambda b, i: (b, i, 0)
    bmap = lambda b, i: (b, 0, 0)
    c2 = lambda b, i: (0, 0)
    c3 = lambda b, i: (0, 0, 0)
    in_specs = [pl.BlockSpec((1, tile, D), tmap),
                pl.BlockSpec((1, tile, o.shape[-1]), tmap),
                pl.BlockSpec((1, tile, gw2), tmap),
                pl.BlockSpec((1, tile, cw2), tmap),
                pl.BlockSpec((1, CONV_HALO, cw2), lambda b, i: (b, jnp.maximum(i * hb - 1, 0), 0)),
                pl.BlockSpec((1, CONV_HALO, cw2), lambda b, i: (b, jnp.minimum((i + 1) * hb, nhb - 1), 0)),
                pl.BlockSpec((1, 1, D), bmap), pl.BlockSpec((1, 1, D), bmap), pl.BlockSpec((1, 1, D), bmap)]
    for a in prm:
        in_specs.append(pl.BlockSpec(a.shape, c2 if a.ndim == 2 else c3))
    return pl.pallas_call(
        functools.partial(_mix_kernel, tile=tile, alpha=alpha),
        grid=(B, nt),
        in_specs=in_specs,
        out_specs=[pl.BlockSpec((1, tile, D), tmap), pl.BlockSpec((1, tile, D), tmap)],
        out_shape=[jax.ShapeDtypeStruct((B, L, D), F32), jax.ShapeDtypeStruct((B, L, D), F32)],
        scratch_shapes=[pltpu.VMEM((tile + 2 * CONV_HALO, cw2 // 2), F32)],
        compiler_params=_cparams(("arbitrary", "arbitrary")),
        name="mixers",
    )(x, o, zg, zc, zc, zc, g1, sh2, sc2, *prm)


def _oddeven_merge_sort_pairs(n):
    pairs = []
    p = 1
    while p < n:
        k = p
        while k >= 1:
            for j in range(k % p, n - k, 2 * k):
                for i in range(min(k, n - j - k)):
                    if (i + j) // (2 * p) == (i + j + k) // (2 * p):
                        pairs.append((i + j, i + j + k))
            k //= 2
        p *= 2
    return pairs


_SORT16 = _oddeven_merge_sort_pairs(16)
_BITONIC16 = [(i, i + d) for d in (8, 4, 2, 1) for i in range(16) if (i & d) == 0]


def _before(a, ia, b, ib):
    return (a > b) | ((a == b) & (ia < ib))


def _cmpx(vals, keys, pay, i, j):
    keep = _before(vals[i], keys[i], vals[j], keys[j])
    vals[i], vals[j] = jnp.where(keep, vals[i], vals[j]), jnp.where(keep, vals[j], vals[i])
    keys[i], keys[j] = jnp.where(keep, keys[i], keys[j]), jnp.where(keep, keys[j], keys[i])
    if pay is not None:
        pay[i], pay[j] = jnp.where(keep, pay[i], pay[j]), jnp.where(keep, pay[j], pay[i])


def _top16_merge(av, ak, ap, bv, bk, bp):
    n = len(av)
    ov, ok, op = [], [], []
    for i in range(n):
        keep = _before(av[i], ak[i], bv[n - 1 - i], bk[n - 1 - i])
        ov.append(jnp.where(keep, av[i], bv[n - 1 - i]))
        ok.append(jnp.where(keep, ak[i], bk[n - 1 - i]))
        if ap is not None:
            op.append(jnp.where(keep, ap[i], bp[n - 1 - i]))
    return ov, ok, (op if ap is not None else None)


def _stage2_candidates():
    return [(a, b) for a in range(PEER_TOPK) for b in range(PEER_TOPK) if (a + 1) * (b + 1) <= PEER_TOPK]


def _retr_kernel(h_ref, wq_ref, keys_ref, gate_ref, eidx_ref, q_s, sv_s, si_s, *, tile):
    q_s[...] = jnp.dot(h_ref[...].astype(BF16), wq_ref[...], preferred_element_type=F32)
    sub = lax.broadcasted_iota(jnp.int32, (8, tile), 0)
    sv_s[...] = jnp.zeros(sv_s.shape, F32)
    si_s[...] = jnp.zeros(si_s.shape, jnp.int32)

    def stage1(hp, _):
        qh = q_s[:, pl.ds(pl.multiple_of(hp * PEER_HALF, PEER_HALF), PEER_HALF)].astype(BF16)
        st = lax.dot_general(keys_ref[hp], qh, (((1,), (1,)), ((), ())),
                             preferred_element_type=F32)
        st = st.reshape(N_KEYS // 8, 8, tile)
        vals = [st[i] for i in range(16)]
        keys = [sub + 8 * i for i in range(16)]
        for (i, j) in _SORT16:
            _cmpx(vals, keys, None, i, j)
        for shift in (4, 2, 1):
            bv = [pltpu.roll(t, shift, axis=0) for t in vals]
            bk = [pltpu.roll(t, shift, axis=0) for t in keys]
            vals, keys, _ = _top16_merge(vals, keys, None, bv, bk, None)
            for (i, j) in _BITONIC16:
                _cmpx(vals, keys, None, i, j)
        head = hp // 2
        half = hp % 2
        for i in range(16):
            sv_s[half, i] = jnp.where(sub == head, vals[i], sv_s[half, i])
            si_s[half, i] = jnp.where(sub == head, keys[i], si_s[half, i])
        return 0

    lax.fori_loop(0, 2 * PEER_HEADS, stage1, 0)

    cands = _stage2_candidates()
    n_pad = 64
    vals, poss, exps = [], [], []
    for (a, b) in cands:
        vals.append(sv_s[0, a] + sv_s[1, b])
        poss.append(jnp.full((8, tile), a * PEER_TOPK + b, jnp.int32))
        exps.append(si_s[0, a] * N_KEYS + si_s[1, b])
    for t in range(n_pad - len(cands)):
        vals.append(jnp.full((8, tile), -jnp.inf, F32))
        poss.append(jnp.full((8, tile), PEER_TOPK * PEER_TOPK + t, jnp.int32))
        exps.append(jnp.zeros((8, tile), jnp.int32))
    groups = []
    for g in range(n_pad // 16):
        gv, gk, gp = vals[16 * g:16 * g + 16], poss[16 * g:16 * g + 16], exps[16 * g:16 * g + 16]
        for (i, j) in _SORT16:
            _cmpx(gv, gk, gp, i, j)
        groups.append((gv, gk, gp))
    while len(groups) > 1:
        nxt = []
        for g in range(0, len(groups), 2):
            mv, mk, mp = _top16_merge(*groups[g], *groups[g + 1])
            if len(groups) > 2:
                for (i, j) in _BITONIC16:
                    _cmpx(mv, mk, mp, i, j)
            nxt.append((mv, mk, mp))
        groups = nxt
    best, _, experts = groups[0]

    mx = functools.reduce(jnp.maximum, best)
    ex = [jnp.exp(b - mx) for b in best]
    inv = 1.0 / functools.reduce(lambda a, b: a + b, ex)
    gate_ref[...] = jnp.concatenate([e * inv for e in ex], axis=0).T
    eidx_ref[...] = jnp.concatenate(experts, axis=0).T


def _retrieval(h2, wq_bf, keys_bf, *, tile):
    N, D = h2.shape
    QW = wq_bf.shape[1]
    return pl.pallas_call(
        functools.partial(_retr_kernel, tile=tile),
        grid=(N // tile,),
        in_specs=[pl.BlockSpec((tile, D), lambda i: (i, 0)),
                  pl.BlockSpec((D, QW), lambda i: (0, 0)),
                  pl.BlockSpec(keys_bf.shape, lambda i: (0, 0, 0))],
        out_specs=[pl.BlockSpec((tile, PEER_TOPK * PEER_HEADS), lambda i: (i, 0)),
                   pl.BlockSpec((tile, PEER_TOPK * PEER_HEADS), lambda i: (i, 0))],
        out_shape=[jax.ShapeDtypeStruct((N, PEER_TOPK * PEER_HEADS), F32),
                   jax.ShapeDtypeStruct((N, PEER_TOPK * PEER_HEADS), jnp.int32)],
        scratch_shapes=[pltpu.VMEM((tile, QW), F32),
                        pltpu.VMEM((2, PEER_TOPK, PEER_HEADS, tile), F32),
                        pltpu.VMEM((2, PEER_TOPK, PEER_HEADS, tile), jnp.int32)],
        compiler_params=_cparams(("arbitrary",)),
        name="retrieval",
    )(h2, wq_bf, keys_bf)


def _expert_token(r, h_row, gate_col):
    u = pltpu.bitcast(r & jnp.uint32(0xFFFF0000), F32)
    v = pltpu.bitcast(r << 16, F32)
    d = jnp.sum(u * h_row, axis=-1, keepdims=True)
    w = jnp.broadcast_to(gate_col * _gelu_tanh(d), r.shape)
    return jnp.sum(v * w, axis=0, keepdims=True)


def _expert_kernel(idx0_ref, idxn_ref, tab_ref, h_ref, gate_ref, x1_ref, g2_ref, lng_ref, lnb_ref,
                   o_ref, buf_a, buf_b, f_ref, sem_ref, *, gtok, alpha):
    i = pl.program_id(0)
    n_steps = pl.num_programs(0)
    per_tok = PEER_HEADS * PEER_TOPK
    d_model = h_ref.shape[-1]

    def start_token(idx_ref, buf, sem_row, j):
        for g in range(per_tok // 8):
            for t in range(8):
                e = idx_ref[0, 0, j * per_tok + g * 8 + t]
                pltpu.make_async_copy(tab_ref.at[e], buf.at[j, g, pl.ds(t, 1), :],
                                      sem_ref.at[sem_row, j]).start(priority=t % 2)

    def wait_token(buf, other, sem_row, j):
        pltpu.make_async_copy(other.at[j], buf.at[j], sem_ref.at[sem_row, j]).wait()

    def token(j, cur, nxt, cur_row):
        wait_token(cur, nxt, cur_row, j)
        start_token(idxn_ref, nxt, 1 - cur_row, j)
        f_ref[j:j + 1, :] = _expert_token(cur[j].reshape(per_tok, d_model), h_ref[j:j + 1, :],
                                          gate_t[:, j:j + 1])

    @pl.when(i == 0)
    def _():
        def prime(j, _):
            start_token(idx0_ref, buf_a, 0, j)
            return 0
        lax.fori_loop(0, gtok, prime, 0)

    gate_t = gate_ref[...].T

    @pl.when(i % 2 == 0)
    def _():
        for j in range(gtok):
            token(j, buf_a, buf_b, 0)

    @pl.when(i % 2 == 1)
    def _():
        for j in range(gtok):
            token(j, buf_b, buf_a, 1)

    @pl.when((i == n_steps - 1) & (i % 2 == 0))
    def _():
        lax.fori_loop(0, gtok, lambda j, c: (wait_token(buf_b, buf_a, 1, j), 0)[1], 0)

    @pl.when((i == n_steps - 1) & (i % 2 == 1))
    def _():
        lax.fori_loop(0, gtok, lambda j, c: (wait_token(buf_a, buf_b, 0, j), 0)[1], 0)

    o_ref[...] = _ln(alpha * x1_ref[...] + g2_ref[0] * f_ref[...]) * lng_ref[...] + lnb_ref[...]


def _experts(eidx, gate, tab, h2, x1, g2, lng, lnb, *, n_tok, gtok, seq, alpha):
    D = h2.shape[1]
    N = n_tok
    S = N // gtok
    per_tok = PEER_HEADS * PEER_TOPK
    rows = gtok * per_tok
    steps_per_batch = seq // gtok
    smem = pltpu.MemorySpace.SMEM
    return pl.pallas_call(
        functools.partial(_expert_kernel, gtok=gtok, alpha=alpha),
        grid=(S,),
        in_specs=[pl.BlockSpec((1, 1, rows), lambda i: (0, 0, 0), memory_space=smem),
                  pl.BlockSpec((1, 1, rows), lambda i: (jnp.minimum(i + 1, S - 1), 0, 0), memory_space=smem),
                  pl.BlockSpec(memory_space=pl.ANY),
                  pl.BlockSpec((gtok, D), lambda i: (i, 0)),
                  pl.BlockSpec((gtok, PEER_HEADS * PEER_TOPK), lambda i: (i, 0)),
                  pl.BlockSpec((gtok, D), lambda i: (i, 0)),
                  pl.BlockSpec((1, 1, D), lambda i: (i // steps_per_batch, 0, 0)),
                  pl.BlockSpec((1, D), lambda i: (0, 0)),
                  pl.BlockSpec((1, D), lambda i: (0, 0))],
        out_specs=pl.BlockSpec((gtok, D), lambda i: (i, 0)),
        out_shape=jax.ShapeDtypeStruct((N, D), F32),
        scratch_shapes=[pltpu.VMEM((gtok, per_tok // 8, 8, D), jnp.uint32),
                        pltpu.VMEM((gtok, per_tok // 8, 8, D), jnp.uint32),
                        pltpu.VMEM((gtok, D), F32),
                        pltpu.SemaphoreType.DMA((2, gtok))],
        compiler_params=_cparams(("arbitrary",)),
        name="experts",
    )(eidx, eidx, tab, h2, gate, x1, g2, lng, lnb)


def _expert_dense_kernel(rows_ref, h_ref, gate_ref, x1_ref, g2_ref, lng_ref, lnb_ref, o_ref, f_ref, *, gtok, alpha):
    per_tok = PEER_HEADS * PEER_TOPK
    gate_t = gate_ref[...].T
    for j in range(gtok):
        f_ref[j:j + 1, :] = _expert_token(rows_ref[j * per_tok:(j + 1) * per_tok, :], h_ref[j:j + 1, :],
                                          gate_t[:, j:j + 1])
    o_ref[...] = _ln(alpha * x1_ref[...] + g2_ref[0] * f_ref[...]) * lng_ref[...] + lnb_ref[...]


def _experts_dense(rows, gate, h2, x1, g2, lng, lnb, *, tok0, gtok, seq, alpha):
    N, D = h2.shape
    per_tok = PEER_HEADS * PEER_TOPK
    n_tok = N - tok0
    off = tok0 // gtok
    steps_per_batch = seq // gtok
    return pl.pallas_call(
        functools.partial(_expert_dense_kernel, gtok=gtok, alpha=alpha),
        grid=(n_tok // gtok,),
        in_specs=[pl.BlockSpec((gtok * per_tok, D), lambda i: (i, 0)),
                  pl.BlockSpec((gtok, D), lambda i: (i + off, 0)),
                  pl.BlockSpec((gtok, per_tok), lambda i: (i + off, 0)),
                  pl.BlockSpec((gtok, D), lambda i: (i + off, 0)),
                  pl.BlockSpec((1, 1, D), lambda i: ((i + off) // steps_per_batch, 0, 0)),
                  pl.BlockSpec((1, D), lambda i: (0, 0)),
                  pl.BlockSpec((1, D), lambda i: (0, 0))],
        out_specs=pl.BlockSpec((gtok, D), lambda i: (i, 0)),
        out_shape=jax.ShapeDtypeStruct((n_tok, D), F32),
        scratch_shapes=[pltpu.VMEM((gtok, D), F32)],
        compiler_params=_cparams(("arbitrary",)),
        name="experts_dense",
    )(rows, h2, gate, x1, g2, lng, lnb)


SC_WORKERS = 32
SC_IDX_BLOCK = 128
SC_CHUNK = 32


def _sc_gather_rows(tab, idx2d, row0):
    from jax.experimental.pallas import tpu_sc as plsc
    n_rows = idx2d.shape[0] - row0
    D = tab.shape[1]
    steps = n_rows // SC_WORKERS
    n_chunks = SC_IDX_BLOCK // SC_CHUNK
    mesh = plsc.VectorSubcoreMesh(core_axis_name="core", subcore_axis_name="subcore")

    @pl.kernel(out_type=jax.ShapeDtypeStruct((n_rows * SC_IDX_BLOCK, D), tab.dtype), mesh=mesh,
               scratch_types=[pltpu.VMEM((1, SC_IDX_BLOCK), jnp.int32),
                              pltpu.VMEM((2, SC_CHUNK, D), tab.dtype),
                              pltpu.SemaphoreType.DMA((2,))])
    def k(x_hbm, i_hbm, o_hbm, i_vmem, buf, wsem):
        wid = lax.axis_index("core") * (SC_WORKERS // 2) + lax.axis_index("subcore")

        def write(row, c):
            return pltpu.make_async_copy(buf.at[c % 2], o_hbm.at[pl.ds(row * SC_IDX_BLOCK + c * SC_CHUNK, SC_CHUNK)],
                                         wsem.at[c % 2])

        @pl.loop(0, steps)
        def _(st):
            row = wid * steps + st
            pltpu.sync_copy(i_hbm.at[pl.ds(row0 + row, 1)], i_vmem)
            for c in range(n_chunks):
                if c >= 2:
                    write(row, c - 2).wait()
                pltpu.sync_copy(x_hbm.at[i_vmem.at[0, pl.ds(c * SC_CHUNK, SC_CHUNK)]], buf.at[c % 2])
                write(row, c).start()
            for c in range(n_chunks - 2, n_chunks):
                write(row, c).wait()
    return k(tab, idx2d)


def _rope_tables(L):
    half = HEAD_DIM // 4
    t = np.arange(L)
    freqs = ROPE_THETA ** (-np.arange(half, dtype=np.float32) / half)
    lane = np.arange(HEAD_DIM)
    pos = np.where(lane[None, :] < HEAD_DIM // 2, (t // GRID_W)[:, None], (t % GRID_W)[:, None]).astype(np.float32)
    ang = pos * freqs[lane % half][None, :].astype(np.float32)
    cos, sin = np.cos(ang), np.sin(ang)
    first = ((lane % (2 * half)) < half)[None, :]
    slo = np.where(first, -sin, 0.0)
    shi = np.where(first, 0.0, sin)
    tile = lambda a: jnp.asarray(np.tile(a, (1, N_KV_HEADS)), F32)
    return tile(cos), tile(slo), tile(shi)


def _pack_tables(u, v):
    ub = lax.bitcast_convert_type(u.astype(BF16), jnp.uint16).astype(jnp.uint32)
    vb = lax.bitcast_convert_type(v.astype(BF16), jnp.uint16).astype(jnp.uint32)
    return (ub << 16) | vb


def _pick(n, prefs):
    for p in prefs:
        if n % p == 0:
            return p
    return n


def kernel(x, c, ctx, c_ctx, w_mod, b_mod, w_in, q_norm_g, k_norm_g, gmlp_ln_g, gmlp_ln_b, gmlp_w_s, gmlp_b_s,
           conv_w_dw, conv_b_dw, conv_ln_g, conv_ln_b, conv_w_pw, conv_b_pw, w_out, ln1_g, ln1_b, ln2_g, ln2_b,
           peer_w_q, peer_sub_keys, peer_u, peer_v):
    B, L, D = x.shape
    LC = ctx.shape[1]
    depth = w_mod.shape[0]
    alpha = (2 * depth) ** 0.25
    gw = gmlp_ln_g.shape[-1]

    cos, slo, shi = _rope_tables(L)
    ones_c = jnp.ones((LC, KV_WIDTH), F32)
    zeros_c = jnp.zeros((LC, KV_WIDTH), F32)
    gidx = np.arange(ATTN_WIDTH) // HEAD_DIM
    gmat = jnp.asarray((gidx[:, None] == gidx[None, :]) / HEAD_DIM, BF16)

    cond = jnp.concatenate([c, c_ctx[None, :], jnp.zeros((-(B + 1) % 8, D), F32)], axis=0)

    t_in = _pick(L, (512, 256, 128))
    t_mix = _pick(L, (256, 128))
    t_inc = _pick(LC, (512, 256, 128))
    t_mixc = _pick(LC, (256, 128))
    tk_lat = _pick(L + LC, (768, 512, 384, 256, 128))
    tk_ctx = _pick(LC, (512, 256, 128))
    gtok = 16

    def peer(h2, x1, g2, l, seq, wq_bf, keys_bf, tab):
        n = h2.shape[0] * h2.shape[1]
        gate, eidx = _retrieval(h2.reshape(n, D), wq_bf, keys_bf, tile=_pick(n, (512, 256, 128)))
        h2f, x1f = h2.reshape(n, D), x1.reshape(n, D)
        lng, lnb = ln2_g[l].reshape(1, D), ln2_b[l].reshape(1, D)
        n_sc = (n // 2) // (SC_WORKERS * gtok) * (SC_WORKERS * gtok) if n >= 8192 else 0
        out = _experts(eidx.reshape(n // gtok, 1, -1), gate, tab, h2f, x1f, g2, lng, lnb,
                       n_tok=n - n_sc, gtok=gtok, seq=seq, alpha=alpha)
        if n_sc:
            rows = _sc_gather_rows(tab.reshape(-1, D), eidx, n - n_sc)
            out_sc = _experts_dense(rows, gate, h2f, x1f, g2, lng, lnb, tok0=n - n_sc, gtok=gtok, seq=seq,
                                    alpha=alpha)
            out = jnp.concatenate([out, out_sc], axis=0)
        return out.reshape(h2.shape)

    x_lat, x_ctx = x, ctx
    for l in range(depth):
        last = l == depth - 1
        mod = _modulation(cond, w_mod[l], b_mod[l])
        m_lat = [mod[:B, k * D:(k + 1) * D].reshape(B, 1, D) for k in range(6)]
        m_ctx = [jnp.broadcast_to(mod[B:B + 1, k * D:(k + 1) * D].reshape(1, 1, D), (B, 1, D)) for k in range(6)]

        w_in_bf = w_in[l].astype(BF16)
        gq = jnp.tile(q_norm_g[l], N_Q_HEADS).reshape(1, ATTN_WIDTH)
        gk = jnp.tile(k_norm_g[l], N_KV_HEADS).reshape(1, KV_WIDTH)
        mix_prm = (gmlp_ln_g[l].reshape(1, gw), gmlp_ln_b[l].reshape(1, gw),
                   gmlp_w_s[l].astype(BF16),
                   jnp.repeat(gmlp_b_s[l].T, gw // GMLP_GROUPS, axis=1),
                   conv_w_dw[l], conv_b_dw[l].reshape(1, -1), conv_ln_g[l].reshape(1, -1),
                   conv_ln_b[l].reshape(1, -1), conv_w_pw[l].astype(BF16), conv_b_pw[l].reshape(1, -1),
                   w_out[l].astype(BF16), ln1_g[l].reshape(1, D), ln1_b[l].reshape(1, D))
        wq_bf = peer_w_q[l].astype(BF16)
        keys_bf = peer_sub_keys[l].reshape(2 * PEER_HEADS, N_KEYS, PEER_HALF).astype(BF16)
        tab = _pack_tables(peer_u[l], peer_v[l]).reshape(-1, 1, D)

        q_l, kt_l, v_l, zg_l, zc_l = _inproj(x_lat, m_lat[0], m_lat[1], w_in_bf, gq, gk, cos, slo, shi, gmat,
                                              tile=t_in)
        q_c, kt_c, v_c, zg_c, zc_c = _inproj(x_ctx, m_ctx[0], m_ctx[1], w_in_bf, gq, gk, ones_c, zeros_c, zeros_c,
                                              gmat, tile=t_inc)
        kt_all = jnp.concatenate([kt_c, kt_l], axis=3)
        v_all = jnp.concatenate([v_c, v_l], axis=1)
        o_lat = _attention(q_l, kt_all, v_all, tq=128, tk=tk_lat)
        x1, h2 = _mixers(x_lat, o_lat, zg_l, zc_l, m_lat[2], m_lat[3], m_lat[4], mix_prm, tile=t_mix, alpha=alpha)
        x_lat_new = peer(h2, x1, m_lat[5], l, L, wq_bf, keys_bf, tab)

        if not last:
            o_ctx = _attention(q_c, kt_c, v_c, tq=128, tk=tk_ctx)
            x1c, h2c = _mixers(x_ctx, o_ctx, zg_c, zc_c, m_ctx[2], m_ctx[3], m_ctx[4], mix_prm, tile=t_mixc,
                               alpha=alpha)
            x_ctx = peer(h2c, x1c, m_ctx[5], l, LC, wq_bf, keys_bf, tab)
        x_lat = x_lat_new
    return x_lat
```

```python
import functools
import math

import jax
import jax.numpy as jnp
import numpy as np
from jax import lax
from jax.experimental import pallas as pl
from jax.experimental.pallas import tpu as pltpu

F32 = jnp.float32
BF16 = jnp.bfloat16

EPS = 1e-6
GRID_W = 64
N_Q_HEADS = 8
N_KV_HEADS = 2
Q_PER_KV = N_Q_HEADS // N_KV_HEADS
HEAD_DIM = 64
ATTN_WIDTH = N_Q_HEADS * HEAD_DIM
KV_WIDTH = N_KV_HEADS * HEAD_DIM
QKV_WIDTH = ATTN_WIDTH + 2 * KV_WIDTH
ROPE_THETA = 10000.0
CHUNK = 128
GMLP_GROUPS = 4
CONV_K = 31
CONV_HALO = 16
PEER_HEADS = 8
N_KEYS = 128
PEER_TOPK = 16
PEER_HALF = 128

VMEM_LIMIT = 56 * 1024 * 1024


def _cparams(sem):
    return pltpu.CompilerParams(dimension_semantics=sem, vmem_limit_bytes=VMEM_LIMIT)


def _gelu_tanh(x):
    return 0.5 * x * (1.0 + jnp.tanh(math.sqrt(2.0 / math.pi) * (x + 0.044715 * (x * x * x))))


def _ln(x):
    mu = jnp.mean(x, axis=-1, keepdims=True)
    xc = x - mu
    var = jnp.mean(xc * xc, axis=-1, keepdims=True)
    return xc * lax.rsqrt(var + EPS)


def _mod_kernel(c_ref, w_ref, b_ref, o_ref):
    c = c_ref[...]
    s = c * jax.nn.sigmoid(c)
    o_ref[...] = jnp.dot(s, w_ref[...], preferred_element_type=F32,
                         precision=lax.Precision.HIGHEST) + b_ref[...]


def _modulation(cond, w_mod, b_mod):
    R, D = cond.shape
    N = w_mod.shape[1]
    tn = 1024
    return pl.pallas_call(
        _mod_kernel,
        grid=(N // tn,),
        in_specs=[pl.BlockSpec((R, D), lambda j: (0, 0)),
                  pl.BlockSpec((D, tn), lambda j: (0, j)),
                  pl.BlockSpec((1, tn), lambda j: (0, j))],
        out_specs=pl.BlockSpec((R, tn), lambda j: (0, j)),
        out_shape=jax.ShapeDtypeStruct((R, N), F32),
        compiler_params=_cparams(("arbitrary",)),
        name="modulation",
    )(cond, w_mod, b_mod.reshape(1, N))


def _group_mean_sq(t, gmat):
    sq = t * t
    hi = sq.astype(BF16)
    lo = (sq - hi.astype(F32)).astype(BF16)
    return (jnp.dot(hi, gmat, preferred_element_type=F32)
            + jnp.dot(lo, gmat, preferred_element_type=F32))


def _rope(t, cos, sin_lo, sin_hi):
    w = t.shape[-1]
    up = pltpu.roll(t, w - 16, axis=1)
    dn = pltpu.roll(t, 16, axis=1)
    return t * cos + up * sin_lo + dn * sin_hi


def _inproj_kernel(x_ref, sh_ref, sc_ref, w_ref, gq_ref, gk_ref, cos_ref, slo_ref, shi_ref, gmat_ref,
                   q_ref, kt_ref, v_ref, zg_ref, zc_ref):
    h = _ln(x_ref[0]) * (1.0 + sc_ref[0]) + sh_ref[0]
    p = jnp.dot(h.astype(BF16), w_ref[...], preferred_element_type=F32)
    q = p[:, :ATTN_WIDTH]
    k = p[:, ATTN_WIDTH:ATTN_WIDTH + KV_WIDTH]
    v = p[:, ATTN_WIDTH + KV_WIDTH:QKV_WIDTH]
    gm_w = (p.shape[1] - QKV_WIDTH) // 2
    zg_ref[0] = p[:, QKV_WIDTH:QKV_WIDTH + gm_w]
    zc_ref[0] = p[:, QKV_WIDTH + gm_w:]
    v_ref[0] = v.astype(BF16)

    gmat = gmat_ref[...]
    cos, slo, shi = cos_ref[...], slo_ref[...], shi_ref[...]
    qn = q * lax.rsqrt(_group_mean_sq(q, gmat) + EPS) * gq_ref[...]
    rep = ATTN_WIDTH // KV_WIDTH
    qr = _rope(qn, jnp.concatenate([cos] * rep, axis=1), jnp.concatenate([slo] * rep, axis=1),
               jnp.concatenate([shi] * rep, axis=1))
    qr = (qr * (HEAD_DIM ** -0.5 * math.log2(math.e))).astype(BF16)
    for hh in range(N_Q_HEADS):
        q_ref[0, hh] = qr[:, hh * HEAD_DIM:(hh + 1) * HEAD_DIM]

    kn = k * lax.rsqrt(_group_mean_sq(k, gmat[:KV_WIDTH, :KV_WIDTH]) + EPS) * gk_ref[...]
    kt = _rope(kn, cos, slo, shi).T
    for g in range(N_KV_HEADS):
        kt_ref[0, g] = kt[g * HEAD_DIM:(g + 1) * HEAD_DIM, :].astype(BF16)


def _inproj(x, sh, sc, w_in_bf, gq, gk, cos, slo, shi, gmat, *, tile):
    B, L, D = x.shape
    W = w_in_bf.shape[1]
    gm_w = (W - QKV_WIDTH) // 2
    nt = L // tile
    bmap = lambda b, i: (b, 0, 0)
    cmap = lambda b, i: (0, 0)
    return pl.pallas_call(
        _inproj_kernel,
        grid=(B, nt),
        in_specs=[pl.BlockSpec((1, tile, D), lambda b, i: (b, i, 0)),
                  pl.BlockSpec((1, 1, D), bmap), pl.BlockSpec((1, 1, D), bmap),
                  pl.BlockSpec((D, W), cmap),
                  pl.BlockSpec((1, ATTN_WIDTH), cmap), pl.BlockSpec((1, KV_WIDTH), cmap),
                  pl.BlockSpec((tile, KV_WIDTH), lambda b, i: (i, 0)),
                  pl.BlockSpec((tile, KV_WIDTH), lambda b, i: (i, 0)),
                  pl.BlockSpec((tile, KV_WIDTH), lambda b, i: (i, 0)),
                  pl.BlockSpec((ATTN_WIDTH, ATTN_WIDTH), cmap)],
        out_specs=[pl.BlockSpec((1, N_Q_HEADS, tile, HEAD_DIM), lambda b, i: (b, 0, i, 0)),
                   pl.BlockSpec((1, N_KV_HEADS, HEAD_DIM, tile), lambda b, i: (b, 0, 0, i)),
                   pl.BlockSpec((1, tile, KV_WIDTH), lambda b, i: (b, i, 0)),
                   pl.BlockSpec((1, tile, gm_w), lambda b, i: (b, i, 0)),
                   pl.BlockSpec((1, tile, gm_w), lambda b, i: (b, i, 0))],
        out_shape=[jax.ShapeDtypeStruct((B, N_Q_HEADS, L, HEAD_DIM), BF16),
                   jax.ShapeDtypeStruct((B, N_KV_HEADS, HEAD_DIM, L), BF16),
                   jax.ShapeDtypeStruct((B, L, KV_WIDTH), BF16),
                   jax.ShapeDtypeStruct((B, L, gm_w), F32),
                   jax.ShapeDtypeStruct((B, L, gm_w), F32)],
        compiler_params=_cparams(("arbitrary", "arbitrary")),
        name="inproj",
    )(x, sh, sc, w_in_bf, gq, gk, cos, slo, shi, gmat)


def _attn_kernel(q_ref, kt_ref, vx_ref, o_ref, p_ref, *, tq, tk, nk):
    m_rows = Q_PER_KV * tq
    qs = [q_ref[0, Q_PER_KV * g:Q_PER_KV * (g + 1)].reshape(m_rows, HEAD_DIM) for g in range(N_KV_HEADS)]

    def pv(c, par, g):
        off = pl.multiple_of(c * tk, tk)
        return jnp.dot(p_ref[par, g], vx_ref[0, g, pl.ds(off, tk), :], preferred_element_type=F32)

    def step(c, par, carry):
        off = pl.multiple_of(c * tk, tk)
        out = []
        for g in range(N_KV_HEADS):
            m, acc = carry[g]
            s = jnp.dot(qs[g], kt_ref[0, g, :, pl.ds(off, tk)], preferred_element_type=F32)
            m_new = jnp.maximum(m, jnp.max(s, axis=-1, keepdims=True))
            p_ref[1 - par, g] = jnp.exp2(s - m_new).astype(BF16)
            if acc is not None:
                acc = jnp.exp2(m - m_new) * (acc + pv(c - 1, par, g))
            out.append((m_new, acc))
        return tuple(out)

    first = step(0, 1, tuple((jnp.full((m_rows, 1), -1e30, F32), None) for _ in range(N_KV_HEADS)))
    carry = tuple((m, jnp.zeros((m_rows, 2 * HEAD_DIM), F32)) for m, _ in first)

    n_pairs = (nk - 1) // 2
    def pair(t, carry):
        carry = step(2 * t + 1, 0, carry)
        return step(2 * t + 2, 1, carry)
    carry = lax.fori_loop(0, n_pairs, pair, carry)
    par = 0
    if (nk - 1) % 2:
        carry = step(nk - 1, 0, carry)
        par = 1

    for g in range(N_KV_HEADS):
        acc = carry[g][1] + pv(nk - 1, par, g)
        og = acc / pltpu.roll(acc, HEAD_DIM, axis=1)
        for r in range(Q_PER_KV):
            hh = Q_PER_KV * g + r
            o_ref[0, :, hh * HEAD_DIM:(hh + 1) * HEAD_DIM] = og[r * tq:(r + 1) * tq, :HEAD_DIM].astype(o_ref.dtype)


def _attention(q, kt, v, *, tq, tk):
    B, _, L, _ = q.shape
    LK = kt.shape[-1]
    ones = jnp.ones((B, LK, HEAD_DIM), v.dtype)
    vx = jnp.stack([jnp.concatenate([v[..., g * HEAD_DIM:(g + 1) * HEAD_DIM], ones], axis=-1)
                    for g in range(N_KV_HEADS)], axis=1)
    return pl.pallas_call(
        functools.partial(_attn_kernel, tq=tq, tk=tk, nk=LK // tk),
        grid=(B, L // tq),
        in_specs=[pl.BlockSpec((1, N_Q_HEADS, tq, HEAD_DIM), lambda b, i: (b, 0, i, 0)),
                  pl.BlockSpec((1, N_KV_HEADS, HEAD_DIM, LK), lambda b, i: (b, 0, 0, 0)),
                  pl.BlockSpec((1, N_KV_HEADS, LK, 2 * HEAD_DIM), lambda b, i: (b, 0, 0, 0))],
        out_specs=pl.BlockSpec((1, tq, ATTN_WIDTH), lambda b, i: (b, i, 0)),
        out_shape=jax.ShapeDtypeStruct((B, L, ATTN_WIDTH), BF16),
        scratch_shapes=[pltpu.VMEM((2, N_KV_HEADS, Q_PER_KV * tq, tk), BF16)],
        compiler_params=_cparams(("arbitrary", "arbitrary")),
        name="attention",
    )(q, kt, vx)


def _mix_kernel(x_ref, o_ref, zg_ref, zc_ref, zp_ref, zn_ref, g1_ref, sh2_ref, sc2_ref,
                gln_g_ref, gln_b_ref, ws_ref, bs_ref, wdw_ref, bdw_ref, cln_g_ref, cln_b_ref,
                wpw_ref, bpw_ref, wout_ref, ln1g_ref, ln1b_ref,
                x1_ref, h2_ref, ybuf_ref, *, tile, alpha):
    i = pl.program_id(1)
    nt = pl.num_programs(1)
    gw = zg_ref.shape[-1] // 2
    cw = zc_ref.shape[-1] // 2

    z = _gelu_tanh(zg_ref[0])
    u, v = z[:, :gw], z[:, gw:]
    vn = (_ln(v) * gln_g_ref[...] + gln_b_ref[...]).astype(BF16)
    grp = lax.broadcasted_iota(jnp.int32, (CHUNK, gw), 1) // (gw // GMLP_GROUPS)
    sg = []
    for n in range(tile // CHUNK):
        vc = vn[n * CHUNK:(n + 1) * CHUNK]
        s = jnp.zeros((CHUNK, gw), F32)
        for g in range(GMLP_GROUPS):
            s = jnp.where(grp == g, jnp.dot(ws_ref[g], vc, preferred_element_type=F32), s)
        sg.append(s + bs_ref[...])
    y_g = u * jnp.concatenate(sg, axis=0)

    def glu(zz):
        return zz[:, :cw] * jax.nn.sigmoid(zz[:, cw:])

    ybuf_ref[0:CONV_HALO] = glu(zp_ref[0]) * (i > 0).astype(F32)
    ybuf_ref[CONV_HALO:CONV_HALO + tile] = glu(zc_ref[0])
    ybuf_ref[CONV_HALO + tile:2 * CONV_HALO + tile] = glu(zn_ref[0]) * (i < nt - 1).astype(F32)
    base = CONV_HALO - CONV_K // 2
    conv = jnp.zeros((tile, cw), F32)
    for kk in range(CONV_K):
        conv = conv + ybuf_ref[base + kk:base + kk + tile, :] * wdw_ref[kk:kk + 1, :]
    conv = conv + bdw_ref[...]
    t = _ln(conv) * cln_g_ref[...] + cln_b_ref[...]
    t = t * jax.nn.sigmoid(t)
    y_c = jnp.dot(t.astype(BF16), wpw_ref[...], preferred_element_type=F32) + bpw_ref[...]

    aw = o_ref.shape[-1]
    y = (jnp.dot(o_ref[0], wout_ref[0:aw, :], preferred_element_type=F32)
         + jnp.dot(y_g.astype(BF16), wout_ref[aw:aw + gw, :], preferred_element_type=F32)
         + jnp.dot(y_c.astype(BF16), wout_ref[aw + gw:, :], preferred_element_type=F32))
    x1 = _ln(alpha * x_ref[0] + g1_ref[0] * y) * ln1g_ref[...] + ln1b_ref[...]
    x1_ref[0] = x1
    h2_ref[0] = _ln(x1) * (1.0 + sc2_ref[0]) + sh2_ref[0]


def _mixers(x, o, zg, zc, g1, sh2, sc2, prm, *, tile, alpha):
    B, L, D = x.shape
    nt = L // tile
    hb = tile // CONV_HALO
    nhb = L // CONV_HALO
    gw2, cw2 = zg.shape[-1], zc.shape[-1]
    tmap = lambda b, i: (b, i, 0)
    bmap = lambda b, i: (b, 0, 0)
    c2 = lambda b, i: (0, 0)
    c3 = lambda b, i: (0, 0, 0)
    in_specs = [pl.BlockSpec((1, tile, D), tmap),
                pl.BlockSpec((1, tile, o.shape[-1]), tmap),
                pl.BlockSpec((1, tile, gw2), tmap),
                pl.BlockSpec((1, tile, cw2), tmap),
                pl.BlockSpec((1, CONV_HALO, cw2), lambda b, i: (b, jnp.maximum(i * hb - 1, 0), 0)),
                pl.BlockSpec((1, CONV_HALO, cw2), lambda b, i: (b, jnp.minimum((i + 1) * hb, nhb - 1), 0)),
                pl.BlockSpec((1, 1, D), bmap), pl.BlockSpec((1, 1, D), bmap), pl.BlockSpec((1, 1, D), bmap)]
    for a in prm:
        in_specs.append(pl.BlockSpec(a.shape, c2 if a.ndim == 2 else c3))
    return pl.pallas_call(
        functools.partial(_mix_kernel, tile=tile, alpha=alpha),
        grid=(B, nt),
        in_specs=in_specs,
        out_specs=[pl.BlockSpec((1, tile, D), tmap), pl.BlockSpec((1, tile, D), tmap)],
        out_shape=[jax.ShapeDtypeStruct((B, L, D), F32), jax.ShapeDtypeStruct((B, L, D), F32)],
        scratch_shapes=[pltpu.VMEM((tile + 2 * CONV_HALO, cw2 // 2), F32)],
        compiler_params=_cparams(("arbitrary", "arbitrary")),
        name="mixers",
    )(x, o, zg, zc, zc, zc, g1, sh2, sc2, *prm)


def _oddeven_merge_sort_pairs(n):
    pairs = []
    p = 1
    while p < n:
        k = p
        while k >= 1:
            for j in range(k % p, n - k, 2 * k):
                for i in range(min(k, n - j - k)):
                    if (i + j) // (2 * p) == (i + j + k) // (2 * p):
                        pairs.append((i + j, i + j + k))
            k //= 2
        p *= 2
    return pairs


_SORT16 = _oddeven_merge_sort_pairs(16)
_BITONIC16 = [(i, i + d) for d in (8, 4, 2, 1) for i in range(16) if (i & d) == 0]


def _before(a, ia, b, ib):
    return (a > b) | ((a == b) & (ia < ib))


def _cmpx(vals, keys, pay, i, j):
    keep = _before(vals[i], keys[i], vals[j], keys[j])
    vals[i], vals[j] = jnp.where(keep, vals[i], vals[j]), jnp.where(keep, vals[j], vals[i])
    keys[i], keys[j] = jnp.where(keep, keys[i], keys[j]), jnp.where(keep, keys[j], keys[i])
    if pay is not None:
        pay[i], pay[j] = jnp.where(keep, pay[i], pay[j]), jnp.where(keep, pay[j], pay[i])


def _top16_merge(av, ak, ap, bv, bk, bp):
    n = len(av)
    ov, ok, op = [], [], []
    for i in range(n):
        keep = _before(av[i], ak[i], bv[n - 1 - i], bk[n - 1 - i])
        ov.append(jnp.where(keep, av[i], bv[n - 1 - i]))
        ok.append(jnp.where(keep, ak[i], bk[n - 1 - i]))
        if ap is not None:
            op.append(jnp.where(keep, ap[i], bp[n - 1 - i]))
    return ov, ok, (op if ap is not None else None)


def _stage2_candidates():
    return [(a, b) for a in range(PEER_TOPK) for b in range(PEER_TOPK) if (a + 1) * (b + 1) <= PEER_TOPK]


def _retr_kernel(h_ref, wq_ref, keys_ref, gate_ref, eidx_ref, q_s, sv_s, si_s, *, tile):
    q_s[...] = jnp.dot(h_ref[...].astype(BF16), wq_ref[...], preferred_element_type=F32)
    sub = lax.broadcasted_iota(jnp.int32, (8, tile), 0)
    sv_s[...] = jnp.zeros(sv_s.shape, F32)
    si_s[...] = jnp.zeros(si_s.shape, jnp.int32)

    def stage1(hp, _):
        qh = q_s[:, pl.ds(pl.multiple_of(hp * PEER_HALF, PEER_HALF), PEER_HALF)].astype(BF16)
        st = lax.dot_general(keys_ref[hp], qh, (((1,), (1,)), ((), ())),
                             preferred_element_type=F32)
        st = st.reshape(N_KEYS // 8, 8, tile)
        vals = [st[i] for i in range(16)]
        keys = [sub + 8 * i for i in range(16)]
        for (i, j) in _SORT16:
            _cmpx(vals, keys, None, i, j)
        for shift in (4, 2, 1):
            bv = [pltpu.roll(t, shift, axis=0) for t in vals]
            bk = [pltpu.roll(t, shift, axis=0) for t in keys]
            vals, keys, _ = _top16_merge(vals, keys, None, bv, bk, None)
            for (i, j) in _BITONIC16:
                _cmpx(vals, keys, None, i, j)
        head = hp // 2
        half = hp % 2
        for i in range(16):
            sv_s[half, i] = jnp.where(sub == head, vals[i], sv_s[half, i])
            si_s[half, i] = jnp.where(sub == head, keys[i], si_s[half, i])
        return 0

    lax.fori_loop(0, 2 * PEER_HEADS, stage1, 0)

    cands = _stage2_candidates()
    n_pad = 64
    vals, poss, exps = [], [], []
    for (a, b) in cands:
        vals.append(sv_s[0, a] + sv_s[1, b])
        poss.append(jnp.full((8, tile), a * PEER_TOPK + b, jnp.int32))
        exps.append(si_s[0, a] * N_KEYS + si_s[1, b])
    for t in range(n_pad - len(cands)):
        vals.append(jnp.full((8, tile), -jnp.inf, F32))
        poss.append(jnp.full((8, tile), PEER_TOPK * PEER_TOPK + t, jnp.int32))
        exps.append(jnp.zeros((8, tile), jnp.int32))
    groups = []
    for g in range(n_pad // 16):
        gv, gk, gp = vals[16 * g:16 * g + 16], poss[16 * g:16 * g + 16], exps[16 * g:16 * g + 16]
        for (i, j) in _SORT16:
            _cmpx(gv, gk, gp, i, j)
        groups.append((gv, gk, gp))
    while len(groups) > 1:
        nxt = []
        for g in range(0, len(groups), 2):
            mv, mk, mp = _top16_merge(*groups[g], *groups[g + 1])
            if len(groups) > 2:
                for (i, j) in _BITONIC16:
                    _cmpx(mv, mk, mp, i, j)
            nxt.append((mv, mk, mp))
        groups = nxt
    best, _, experts = groups[0]

    mx = functools.reduce(jnp.maximum, best)
    ex = [jnp.exp(b - mx) for b in best]
    inv = 1.0 / functools.reduce(lambda a, b: a + b, ex)
    gate_ref[...] = jnp.concatenate([e * inv for e in ex], axis=0).T
    eidx_ref[...] = jnp.concatenate(experts, axis=0).T


def _retrieval(h2, wq_bf, keys_bf, *, tile):
    N, D = h2.shape
    QW = wq_bf.shape[1]
    return pl.pallas_call(
        functools.partial(_retr_kernel, tile=tile),
        grid=(N // tile,),
        in_specs=[pl.BlockSpec((tile, D), lambda i: (i, 0)),
                  pl.BlockSpec((D, QW), lambda i: (0, 0)),
                  pl.BlockSpec(keys_bf.shape, lambda i: (0, 0, 0))],
        out_specs=[pl.BlockSpec((tile, PEER_TOPK * PEER_HEADS), lambda i: (i, 0)),
                   pl.BlockSpec((tile, PEER_TOPK * PEER_HEADS), lambda i: (i, 0))],
        out_shape=[jax.ShapeDtypeStruct((N, PEER_TOPK * PEER_HEADS), F32),
                   jax.ShapeDtypeStruct((N, PEER_TOPK * PEER_HEADS), jnp.int32)],
        scratch_shapes=[pltpu.VMEM((tile, QW), F32),
                        pltpu.VMEM((2, PEER_TOPK, PEER_HEADS, tile), F32),
                        pltpu.VMEM((2, PEER_TOPK, PEER_HEADS, tile), jnp.int32)],
        compiler_params=_cparams(("arbitrary",)),
        name="retrieval",
    )(h2, wq_bf, keys_bf)


def _expert_token(r, h_row, gate_col):
    u = pltpu.bitcast(r & jnp.uint32(0xFFFF0000), F32)
    v = pltpu.bitcast(r << 16, F32)
    d = jnp.sum(u * h_row, axis=-1, keepdims=True)
    w = jnp.broadcast_to(gate_col * _gelu_tanh(d), r.shape)
    return jnp.sum(v * w, axis=0, keepdims=True)


def _expert_kernel(*refs, gtok, alpha, dense):
    if dense:
        (idx0_ref, idxn_ref, tab_ref, h_ref, gate_ref, x1_ref, g2_ref, rows_ref, hd_ref, gated_ref, x1d_ref, g2d_ref,
         lng_ref, lnb_ref, o_ref, od_ref, buf_a, buf_b, f_ref, sem_ref) = refs
    else:
        (idx0_ref, idxn_ref, tab_ref, h_ref, gate_ref, x1_ref, g2_ref, lng_ref, lnb_ref,
         o_ref, buf_a, buf_b, f_ref, sem_ref) = refs
    i = pl.program_id(0)
    n_steps = pl.num_programs(0)
    per_tok = PEER_HEADS * PEER_TOPK
    d_model = h_ref.shape[-1]

    def start_token(idx_ref, buf, sem_row, j):
        for g in range(per_tok // 8):
            for t in range(8):
                e = idx_ref[0, 0, j * per_tok + g * 8 + t]
                pltpu.make_async_copy(tab_ref.at[e], buf.at[j, g, pl.ds(t, 1), :],
                                      sem_ref.at[sem_row, j]).start(priority=t % 2)

    def wait_token(buf, other, sem_row, j):
        pltpu.make_async_copy(other.at[j], buf.at[j], sem_ref.at[sem_row, j]).wait()

    def token(j, cur, nxt, cur_row):
        wait_token(cur, nxt, cur_row, j)
        start_token(idxn_ref, nxt, 1 - cur_row, j)
        f_ref[j:j + 1, :] = _expert_token(cur[j].reshape(per_tok, d_model), h_ref[j:j + 1, :],
                                          gate_t[:, j:j + 1])

    @pl.when(i == 0)
    def _():
        def prime(j, _):
            start_token(idx0_ref, buf_a, 0, j)
            return 0
        lax.fori_loop(0, gtok, prime, 0)

    gate_t = gate_ref[...].T

    @pl.when(i % 2 == 0)
    def _():
        for j in range(gtok):
            token(j, buf_a, buf_b, 0)

    @pl.when(i % 2 == 1)
    def _():
        for j in range(gtok):
            token(j, buf_b, buf_a, 1)

    @pl.when((i == n_steps - 1) & (i % 2 == 0))
    def _():
        lax.fori_loop(0, gtok, lambda j, c: (wait_token(buf_b, buf_a, 1, j), 0)[1], 0)

    @pl.when((i == n_steps - 1) & (i % 2 == 1))
    def _():
        lax.fori_loop(0, gtok, lambda j, c: (wait_token(buf_a, buf_b, 0, j), 0)[1], 0)

    o_ref[...] = _ln(alpha * x1_ref[...] + g2_ref[0] * f_ref[...]) * lng_ref[...] + lnb_ref[...]

    if dense:
        gated_t = gated_ref[...].T
        for j in range(gtok):
            f_ref[j:j + 1, :] = _expert_token(rows_ref[j * per_tok:(j + 1) * per_tok, :], hd_ref[j:j + 1, :],
                                              gated_t[:, j:j + 1])
        od_ref[...] = _ln(alpha * x1d_ref[...] + g2d_ref[0] * f_ref[...]) * lng_ref[...] + lnb_ref[...]


def _experts(eidx, gate, tab, h2, x1, g2, lng, lnb, *, tok0, n_tok, gtok, seq, alpha, dense_rows=None,
             dense_tok0=0):
    D = h2.shape[1]
    S = n_tok // gtok
    off = tok0 // gtok
    doff = dense_tok0 // gtok
    per_tok = PEER_HEADS * PEER_TOPK
    rows = gtok * per_tok
    spb = seq // gtok
    smem = pltpu.MemorySpace.SMEM
    dense = dense_rows is not None
    tok_spec = lambda o: pl.BlockSpec((gtok, D), lambda i: (i + o, 0))
    gate_spec = lambda o: pl.BlockSpec((gtok, per_tok), lambda i: (i + o, 0))
    g2_spec = lambda o: pl.BlockSpec((1, 1, D), lambda i: ((i + o) // spb, 0, 0))
    row_spec = pl.BlockSpec((1, D), lambda i: (0, 0))
    in_specs = [pl.BlockSpec((1, 1, rows), lambda i: (off, 0, 0), memory_space=smem),
                pl.BlockSpec((1, 1, rows), lambda i: (jnp.minimum(i + 1, S - 1) + off, 0, 0), memory_space=smem),
                pl.BlockSpec(memory_space=pl.ANY),
                tok_spec(off), gate_spec(off), tok_spec(off), g2_spec(off)]
    args = [eidx, eidx, tab, h2, gate, x1, g2]
    out_spec = pl.BlockSpec((gtok, D), lambda i: (i, 0))
    out_shape = jax.ShapeDtypeStruct((n_tok, D), F32)
    if dense:
        in_specs += [pl.BlockSpec((rows, D), lambda i: (i, 0)), tok_spec(doff), gate_spec(doff), tok_spec(doff),
                     g2_spec(doff)]
        args += [dense_rows, h2, gate, x1, g2]
    return pl.pallas_call(
        functools.partial(_expert_kernel, gtok=gtok, alpha=alpha, dense=dense),
        grid=(S,),
        in_specs=in_specs + [row_spec, row_spec],
        out_specs=[out_spec, out_spec] if dense else out_spec,
        out_shape=[out_shape, out_shape] if dense else out_shape,
        scratch_shapes=[pltpu.VMEM((gtok, per_tok // 8, 8, D), jnp.uint32),
                        pltpu.VMEM((gtok, per_tok // 8, 8, D), jnp.uint32),
                        pltpu.VMEM((gtok, D), F32),
                        pltpu.SemaphoreType.DMA((2, gtok))],
        compiler_params=_cparams(("arbitrary",)),
        name="experts",
    )(*args, lng, lnb)


def _expert_dense_kernel(rows_ref, h_ref, gate_ref, x1_ref, g2_ref, lng_ref, lnb_ref, o_ref, f_ref, *, gtok, alpha):
    per_tok = PEER_HEADS * PEER_TOPK
    gate_t = gate_ref[...].T
    for j in range(gtok):
        f_ref[j:j + 1, :] = _expert_token(rows_ref[j * per_tok:(j + 1) * per_tok, :], h_ref[j:j + 1, :],
                                          gate_t[:, j:j + 1])
    o_ref[...] = _ln(alpha * x1_ref[...] + g2_ref[0] * f_ref[...]) * lng_ref[...] + lnb_ref[...]


def _experts_dense(rows, gate, h2, x1, g2, lng, lnb, *, tok0, n_tok, gtok, seq, alpha):
    D = h2.shape[1]
    per_tok = PEER_HEADS * PEER_TOPK
    off = tok0 // gtok
    steps_per_batch = seq // gtok
    return pl.pallas_call(
        functools.partial(_expert_dense_kernel, gtok=gtok, alpha=alpha),
        grid=(n_tok // gtok,),
        in_specs=[pl.BlockSpec((gtok * per_tok, D), lambda i: (i, 0)),
                  pl.BlockSpec((gtok, D), lambda i: (i + off, 0)),
                  pl.BlockSpec((gtok, per_tok), lambda i: (i + off, 0)),
                  pl.BlockSpec((gtok, D), lambda i: (i + off, 0)),
                  pl.BlockSpec((1, 1, D), lambda i: ((i + off) // steps_per_batch, 0, 0)),
                  pl.BlockSpec((1, D), lambda i: (0, 0)),
                  pl.BlockSpec((1, D), lambda i: (0, 0))],
        out_specs=pl.BlockSpec((gtok, D), lambda i: (i, 0)),
        out_shape=jax.ShapeDtypeStruct((n_tok, D), F32),
        scratch_shapes=[pltpu.VMEM((gtok, D), F32)],
        compiler_params=_cparams(("arbitrary",)),
        name="experts_dense",
    )(rows, h2, gate, x1, g2, lng, lnb)


SC_WORKERS = 32
SC_IDX_BLOCK = 128
SC_CHUNK = 32
PEER_PHASES = 4


def _sc_gather_rows(tab, idx2d, row0, n_rows):
    from jax.experimental.pallas import tpu_sc as plsc
    D = tab.shape[1]
    steps = n_rows // SC_WORKERS
    n_chunks = SC_IDX_BLOCK // SC_CHUNK
    mesh = plsc.VectorSubcoreMesh(core_axis_name="core", subcore_axis_name="subcore")

    @pl.kernel(out_type=jax.ShapeDtypeStruct((n_rows * SC_IDX_BLOCK, D), tab.dtype), mesh=mesh,
               scratch_types=[pltpu.VMEM((1, SC_IDX_BLOCK), jnp.int32),
                              pltpu.VMEM((2, SC_CHUNK, D), tab.dtype),
                              pltpu.SemaphoreType.DMA((2,))])
    def k(x_hbm, i_hbm, o_hbm, i_vmem, buf, wsem):
        wid = lax.axis_index("core") * (SC_WORKERS // 2) + lax.axis_index("subcore")

        def write(row, c):
            return pltpu.make_async_copy(buf.at[c % 2], o_hbm.at[pl.ds(row * SC_IDX_BLOCK + c * SC_CHUNK, SC_CHUNK)],
                                         wsem.at[c % 2])

        @pl.loop(0, steps)
        def _(st):
            row = wid * steps + st
            pltpu.sync_copy(i_hbm.at[pl.ds(row0 + row, 1)], i_vmem)
            for c in range(n_chunks):
                if c >= 2:
                    write(row, c - 2).wait()
                pltpu.sync_copy(x_hbm.at[i_vmem.at[0, pl.ds(c * SC_CHUNK, SC_CHUNK)]], buf.at[c % 2])
                write(row, c).start()
            for c in range(n_chunks - 2, n_chunks):
                write(row, c).wait()
    return k(tab, idx2d)


def _rope_tables(L):
    half = HEAD_DIM // 4
    t = np.arange(L)
    freqs = ROPE_THETA ** (-np.arange(half, dtype=np.float32) / half)
    lane = np.arange(HEAD_DIM)
    pos = np.where(lane[None, :] < HEAD_DIM // 2, (t // GRID_W)[:, None], (t % GRID_W)[:, None]).astype(np.float32)
    ang = pos * freqs[lane % half][None, :].astype(np.float32)
    cos, sin = np.cos(ang), np.sin(ang)
    first = ((lane % (2 * half)) < half)[None, :]
    slo = np.where(first, -sin, 0.0)
    shi = np.where(first, 0.0, sin)
    tile = lambda a: jnp.asarray(np.tile(a, (1, N_KV_HEADS)), F32)
    return tile(cos), tile(slo), tile(shi)


def _pack_tables(u, v):
    ub = lax.bitcast_convert_type(u.astype(BF16), jnp.uint16).astype(jnp.uint32)
    vb = lax.bitcast_convert_type(v.astype(BF16), jnp.uint16).astype(jnp.uint32)
    return (ub << 16) | vb


def _pick(n, prefs):
    for p in prefs:
        if n % p == 0:
            return p
    return n


def kernel(x, c, ctx, c_ctx, w_mod, b_mod, w_in, q_norm_g, k_norm_g, gmlp_ln_g, gmlp_ln_b, gmlp_w_s, gmlp_b_s,
           conv_w_dw, conv_b_dw, conv_ln_g, conv_ln_b, conv_w_pw, conv_b_pw, w_out, ln1_g, ln1_b, ln2_g, ln2_b,
           peer_w_q, peer_sub_keys, peer_u, peer_v):
    B, L, D = x.shape
    LC = ctx.shape[1]
    depth = w_mod.shape[0]
    alpha = (2 * depth) ** 0.25
    gw = gmlp_ln_g.shape[-1]

    cos, slo, shi = _rope_tables(L)
    ones_c = jnp.ones((LC, KV_WIDTH), F32)
    zeros_c = jnp.zeros((LC, KV_WIDTH), F32)
    gidx = np.arange(ATTN_WIDTH) // HEAD_DIM
    gmat = jnp.asarray((gidx[:, None] == gidx[None, :]) / HEAD_DIM, BF16)

    cond = jnp.concatenate([c, c_ctx[None, :], jnp.zeros((-(B + 1) % 8, D), F32)], axis=0)

    t_in = _pick(L, (512, 256, 128))
    t_mix = _pick(L, (256, 128))
    t_inc = _pick(LC, (512, 256, 128))
    t_mixc = _pick(LC, (256, 128))
    tk_lat = _pick(L + LC, (768, 512, 384, 256, 128))
    tk_ctx = _pick(LC, (512, 256, 128))
    gtok = 16

    def peer(h2, x1, g2, l, seq, wq_bf, keys_bf, tab):
        n = h2.shape[0] * h2.shape[1]
        gate, eidx = _retrieval(h2.reshape(n, D), wq_bf, keys_bf, tile=_pick(n, (512, 256, 128)))
        h2f, x1f = h2.reshape(n, D), x1.reshape(n, D)
        lng, lnb = ln2_g[l].reshape(1, D), ln2_b[l].reshape(1, D)
        eidx3 = eidx.reshape(n // gtok, 1, -1)
        common = dict(gtok=gtok, seq=seq, alpha=alpha)
        if n < 8192:
            out = _experts(eidx3, gate, tab, h2f, x1f, g2, lng, lnb, tok0=0, n_tok=n, **common)
            return out.reshape(h2.shape)
        half = n // 2
        cn = half // PEER_PHASES
        tab2 = tab.reshape(-1, D)
        outs_tc, outs_sc, rows_prev = [], [], None
        for p in range(PEER_PHASES):
            rows_p = _sc_gather_rows(tab2, eidx, half + p * cn, cn)
            res = _experts(eidx3, gate, tab, h2f, x1f, g2, lng, lnb, tok0=p * cn, n_tok=cn, dense_rows=rows_prev,
                           dense_tok0=half + (p - 1) * cn, **common)
            if rows_prev is None:
                outs_tc.append(res)
            else:
                outs_tc.append(res[0])
                outs_sc.append(res[1])
            rows_prev = rows_p
        outs_sc.append(_experts_dense(rows_prev, gate, h2f, x1f, g2, lng, lnb, tok0=n - cn, n_tok=cn, **common))
        out = jnp.concatenate(outs_tc + outs_sc, axis=0)
        return out.reshape(h2.shape)

    x_lat, x_ctx = x, ctx
    for l in range(depth):
        last = l == depth - 1
        mod = _modulation(cond, w_mod[l], b_mod[l])
        m_lat = [mod[:B, k * D:(k + 1) * D].reshape(B, 1, D) for k in range(6)]
        m_ctx = [jnp.broadcast_to(mod[B:B + 1, k * D:(k + 1) * D].reshape(1, 1, D), (B, 1, D)) for k in range(6)]

        w_in_bf = w_in[l].astype(BF16)
        gq = jnp.tile(q_norm_g[l], N_Q_HEADS).reshape(1, ATTN_WIDTH)
        gk = jnp.tile(k_norm_g[l], N_KV_HEADS).reshape(1, KV_WIDTH)
        mix_prm = (gmlp_ln_g[l].reshape(1, gw), gmlp_ln_b[l].reshape(1, gw),
                   gmlp_w_s[l].astype(BF16),
                   jnp.repeat(gmlp_b_s[l].T, gw // GMLP_GROUPS, axis=1),
                   conv_w_dw[l], conv_b_dw[l].reshape(1, -1), conv_ln_g[l].reshape(1, -1),
                   conv_ln_b[l].reshape(1, -1), conv_w_pw[l].astype(BF16), conv_b_pw[l].reshape(1, -1),
                   w_out[l].astype(BF16), ln1_g[l].reshape(1, D), ln1_b[l].reshape(1, D))
        wq_bf = peer_w_q[l].astype(BF16)
        keys_bf = peer_sub_keys[l].reshape(2 * PEER_HEADS, N_KEYS, PEER_HALF).astype(BF16)
        tab = _pack_tables(peer_u[l], peer_v[l]).reshape(-1, 1, D)

        q_l, kt_l, v_l, zg_l, zc_l = _inproj(x_lat, m_lat[0], m_lat[1], w_in_bf, gq, gk, cos, slo, shi, gmat,
                                              tile=t_in)
        q_c, kt_c, v_c, zg_c, zc_c = _inproj(x_ctx, m_ctx[0], m_ctx[1], w_in_bf, gq, gk, ones_c, zeros_c, zeros_c,
                                              gmat, tile=t_inc)
        kt_all = jnp.concatenate([kt_c, kt_l], axis=3)
        v_all = jnp.concatenate([v_c, v_l], axis=1)
        o_lat = _attention(q_l, kt_all, v_all, tq=128, tk=tk_lat)
        x1, h2 = _mixers(x_lat, o_lat, zg_l, zc_l, m_lat[2], m_lat[3], m_lat[4], mix_prm, tile=t_mix, alpha=alpha)
        x_lat_new = peer(h2, x1, m_lat[5], l, L, wq_bf, keys_bf, tab)

        if not last:
            o_ctx = _attention(q_c, kt_c, v_c, tq=128, tk=tk_ctx)
            x1c, h2c = _mixers(x_ctx, o_ctx, zg_c, zc_c, m_ctx[2], m_ctx[3], m_ctx[4], mix_prm, tile=t_mixc,
                               alpha=alpha)
            x_ctx = peer(h2c, x1c, m_ctx[5], l, LC, wq_bf, keys_bf, tab)
        x_lat = x_lat_new
    return x_lat
```

```python
import functools
import math

import jax
import jax.numpy as jnp
import numpy as np
from jax import lax
from jax.experimental import pallas as pl
from jax.experimental.pallas import tpu as pltpu

F32 = jnp.float32
BF16 = jnp.bfloat16

EPS = 1e-6
GRID_W = 64
N_Q_HEADS = 8
N_KV_HEADS = 2
Q_PER_KV = N_Q_HEADS // N_KV_HEADS
HEAD_DIM = 64
ATTN_WIDTH = N_Q_HEADS * HEAD_DIM
KV_WIDTH = N_KV_HEADS * HEAD_DIM
QKV_WIDTH = ATTN_WIDTH + 2 * KV_WIDTH
ROPE_THETA = 10000.0
CHUNK = 128
GMLP_GROUPS = 4
CONV_K = 31
CONV_HALO = 16
PEER_HEADS = 8
N_KEYS = 128
PEER_TOPK = 16
PEER_HALF = 128

VMEM_LIMIT = 56 * 1024 * 1024


def _cparams(sem):
    return pltpu.CompilerParams(dimension_semantics=sem, vmem_limit_bytes=VMEM_LIMIT)


def _gelu_tanh(x):
    return 0.5 * x * (1.0 + jnp.tanh(math.sqrt(2.0 / math.pi) * (x + 0.044715 * (x * x * x))))


def _ln(x):
    mu = jnp.mean(x, axis=-1, keepdims=True)
    xc = x - mu
    var = jnp.mean(xc * xc, axis=-1, keepdims=True)
    return xc * lax.rsqrt(var + EPS)


def _mod_kernel(c_ref, w_ref, b_ref, o_ref):
    c = c_ref[...]
    s = c * jax.nn.sigmoid(c)
    o_ref[...] = jnp.dot(s, w_ref[...], preferred_element_type=F32,
                         precision=lax.Precision.HIGHEST) + b_ref[...]


def _modulation(cond, w_mod, b_mod):
    R, D = cond.shape
    N = w_mod.shape[1]
    tn = 1024
    return pl.pallas_call(
        _mod_kernel,
        grid=(N // tn,),
        in_specs=[pl.BlockSpec((R, D), lambda j: (0, 0)),
                  pl.BlockSpec((D, tn), lambda j: (0, j)),
                  pl.BlockSpec((1, tn), lambda j: (0, j))],
        out_specs=pl.BlockSpec((R, tn), lambda j: (0, j)),
        out_shape=jax.ShapeDtypeStruct((R, N), F32),
        compiler_params=_cparams(("arbitrary",)),
        name="modulation",
    )(cond, w_mod, b_mod.reshape(1, N))


def _group_mean_sq(t, gmat):
    sq = t * t
    hi = sq.astype(BF16)
    lo = (sq - hi.astype(F32)).astype(BF16)
    return (jnp.dot(hi, gmat, preferred_element_type=F32)
            + jnp.dot(lo, gmat, preferred_element_type=F32))


def _rope(t, cos, sin_lo, sin_hi):
    w = t.shape[-1]
    up = pltpu.roll(t, w - 16, axis=1)
    dn = pltpu.roll(t, 16, axis=1)
    return t * cos + up * sin_lo + dn * sin_hi


def _inproj_kernel(x_ref, sh_ref, sc_ref, w_ref, gq_ref, gk_ref, cos_ref, slo_ref, shi_ref, gmat_ref,
                   q_ref, kt_ref, v_ref, zg_ref, zc_ref):
    h = _ln(x_ref[0]) * (1.0 + sc_ref[0]) + sh_ref[0]
    p = jnp.dot(h.astype(BF16), w_ref[...], preferred_element_type=F32)
    q = p[:, :ATTN_WIDTH]
    k = p[:, ATTN_WIDTH:ATTN_WIDTH + KV_WIDTH]
    v = p[:, ATTN_WIDTH + KV_WIDTH:QKV_WIDTH]
    gm_w = (p.shape[1] - QKV_WIDTH) // 2
    zg_ref[0] = p[:, QKV_WIDTH:QKV_WIDTH + gm_w]
    zc_ref[0] = p[:, QKV_WIDTH + gm_w:]
    v_ref[0] = v.astype(BF16)

    gmat = gmat_ref[...]
    cos, slo, shi = cos_ref[...], slo_ref[...], shi_ref[...]
    qn = q * lax.rsqrt(_group_mean_sq(q, gmat) + EPS) * gq_ref[...]
    rep = ATTN_WIDTH // KV_WIDTH
    qr = _rope(qn, jnp.concatenate([cos] * rep, axis=1), jnp.concatenate([slo] * rep, axis=1),
               jnp.concatenate([shi] * rep, axis=1))
    qr = (qr * (HEAD_DIM ** -0.5 * math.log2(math.e))).astype(BF16)
    for hh in range(N_Q_HEADS):
        q_ref[0, hh] = qr[:, hh * HEAD_DIM:(hh + 1) * HEAD_DIM]

    kn = k * lax.rsqrt(_group_mean_sq(k, gmat[:KV_WIDTH, :KV_WIDTH]) + EPS) * gk_ref[...]
    kt = _rope(kn, cos, slo, shi).T
    for g in range(N_KV_HEADS):
        kt_ref[0, g] = kt[g * HEAD_DIM:(g + 1) * HEAD_DIM, :].astype(BF16)


def _inproj(x, sh, sc, w_in_bf, gq, gk, cos, slo, shi, gmat, *, tile):
    B, L, D = x.shape
    W = w_in_bf.shape[1]
    gm_w = (W - QKV_WIDTH) // 2
    nt = L // tile
    bmap = lambda b, i: (b, 0, 0)
    cmap = lambda b, i: (0, 0)
    return pl.pallas_call(
        _inproj_kernel,
        grid=(B, nt),
        in_specs=[pl.BlockSpec((1, tile, D), lambda b, i: (b, i, 0)),
                  pl.BlockSpec((1, 1, D), bmap), pl.BlockSpec((1, 1, D), bmap),
                  pl.BlockSpec((D, W), cmap),
                  pl.BlockSpec((1, ATTN_WIDTH), cmap), pl.BlockSpec((1, KV_WIDTH), cmap),
                  pl.BlockSpec((tile, KV_WIDTH), lambda b, i: (i, 0)),
                  pl.BlockSpec((tile, KV_WIDTH), lambda b, i: (i, 0)),
                  pl.BlockSpec((tile, KV_WIDTH), lambda b, i: (i, 0)),
                  pl.BlockSpec((ATTN_WIDTH, ATTN_WIDTH), cmap)],
        out_specs=[pl.BlockSpec((1, N_Q_HEADS, tile, HEAD_DIM), lambda b, i: (b, 0, i, 0)),
                   pl.BlockSpec((1, N_KV_HEADS, HEAD_DIM, tile), lambda b, i: (b, 0, 0, i)),
                   pl.BlockSpec((1, tile, KV_WIDTH), lambda b, i: (b, i, 0)),
                   pl.BlockSpec((1, tile, gm_w), lambda b, i: (b, i, 0)),
                   pl.BlockSpec((1, tile, gm_w), lambda b, i: (b, i, 0))],
        out_shape=[jax.ShapeDtypeStruct((B, N_Q_HEADS, L, HEAD_DIM), BF16),
                   jax.ShapeDtypeStruct((B, N_KV_HEADS, HEAD_DIM, L), BF16),
                   jax.ShapeDtypeStruct((B, L, KV_WIDTH), BF16),
                   jax.ShapeDtypeStruct((B, L, gm_w), F32),
                   jax.ShapeDtypeStruct((B, L, gm_w), F32)],
        compiler_params=_cparams(("arbitrary", "arbitrary")),
        name="inproj",
    )(x, sh, sc, w_in_bf, gq, gk, cos, slo, shi, gmat)


def _attn_kernel(q_ref, kt_ref, vx_ref, o_ref, p_ref, *, tq, tk, nk):
    m_rows = Q_PER_KV * tq
    qs = [q_ref[0, Q_PER_KV * g:Q_PER_KV * (g + 1)].reshape(m_rows, HEAD_DIM) for g in range(N_KV_HEADS)]

    def pv(c, par, g):
        off = pl.multiple_of(c * tk, tk)
        return jnp.dot(p_ref[par, g], vx_ref[0, g, pl.ds(off, tk), :], preferred_element_type=F32)

    def step(c, par, carry):
        off = pl.multiple_of(c * tk, tk)
        out = []
        for g in range(N_KV_HEADS):
            m, acc = carry[g]
            s = jnp.dot(qs[g], kt_ref[0, g, :, pl.ds(off, tk)], preferred_element_type=F32)
            m_new = jnp.maximum(m, jnp.max(s, axis=-1, keepdims=True))
            p_ref[1 - par, g] = jnp.exp2(s - m_new).astype(BF16)
            if acc is not None:
                acc = jnp.exp2(m - m_new) * (acc + pv(c - 1, par, g))
            out.append((m_new, acc))
        return tuple(out)

    first = step(0, 1, tuple((jnp.full((m_rows, 1), -1e30, F32), None) for _ in range(N_KV_HEADS)))
    carry = tuple((m, jnp.zeros((m_rows, 2 * HEAD_DIM), F32)) for m, _ in first)

    n_pairs = (nk - 1) // 2
    def pair(t, carry):
        carry = step(2 * t + 1, 0, carry)
        return step(2 * t + 2, 1, carry)
    carry = lax.fori_loop(0, n_pairs, pair, carry)
    par = 0
    if (nk - 1) % 2:
        carry = step(nk - 1, 0, carry)
        par = 1

    for g in range(N_KV_HEADS):
        acc = carry[g][1] + pv(nk - 1, par, g)
        og = acc / pltpu.roll(acc, HEAD_DIM, axis=1)
        for r in range(Q_PER_KV):
            hh = Q_PER_KV * g + r
            o_ref[0, :, hh * HEAD_DIM:(hh + 1) * HEAD_DIM] = og[r * tq:(r + 1) * tq, :HEAD_DIM].astype(o_ref.dtype)


def _attention(q, kt, v, *, tq, tk):
    B, _, L, _ = q.shape
    LK = kt.shape[-1]
    ones = jnp.ones((B, LK, HEAD_DIM), v.dtype)
    vx = jnp.stack([jnp.concatenate([v[..., g * HEAD_DIM:(g + 1) * HEAD_DIM], ones], axis=-1)
                    for g in range(N_KV_HEADS)], axis=1)
    return pl.pallas_call(
        functools.partial(_attn_kernel, tq=tq, tk=tk, nk=LK // tk),
        grid=(B, L // tq),
        in_specs=[pl.BlockSpec((1, N_Q_HEADS, tq, HEAD_DIM), lambda b, i: (b, 0, i, 0)),
                  pl.BlockSpec((1, N_KV_HEADS, HEAD_DIM, LK), lambda b, i: (b, 0, 0, 0)),
                  pl.BlockSpec((1, N_KV_HEADS, LK, 2 * HEAD_DIM), lambda b, i: (b, 0, 0, 0))],
        out_specs=pl.BlockSpec((1, tq, ATTN_WIDTH), lambda b, i: (b, i, 0)),
        out_shape=jax.ShapeDtypeStruct((B, L, ATTN_WIDTH), BF16),
        scratch_shapes=[pltpu.VMEM((2, N_KV_HEADS, Q_PER_KV * tq, tk), BF16)],
        compiler_params=_cparams(("arbitrary", "arbitrary")),
        name="attention",
    )(q, kt, vx)


def _mix_kernel(x_ref, o_ref, zg_ref, zc_ref, zp_ref, zn_ref, g1_ref, sh2_ref, sc2_ref,
                gln_g_ref, gln_b_ref, ws_ref, bs_ref, wdw_ref, bdw_ref, cln_g_ref, cln_b_ref,
                wpw_ref, bpw_ref, wout_ref, ln1g_ref, ln1b_ref,
                x1_ref, h2_ref, ybuf_ref, *, tile, alpha):
    i = pl.program_id(1)
    nt = pl.num_programs(1)
    gw = zg_ref.shape[-1] // 2
    cw = zc_ref.shape[-1] // 2

    z = _gelu_tanh(zg_ref[0])
    u, v = z[:, :gw], z[:, gw:]
    vn = (_ln(v) * gln_g_ref[...] + gln_b_ref[...]).astype(BF16)
    grp = lax.broadcasted_iota(jnp.int32, (CHUNK, gw), 1) // (gw // GMLP_GROUPS)
    sg = []
    for n in range(tile // CHUNK):
        vc = vn[n * CHUNK:(n + 1) * CHUNK]
        s = jnp.zeros((CHUNK, gw), F32)
        for g in range(GMLP_GROUPS):
            s = jnp.where(grp == g, jnp.dot(ws_ref[g], vc, preferred_element_type=F32), s)
        sg.append(s + bs_ref[...])
    y_g = u * jnp.concatenate(sg, axis=0)

    def glu(zz):
        return zz[:, :cw] * jax.nn.sigmoid(zz[:, cw:])

    ybuf_ref[0:CONV_HALO] = glu(zp_ref[0]) * (i > 0).astype(F32)
    ybuf_ref[CONV_HALO:CONV_HALO + tile] = glu(zc_ref[0])
    ybuf_ref[CONV_HALO + tile:2 * CONV_HALO + tile] = glu(zn_ref[0]) * (i < nt - 1).astype(F32)
    base = CONV_HALO - CONV_K // 2
    conv = jnp.zeros((tile, cw), F32)
    for kk in range(CONV_K):
        conv = conv + ybuf_ref[base + kk:base + kk + tile, :] * wdw_ref[kk:kk + 1, :]
    conv = conv + bdw_ref[...]
    t = _ln(conv) * cln_g_ref[...] + cln_b_ref[...]
    t = t * jax.nn.sigmoid(t)
    y_c = jnp.dot(t.astype(BF16), wpw_ref[...], preferred_element_type=F32) + bpw_ref[...]

    aw = o_ref.shape[-1]
    y = (jnp.dot(o_ref[0], wout_ref[0:aw, :], preferred_element_type=F32)
         + jnp.dot(y_g.astype(BF16), wout_ref[aw:aw + gw, :], preferred_element_type=F32)
         + jnp.dot(y_c.astype(BF16), wout_ref[aw + gw:, :], preferred_element_type=F32))
    x1 = _ln(alpha * x_ref[0] + g1_ref[0] * y) * ln1g_ref[...] + ln1b_ref[...]
    x1_ref[0] = x1
    h2_ref[0] = _ln(x1) * (1.0 + sc2_ref[0]) + sh2_ref[0]


def _mixers(x, o, zg, zc, g1, sh2, sc2, prm, *, tile, alpha):
    B, L, D = x.shape
    nt = L // tile
    hb = tile // CONV_HALO
    nhb = L // CONV_HALO
    gw2, cw2 = zg.shape[-1], zc.shape[-1]
    tmap = lambda b, i: (b, i, 0)
    bmap = lambda b, i: (b, 0, 0)
    c2 = lambda b, i: (0, 0)
    c3 = lambda b, i: (0, 0, 0)
    in_specs = [pl.BlockSpec((1, tile, D), tmap),
                pl.BlockSpec((1, tile, o.shape[-1]), tmap),
                pl.BlockSpec((1, tile, gw2), tmap),
                pl.BlockSpec((1, tile, cw2), tmap),
                pl.BlockSpec((1, CONV_HALO, cw2), lambda b, i: (b, jnp.maximum(i * hb - 1, 0), 0)),
                pl.BlockSpec((1, CONV_HALO, cw2), lambda b, i: (b, jnp.minimum((i + 1) * hb, nhb - 1), 0)),
                pl.BlockSpec((1, 1, D), bmap), pl.BlockSpec((1, 1, D), bmap), pl.BlockSpec((1, 1, D), bmap)]
    for a in prm:
        in_specs.append(pl.BlockSpec(a.shape, c2 if a.ndim == 2 else c3))
    return pl.pallas_call(
        functools.partial(_mix_kernel, tile=tile, alpha=alpha),
        grid=(B, nt),
        in_specs=in_specs,
        out_specs=[pl.BlockSpec((1, tile, D), tmap), pl.BlockSpec((1, tile, D), tmap)],
        out_shape=[jax.ShapeDtypeStruct((B, L, D), F32), jax.ShapeDtypeStruct((B, L, D), F32)],
        scratch_shapes=[pltpu.VMEM((tile + 2 * CONV_HALO, cw2 // 2), F32)],
        compiler_params=_cparams(("arbitrary", "arbitrary")),
        name="mixers",
    )(x, o, zg, zc, zc, zc, g1, sh2, sc2, *prm)


def _oddeven_merge_sort_pairs(n):
    pairs = []
    p = 1
    while p < n:
        k = p
        while k >= 1:
            for j in range(k % p, n - k, 2 * k):
                for i in range(min(k, n - j - k)):
                    if (i + j) // (2 * p) == (i + j + k) // (2 * p):
                        pairs.append((i + j, i + j + k))
            k //= 2
        p *= 2
    return pairs


_SORT16 = _oddeven_merge_sort_pairs(16)
_BITONIC16 = [(i, i + d) for d in (8, 4, 2, 1) for i in range(16) if (i & d) == 0]


def _before(a, ia, b, ib):
    return (a > b) | ((a == b) & (ia < ib))


def _cmpx(vals, keys, pay, i, j):
    keep = _before(vals[i], keys[i], vals[j], keys[j])
    vals[i], vals[j] = jnp.where(keep, vals[i], vals[j]), jnp.where(keep, vals[j], vals[i])
    keys[i], keys[j] = jnp.where(keep, keys[i], keys[j]), jnp.where(keep, keys[j], keys[i])
    if pay is not None:
        pay[i], pay[j] = jnp.where(keep, pay[i], pay[j]), jnp.where(keep, pay[j], pay[i])


def _top16_merge(av, ak, ap, bv, bk, bp):
    n = len(av)
    ov, ok, op = [], [], []
    for i in range(n):
        keep = _before(av[i], ak[i], bv[n - 1 - i], bk[n - 1 - i])
        ov.append(jnp.where(keep, av[i], bv[n - 1 - i]))
        ok.append(jnp.where(keep, ak[i], bk[n - 1 - i]))
        if ap is not None:
            op.append(jnp.where(keep, ap[i], bp[n - 1 - i]))
    return ov, ok, (op if ap is not None else None)


def _stage2_candidates():
    return [(a, b) for a in range(PEER_TOPK) for b in range(PEER_TOPK) if (a + 1) * (b + 1) <= PEER_TOPK]


def _retr_kernel(h_ref, wq_ref, keys_ref, gate_ref, eidx_ref, q_s, sv_s, si_s, *, tile):
    q_s[...] = jnp.dot(h_ref[...].astype(BF16), wq_ref[...], preferred_element_type=F32)
    sub = lax.broadcasted_iota(jnp.int32, (8, tile), 0)
    sv_s[...] = jnp.zeros(sv_s.shape, F32)
    si_s[...] = jnp.zeros(si_s.shape, jnp.int32)

    def stage1(hp, _):
        qh = q_s[:, pl.ds(pl.multiple_of(hp * PEER_HALF, PEER_HALF), PEER_HALF)].astype(BF16)
        st = lax.dot_general(keys_ref[hp], qh, (((1,), (1,)), ((), ())),
                             preferred_element_type=F32)
        st = st.reshape(N_KEYS // 8, 8, tile)
        vals = [st[i] for i in range(16)]
        keys = [sub + 8 * i for i in range(16)]
        for (i, j) in _SORT16:
            _cmpx(vals, keys, None, i, j)
        for shift in (4, 2, 1):
            bv = [pltpu.roll(t, shift, axis=0) for t in vals]
            bk = [pltpu.roll(t, shift, axis=0) for t in keys]
            vals, keys, _ = _top16_merge(vals, keys, None, bv, bk, None)
            for (i, j) in _BITONIC16:
                _cmpx(vals, keys, None, i, j)
        head = hp // 2
        half = hp % 2
        for i in range(16):
            sv_s[half, i] = jnp.where(sub == head, vals[i], sv_s[half, i])
            si_s[half, i] = jnp.where(sub == head, keys[i], si_s[half, i])
        return 0

    lax.fori_loop(0, 2 * PEER_HEADS, stage1, 0)

    cands = _stage2_candidates()
    n_pad = 64
    vals, poss, exps = [], [], []
    for (a, b) in cands:
        vals.append(sv_s[0, a] + sv_s[1, b])
        poss.append(jnp.full((8, tile), a * PEER_TOPK + b, jnp.int32))
        exps.append(si_s[0, a] * N_KEYS + si_s[1, b])
    for t in range(n_pad - len(cands)):
        vals.append(jnp.full((8, tile), -jnp.inf, F32))
        poss.append(jnp.full((8, tile), PEER_TOPK * PEER_TOPK + t, jnp.int32))
        exps.append(jnp.zeros((8, tile), jnp.int32))
    groups = []
    for g in range(n_pad // 16):
        gv, gk, gp = vals[16 * g:16 * g + 16], poss[16 * g:16 * g + 16], exps[16 * g:16 * g + 16]
        for (i, j) in _SORT16:
            _cmpx(gv, gk, gp, i, j)
        groups.append((gv, gk, gp))
    while len(groups) > 1:
        nxt = []
        for g in range(0, len(groups), 2):
            mv, mk, mp = _top16_merge(*groups[g], *groups[g + 1])
            if len(groups) > 2:
                for (i, j) in _BITONIC16:
                    _cmpx(mv, mk, mp, i, j)
            nxt.append((mv, mk, mp))
        groups = nxt
    best, _, experts = groups[0]

    mx = functools.reduce(jnp.maximum, best)
    ex = [jnp.exp(b - mx) for b in best]
    inv = 1.0 / functools.reduce(lambda a, b: a + b, ex)
    gate_ref[...] = jnp.concatenate([e * inv for e in ex], axis=0).T
    eidx_ref[...] = jnp.concatenate(experts, axis=0).T


def _retrieval(h2, wq_bf, keys_bf, *, tile):
    N, D = h2.shape
    QW = wq_bf.shape[1]
    return pl.pallas_call(
        functools.partial(_retr_kernel, tile=tile),
        grid=(N // tile,),
        in_specs=[pl.BlockSpec((tile, D), lambda i: (i, 0)),
                  pl.BlockSpec((D, QW), lambda i: (0, 0)),
                  pl.BlockSpec(keys_bf.shape, lambda i: (0, 0, 0))],
        out_specs=[pl.BlockSpec((tile, PEER_TOPK * PEER_HEADS), lambda i: (i, 0)),
                   pl.BlockSpec((tile, PEER_TOPK * PEER_HEADS), lambda i: (i, 0))],
        out_shape=[jax.ShapeDtypeStruct((N, PEER_TOPK * PEER_HEADS), F32),
                   jax.ShapeDtypeStruct((N, PEER_TOPK * PEER_HEADS), jnp.int32)],
        scratch_shapes=[pltpu.VMEM((tile, QW), F32),
                        pltpu.VMEM((2, PEER_TOPK, PEER_HEADS, tile), F32),
                        pltpu.VMEM((2, PEER_TOPK, PEER_HEADS, tile), jnp.int32)],
        compiler_params=_cparams(("arbitrary",)),
        name="retrieval",
    )(h2, wq_bf, keys_bf)


def _expert_token(r, h_row, gate_col):
    u = pltpu.bitcast(r & jnp.uint32(0xFFFF0000), F32)
    v = pltpu.bitcast(r << 16, F32)
    d = jnp.sum(u * h_row, axis=-1, keepdims=True)
    w = jnp.broadcast_to(gate_col * _gelu_tanh(d), r.shape)
    return jnp.sum(v * w, axis=0, keepdims=True)


def _expert_kernel(*refs, gtok, alpha, dense):
    if dense:
        (idx0_ref, idxn_ref, tab_ref, h_ref, gate_ref, x1_ref, g2_ref, rows_ref, hd_ref, gated_ref, x1d_ref, g2d_ref,
         lng_ref, lnb_ref, o_ref, od_ref, buf_a, buf_b, f_ref, fd_ref, sem_ref) = refs
    else:
        (idx0_ref, idxn_ref, tab_ref, h_ref, gate_ref, x1_ref, g2_ref, lng_ref, lnb_ref,
         o_ref, buf_a, buf_b, f_ref, fd_ref, sem_ref) = refs
    i = pl.program_id(0)
    n_steps = pl.num_programs(0)
    per_tok = PEER_HEADS * PEER_TOPK
    d_model = h_ref.shape[-1]

    def start_token(idx_ref, buf, sem_row, j):
        for g in range(per_tok // 8):
            for t in range(8):
                e = idx_ref[0, 0, j * per_tok + g * 8 + t]
                pltpu.make_async_copy(tab_ref.at[e], buf.at[j, g, pl.ds(t, 1), :],
                                      sem_ref.at[sem_row, j]).start(priority=t % 2)

    def wait_token(buf, other, sem_row, j):
        pltpu.make_async_copy(other.at[j], buf.at[j], sem_ref.at[sem_row, j]).wait()

    def token(j, cur, nxt, cur_row):
        wait_token(cur, nxt, cur_row, j)
        start_token(idxn_ref, nxt, 1 - cur_row, j)
        f_ref[j:j + 1, :] = _expert_token(cur[j].reshape(per_tok, d_model), h_ref[j:j + 1, :],
                                          gate_t[:, j:j + 1])
        if dense:
            fd_ref[j:j + 1, :] = _expert_token(rows_ref[j * per_tok:(j + 1) * per_tok, :], hd_ref[j:j + 1, :],
                                               gated_t[:, j:j + 1])

    @pl.when(i == 0)
    def _():
        def prime(j, _):
            start_token(idx0_ref, buf_a, 0, j)
            return 0
        lax.fori_loop(0, gtok, prime, 0)

    gate_t = gate_ref[...].T
    gated_t = gated_ref[...].T if dense else None

    @pl.when(i % 2 == 0)
    def _():
        for j in range(gtok):
            token(j, buf_a, buf_b, 0)

    @pl.when(i % 2 == 1)
    def _():
        for j in range(gtok):
            token(j, buf_b, buf_a, 1)

    @pl.when((i == n_steps - 1) & (i % 2 == 0))
    def _():
        lax.fori_loop(0, gtok, lambda j, c: (wait_token(buf_b, buf_a, 1, j), 0)[1], 0)

    @pl.when((i == n_steps - 1) & (i % 2 == 1))
    def _():
        lax.fori_loop(0, gtok, lambda j, c: (wait_token(buf_a, buf_b, 0, j), 0)[1], 0)

    o_ref[...] = _ln(alpha * x1_ref[...] + g2_ref[0] * f_ref[...]) * lng_ref[...] + lnb_ref[...]

    if dense:
        od_ref[...] = _ln(alpha * x1d_ref[...] + g2d_ref[0] * fd_ref[...]) * lng_ref[...] + lnb_ref[...]


def _experts(eidx, gate, tab, h2, x1, g2, lng, lnb, *, tok0, n_tok, gtok, seq, alpha, dense_rows=None,
             dense_tok0=0):
    D = h2.shape[1]
    S = n_tok // gtok
    off = tok0 // gtok
    doff = dense_tok0 // gtok
    per_tok = PEER_HEADS * PEER_TOPK
    rows = gtok * per_tok
    spb = seq // gtok
    smem = pltpu.MemorySpace.SMEM
    dense = dense_rows is not None
    tok_spec = lambda o: pl.BlockSpec((gtok, D), lambda i: (i + o, 0))
    gate_spec = lambda o: pl.BlockSpec((gtok, per_tok), lambda i: (i + o, 0))
    g2_spec = lambda o: pl.BlockSpec((1, 1, D), lambda i: ((i + o) // spb, 0, 0))
    row_spec = pl.BlockSpec((1, D), lambda i: (0, 0))
    in_specs = [pl.BlockSpec((1, 1, rows), lambda i: (off, 0, 0), memory_space=smem),
                pl.BlockSpec((1, 1, rows), lambda i: (jnp.minimum(i + 1, S - 1) + off, 0, 0), memory_space=smem),
                pl.BlockSpec(memory_space=pl.ANY),
                tok_spec(off), gate_spec(off), tok_spec(off), g2_spec(off)]
    args = [eidx, eidx, tab, h2, gate, x1, g2]
    out_spec = pl.BlockSpec((gtok, D), lambda i: (i, 0))
    out_shape = jax.ShapeDtypeStruct((n_tok, D), F32)
    if dense:
        in_specs += [pl.BlockSpec((rows, D), lambda i: (i, 0)), tok_spec(doff), gate_spec(doff), tok_spec(doff),
                     g2_spec(doff)]
        args += [dense_rows, h2, gate, x1, g2]
    return pl.pallas_call(
        functools.partial(_expert_kernel, gtok=gtok, alpha=alpha, dense=dense),
        grid=(S,),
        in_specs=in_specs + [row_spec, row_spec],
        out_specs=[out_spec, out_spec] if dense else out_spec,
        out_shape=[out_shape, out_shape] if dense else out_shape,
        scratch_shapes=[pltpu.VMEM((gtok, per_tok // 8, 8, D), jnp.uint32),
                        pltpu.VMEM((gtok, per_tok // 8, 8, D), jnp.uint32),
                        pltpu.VMEM((gtok, D), F32),
                        pltpu.VMEM((gtok, D), F32),
                        pltpu.SemaphoreType.DMA((2, gtok))],
        compiler_params=_cparams(("arbitrary",)),
        name="experts",
    )(*args, lng, lnb)


def _expert_dense_kernel(rows_ref, h_ref, gate_ref, x1_ref, g2_ref, lng_ref, lnb_ref, o_ref, f_ref, *, gtok, alpha):
    per_tok = PEER_HEADS * PEER_TOPK
    gate_t = gate_ref[...].T
    for j in range(gtok):
        f_ref[j:j + 1, :] = _expert_token(rows_ref[j * per_tok:(j + 1) * per_tok, :], h_ref[j:j + 1, :],
                                          gate_t[:, j:j + 1])
    o_ref[...] = _ln(alpha * x1_ref[...] + g2_ref[0] * f_ref[...]) * lng_ref[...] + lnb_ref[...]


def _experts_dense(rows, gate, h2, x1, g2, lng, lnb, *, tok0, n_tok, gtok, seq, alpha):
    D = h2.shape[1]
    per_tok = PEER_HEADS * PEER_TOPK
    off = tok0 // gtok
    steps_per_batch = seq // gtok
    return pl.pallas_call(
        functools.partial(_expert_dense_kernel, gtok=gtok, alpha=alpha),
        grid=(n_tok // gtok,),
        in_specs=[pl.BlockSpec((gtok * per_tok, D), lambda i: (i, 0)),
                  pl.BlockSpec((gtok, D), lambda i: (i + off, 0)),
                  pl.BlockSpec((gtok, per_tok), lambda i: (i + off, 0)),
                  pl.BlockSpec((gtok, D), lambda i: (i + off, 0)),
                  pl.BlockSpec((1, 1, D), lambda i: ((i + off) // steps_per_batch, 0, 0)),
                  pl.BlockSpec((1, D), lambda i: (0, 0)),
                  pl.BlockSpec((1, D), lambda i: (0, 0))],
        out_specs=pl.BlockSpec((gtok, D), lambda i: (i, 0)),
        out_shape=jax.ShapeDtypeStruct((n_tok, D), F32),
        scratch_shapes=[pltpu.VMEM((gtok, D), F32)],
        compiler_params=_cparams(("arbitrary",)),
        name="experts_dense",
    )(rows, h2, gate, x1, g2, lng, lnb)


SC_WORKERS = 32
SC_IDX_BLOCK = 128
SC_CHUNK = 32
PEER_PHASES = 4


def _sc_gather_rows(tab, idx2d, row0, n_rows):
    from jax.experimental.pallas import tpu_sc as plsc
    D = tab.shape[1]
    steps = n_rows // SC_WORKERS
    n_chunks = SC_IDX_BLOCK // SC_CHUNK
    mesh = plsc.VectorSubcoreMesh(core_axis_name="core", subcore_axis_name="subcore")

    @pl.kernel(out_type=jax.ShapeDtypeStruct((n_rows * SC_IDX_BLOCK, D), tab.dtype), mesh=mesh,
               scratch_types=[pltpu.VMEM((1, SC_IDX_BLOCK), jnp.int32),
                              pltpu.VMEM((2, SC_CHUNK, D), tab.dtype),
                              pltpu.SemaphoreType.DMA((2,))])
    def k(x_hbm, i_hbm, o_hbm, i_vmem, buf, wsem):
        wid = lax.axis_index("core") * (SC_WORKERS // 2) + lax.axis_index("subcore")

        def write(row, c):
            return pltpu.make_async_copy(buf.at[c % 2], o_hbm.at[pl.ds(row * SC_IDX_BLOCK + c * SC_CHUNK, SC_CHUNK)],
                                         wsem.at[c % 2])

        @pl.loop(0, steps)
        def _(st):
            row = wid * steps + st
            pltpu.sync_copy(i_hbm.at[pl.ds(row0 + row, 1)], i_vmem)
            for c in range(n_chunks):
                if c >= 2:
                    write(row, c - 2).wait()
                pltpu.sync_copy(x_hbm.at[i_vmem.at[0, pl.ds(c * SC_CHUNK, SC_CHUNK)]], buf.at[c % 2])
                write(row, c).start()
            for c in range(n_chunks - 2, n_chunks):
                write(row, c).wait()
    return k(tab, idx2d)


def _rope_tables(L):
    half = HEAD_DIM // 4
    t = np.arange(L)
    freqs = ROPE_THETA ** (-np.arange(half, dtype=np.float32) / half)
    lane = np.arange(HEAD_DIM)
    pos = np.where(lane[None, :] < HEAD_DIM // 2, (t // GRID_W)[:, None], (t % GRID_W)[:, None]).astype(np.float32)
    ang = pos * freqs[lane % half][None, :].astype(np.float32)
    cos, sin = np.cos(ang), np.sin(ang)
    first = ((lane % (2 * half)) < half)[None, :]
    slo = np.where(first, -sin, 0.0)
    shi = np.where(first, 0.0, sin)
    tile = lambda a: jnp.asarray(np.tile(a, (1, N_KV_HEADS)), F32)
    return tile(cos), tile(slo), tile(shi)


def _pack_tables(u, v):
    ub = lax.bitcast_convert_type(u.astype(BF16), jnp.uint16).astype(jnp.uint32)
    vb = lax.bitcast_convert_type(v.astype(BF16), jnp.uint16).astype(jnp.uint32)
    return (ub << 16) | vb


def _pick(n, prefs):
    for p in prefs:
        if n % p == 0:
            return p
    return n


def kernel(x, c, ctx, c_ctx, w_mod, b_mod, w_in, q_norm_g, k_norm_g, gmlp_ln_g, gmlp_ln_b, gmlp_w_s, gmlp_b_s,
           conv_w_dw, conv_b_dw, conv_ln_g, conv_ln_b, conv_w_pw, conv_b_pw, w_out, ln1_g, ln1_b, ln2_g, ln2_b,
           peer_w_q, peer_sub_keys, peer_u, peer_v):
    B, L, D = x.shape
    LC = ctx.shape[1]
    depth = w_mod.shape[0]
    alpha = (2 * depth) ** 0.25
    gw = gmlp_ln_g.shape[-1]

    cos, slo, shi = _rope_tables(L)
    ones_c = jnp.ones((LC, KV_WIDTH), F32)
    zeros_c = jnp.zeros((LC, KV_WIDTH), F32)
    gidx = np.arange(ATTN_WIDTH) // HEAD_DIM
    gmat = jnp.asarray((gidx[:, None] == gidx[None, :]) / HEAD_DIM, BF16)

    cond = jnp.concatenate([c, c_ctx[None, :], jnp.zeros((-(B + 1) % 8, D), F32)], axis=0)

    t_in = _pick(L, (512, 256, 128))
    t_mix = _pick(L, (256, 128))
    t_inc = _pick(LC, (512, 256, 128))
    t_mixc = _pick(LC, (256, 128))
    tk_lat = _pick(L + LC, (768, 512, 384, 256, 128))
    tk_ctx = _pick(LC, (512, 256, 128))
    gtok = 16

    def peer(h2, x1, g2, l, seq, wq_bf, keys_bf, tab):
        n = h2.shape[0] * h2.shape[1]
        gate, eidx = _retrieval(h2.reshape(n, D), wq_bf, keys_bf, tile=_pick(n, (512, 256, 128)))
        h2f, x1f = h2.reshape(n, D), x1.reshape(n, D)
        lng, lnb = ln2_g[l].reshape(1, D), ln2_b[l].reshape(1, D)
        eidx3 = eidx.reshape(n // gtok, 1, -1)
        common = dict(gtok=gtok, seq=seq, alpha=alpha)
        if n < 8192:
            out = _experts(eidx3, gate, tab, h2f, x1f, g2, lng, lnb, tok0=0, n_tok=n, **common)
            return out.reshape(h2.shape)
        half = n // 2
        cn = half // PEER_PHASES
        tab2 = tab.reshape(-1, D)
        outs_tc, outs_sc, rows_prev = [], [], None
        for p in range(PEER_PHASES):
            rows_p = _sc_gather_rows(tab2, eidx, half + p * cn, cn)
            res = _experts(eidx3, gate, tab, h2f, x1f, g2, lng, lnb, tok0=p * cn, n_tok=cn, dense_rows=rows_prev,
                           dense_tok0=half + (p - 1) * cn, **common)
            if rows_prev is None:
                outs_tc.append(res)
            else:
                outs_tc.append(res[0])
                outs_sc.append(res[1])
            rows_prev = rows_p
        outs_sc.append(_experts_dense(rows_prev, gate, h2f, x1f, g2, lng, lnb, tok0=n - cn, n_tok=cn, **common))
        out = jnp.concatenate(outs_tc + outs_sc, axis=0)
        return out.reshape(h2.shape)

    x_lat, x_ctx = x, ctx
    for l in range(depth):
        last = l == depth - 1
        mod = _modulation(cond, w_mod[l], b_mod[l])
        m_lat = [mod[:B, k * D:(k + 1) * D].reshape(B, 1, D) for k in range(6)]
        m_ctx = [jnp.broadcast_to(mod[B:B + 1, k * D:(k + 1) * D].reshape(1, 1, D), (B, 1, D)) for k in range(6)]

        w_in_bf = w_in[l].astype(BF16)
        gq = jnp.tile(q_norm_g[l], N_Q_HEADS).reshape(1, ATTN_WIDTH)
        gk = jnp.tile(k_norm_g[l], N_KV_HEADS).reshape(1, KV_WIDTH)
        mix_prm = (gmlp_ln_g[l].reshape(1, gw), gmlp_ln_b[l].reshape(1, gw),
                   gmlp_w_s[l].astype(BF16),
                   jnp.repeat(gmlp_b_s[l].T, gw // GMLP_GROUPS, axis=1),
                   conv_w_dw[l], conv_b_dw[l].reshape(1, -1), conv_ln_g[l].reshape(1, -1),
                   conv_ln_b[l].reshape(1, -1), conv_w_pw[l].astype(BF16), conv_b_pw[l].reshape(1, -1),
                   w_out[l].astype(BF16), ln1_g[l].reshape(1, D), ln1_b[l].reshape(1, D))
        wq_bf = peer_w_q[l].astype(BF16)
        keys_bf = peer_sub_keys[l].reshape(2 * PEER_HEADS, N_KEYS, PEER_HALF).astype(BF16)
        tab = _pack_tables(peer_u[l], peer_v[l]).reshape(-1, 1, D)

        q_l, kt_l, v_l, zg_l, zc_l = _inproj(x_lat, m_lat[0], m_lat[1], w_in_bf, gq, gk, cos, slo, shi, gmat,
                                              tile=t_in)
        q_c, kt_c, v_c, zg_c, zc_c = _inproj(x_ctx, m_ctx[0], m_ctx[1], w_in_bf, gq, gk, ones_c, zeros_c, zeros_c,
                                              gmat, tile=t_inc)
        kt_all = jnp.concatenate([kt_c, kt_l], axis=3)
        v_all = jnp.concatenate([v_c, v_l], axis=1)
        o_lat = _attention(q_l, kt_all, v_all, tq=128, tk=tk_lat)
        x1, h2 = _mixers(x_lat, o_lat, zg_l, zc_l, m_lat[2], m_lat[3], m_lat[4], mix_prm, tile=t_mix, alpha=alpha)
        x_lat_new = peer(h2, x1, m_lat[5], l, L, wq_bf, keys_bf, tab)

        if not last:
            o_ctx = _attention(q_c, kt_c, v_c, tq=128, tk=tk_ctx)
            x1c, h2c = _mixers(x_ctx, o_ctx, zg_c, zc_c, m_ctx[2], m_ctx[3], m_ctx[4], mix_prm, tile=t_mixc,
                               alpha=alpha)
            x_ctx = peer(h2c, x1c, m_ctx[5], l, LC, wq_bf, keys_bf, tab)
        x_lat = x_lat_new
    return x_lat
```

```python
import functools
import math

import jax
import jax.numpy as jnp
import numpy as np
from jax import lax
from jax.experimental import pallas as pl
from jax.experimental.pallas import tpu as pltpu

F32 = jnp.float32
BF16 = jnp.bfloat16

EPS = 1e-6
GRID_W = 64
N_Q_HEADS = 8
N_KV_HEADS = 2
Q_PER_KV = N_Q_HEADS // N_KV_HEADS
HEAD_DIM = 64
ATTN_WIDTH = N_Q_HEADS * HEAD_DIM
KV_WIDTH = N_KV_HEADS * HEAD_DIM
QKV_WIDTH = ATTN_WIDTH + 2 * KV_WIDTH
ROPE_THETA = 10000.0
CHUNK = 128
GMLP_GROUPS = 4
CONV_K = 31
CONV_HALO = 16
PEER_HEADS = 8
N_KEYS = 128
PEER_TOPK = 16
PEER_HALF = 128

VMEM_LIMIT = 56 * 1024 * 1024


def _cparams(sem):
    return pltpu.CompilerParams(dimension_semantics=sem, vmem_limit_bytes=VMEM_LIMIT)


def _gelu_tanh(x):
    return 0.5 * x * (1.0 + jnp.tanh(math.sqrt(2.0 / math.pi) * (x + 0.044715 * (x * x * x))))


def _ln(x):
    mu = jnp.mean(x, axis=-1, keepdims=True)
    xc = x - mu
    var = jnp.mean(xc * xc, axis=-1, keepdims=True)
    return xc * lax.rsqrt(var + EPS)


def _mod_kernel(c_ref, w_ref, b_ref, o_ref):
    c = c_ref[...]
    s = c * jax.nn.sigmoid(c)
    o_ref[...] = jnp.dot(s, w_ref[...], preferred_element_type=F32,
                         precision=lax.Precision.HIGHEST) + b_ref[...]


def _modulation(cond, w_mod, b_mod):
    R, D = cond.shape
    N = w_mod.shape[1]
    tn = 1024
    return pl.pallas_call(
        _mod_kernel,
        grid=(N // tn,),
        in_specs=[pl.BlockSpec((R, D), lambda j: (0, 0)),
                  pl.BlockSpec((D, tn), lambda j: (0, j)),
                  pl.BlockSpec((1, tn), lambda j: (0, j))],
        out_specs=pl.BlockSpec((R, tn), lambda j: (0, j)),
        out_shape=jax.ShapeDtypeStruct((R, N), F32),
        compiler_params=_cparams(("arbitrary",)),
        name="modulation",
    )(cond, w_mod, b_mod.reshape(1, N))


def _group_mean_sq(t, gmat):
    sq = t * t
    hi = sq.astype(BF16)
    lo = (sq - hi.astype(F32)).astype(BF16)
    return (jnp.dot(hi, gmat, preferred_element_type=F32)
            + jnp.dot(lo, gmat, preferred_element_type=F32))


def _rope(t, cos, sin_lo, sin_hi):
    w = t.shape[-1]
    up = pltpu.roll(t, w - 16, axis=1)
    dn = pltpu.roll(t, 16, axis=1)
    return t * cos + up * sin_lo + dn * sin_hi


def _inproj_kernel(x_ref, sh_ref, sc_ref, w_ref, gq_ref, gk_ref, cos_ref, slo_ref, shi_ref, gmat_ref,
                   q_ref, kt_ref, v_ref, zg_ref, zc_ref):
    h = _ln(x_ref[0]) * (1.0 + sc_ref[0]) + sh_ref[0]
    p = jnp.dot(h.astype(BF16), w_ref[...], preferred_element_type=F32)
    q = p[:, :ATTN_WIDTH]
    k = p[:, ATTN_WIDTH:ATTN_WIDTH + KV_WIDTH]
    v = p[:, ATTN_WIDTH + KV_WIDTH:QKV_WIDTH]
    gm_w = (p.shape[1] - QKV_WIDTH) // 2
    zg_ref[0] = p[:, QKV_WIDTH:QKV_WIDTH + gm_w]
    zc_ref[0] = p[:, QKV_WIDTH + gm_w:]
    v_ref[0] = v.astype(BF16)

    gmat = gmat_ref[...]
    cos, slo, shi = cos_ref[...], slo_ref[...], shi_ref[...]
    qn = q * lax.rsqrt(_group_mean_sq(q, gmat) + EPS) * gq_ref[...]
    rep = ATTN_WIDTH // KV_WIDTH
    qr = _rope(qn, jnp.concatenate([cos] * rep, axis=1), jnp.concatenate([slo] * rep, axis=1),
               jnp.concatenate([shi] * rep, axis=1))
    qr = (qr * (HEAD_DIM ** -0.5 * math.log2(math.e))).astype(BF16)
    for hh in range(N_Q_HEADS):
        q_ref[0, hh] = qr[:, hh * HEAD_DIM:(hh + 1) * HEAD_DIM]

    kn = k * lax.rsqrt(_group_mean_sq(k, gmat[:KV_WIDTH, :KV_WIDTH]) + EPS) * gk_ref[...]
    kt = _rope(kn, cos, slo, shi).T
    for g in range(N_KV_HEADS):
        kt_ref[0, g] = kt[g * HEAD_DIM:(g + 1) * HEAD_DIM, :].astype(BF16)


def _inproj(x, sh, sc, w_in_bf, gq, gk, cos, slo, shi, gmat, *, tile):
    B, L, D = x.shape
    W = w_in_bf.shape[1]
    gm_w = (W - QKV_WIDTH) // 2
    nt = L // tile
    bmap = lambda b, i: (b, 0, 0)
    cmap = lambda b, i: (0, 0)
    return pl.pallas_call(
        _inproj_kernel,
        grid=(B, nt),
        in_specs=[pl.BlockSpec((1, tile, D), lambda b, i: (b, i, 0)),
                  pl.BlockSpec((1, 1, D), bmap), pl.BlockSpec((1, 1, D), bmap),
                  pl.BlockSpec((D, W), cmap),
                  pl.BlockSpec((1, ATTN_WIDTH), cmap), pl.BlockSpec((1, KV_WIDTH), cmap),
                  pl.BlockSpec((tile, KV_WIDTH), lambda b, i: (i, 0)),
                  pl.BlockSpec((tile, KV_WIDTH), lambda b, i: (i, 0)),
                  pl.BlockSpec((tile, KV_WIDTH), lambda b, i: (i, 0)),
                  pl.BlockSpec((ATTN_WIDTH, ATTN_WIDTH), cmap)],
        out_specs=[pl.BlockSpec((1, N_Q_HEADS, tile, HEAD_DIM), lambda b, i: (b, 0, i, 0)),
                   pl.BlockSpec((1, N_KV_HEADS, HEAD_DIM, tile), lambda b, i: (b, 0, 0, i)),
                   pl.BlockSpec((1, tile, KV_WIDTH), lambda b, i: (b, i, 0)),
                   pl.BlockSpec((1, tile, gm_w), lambda b, i: (b, i, 0)),
                   pl.BlockSpec((1, tile, gm_w), lambda b, i: (b, i, 0))],
        out_shape=[jax.ShapeDtypeStruct((B, N_Q_HEADS, L, HEAD_DIM), BF16),
                   jax.ShapeDtypeStruct((B, N_KV_HEADS, HEAD_DIM, L), BF16),
                   jax.ShapeDtypeStruct((B, L, KV_WIDTH), BF16),
                   jax.ShapeDtypeStruct((B, L, gm_w), F32),
                   jax.ShapeDtypeStruct((B, L, gm_w), F32)],
        compiler_params=_cparams(("arbitrary", "arbitrary")),
        name="inproj",
    )(x, sh, sc, w_in_bf, gq, gk, cos, slo, shi, gmat)


def _attn_kernel(q_ref, kt_ref, vx_ref, o_ref, p_ref, *, tq, tk, nk):
    m_rows = Q_PER_KV * tq
    qs = [q_ref[0, Q_PER_KV * g:Q_PER_KV * (g + 1)].reshape(m_rows, HEAD_DIM) for g in range(N_KV_HEADS)]

    def pv(c, par, g):
        off = pl.multiple_of(c * tk, tk)
        return jnp.dot(p_ref[par, g], vx_ref[0, g, pl.ds(off, tk), :], preferred_element_type=F32)

    def step(c, par, carry):
        off = pl.multiple_of(c * tk, tk)
        out = []
        for g in range(N_KV_HEADS):
            m, acc = carry[g]
            s = jnp.dot(qs[g], kt_ref[0, g, :, pl.ds(off, tk)], preferred_element_type=F32)
            m_new = jnp.maximum(m, jnp.max(s, axis=-1, keepdims=True))
            p_ref[1 - par, g] = jnp.exp2(s - m_new).astype(BF16)
            if acc is not None:
                acc = jnp.exp2(m - m_new) * (acc + pv(c - 1, par, g))
            out.append((m_new, acc))
        return tuple(out)

    first = step(0, 1, tuple((jnp.full((m_rows, 1), -1e30, F32), None) for _ in range(N_KV_HEADS)))
    carry = tuple((m, jnp.zeros((m_rows, 2 * HEAD_DIM), F32)) for m, _ in first)

    n_pairs = (nk - 1) // 2
    def pair(t, carry):
        carry = step(2 * t + 1, 0, carry)
        return step(2 * t + 2, 1, carry)
    carry = lax.fori_loop(0, n_pairs, pair, carry)
    par = 0
    if (nk - 1) % 2:
        carry = step(nk - 1, 0, carry)
        par = 1

    for g in range(N_KV_HEADS):
        acc = carry[g][1] + pv(nk - 1, par, g)
        og = acc / pltpu.roll(acc, HEAD_DIM, axis=1)
        for r in range(Q_PER_KV):
            hh = Q_PER_KV * g + r
            o_ref[0, :, hh * HEAD_DIM:(hh + 1) * HEAD_DIM] = og[r * tq:(r + 1) * tq, :HEAD_DIM].astype(o_ref.dtype)


def _attention(q, kt, v, *, tq, tk):
    B, _, L, _ = q.shape
    LK = kt.shape[-1]
    ones = jnp.ones((B, LK, HEAD_DIM), v.dtype)
    vx = jnp.stack([jnp.concatenate([v[..., g * HEAD_DIM:(g + 1) * HEAD_DIM], ones], axis=-1)
                    for g in range(N_KV_HEADS)], axis=1)
    return pl.pallas_call(
        functools.partial(_attn_kernel, tq=tq, tk=tk, nk=LK // tk),
        grid=(B, L // tq),
        in_specs=[pl.BlockSpec((1, N_Q_HEADS, tq, HEAD_DIM), lambda b, i: (b, 0, i, 0)),
                  pl.BlockSpec((1, N_KV_HEADS, HEAD_DIM, LK), lambda b, i: (b, 0, 0, 0)),
                  pl.BlockSpec((1, N_KV_HEADS, LK, 2 * HEAD_DIM), lambda b, i: (b, 0, 0, 0))],
        out_specs=pl.BlockSpec((1, tq, ATTN_WIDTH), lambda b, i: (b, i, 0)),
        out_shape=jax.ShapeDtypeStruct((B, L, ATTN_WIDTH), BF16),
        scratch_shapes=[pltpu.VMEM((2, N_KV_HEADS, Q_PER_KV * tq, tk), BF16)],
        compiler_params=_cparams(("arbitrary", "arbitrary")),
        name="attention",
    )(q, kt, vx)


def _mix_kernel(x_ref, o_ref, zg_ref, zc_ref, zp_ref, zn_ref, g1_ref, sh2_ref, sc2_ref,
                gln_g_ref, gln_b_ref, ws_ref, bs_ref, wdw_ref, bdw_ref, cln_g_ref, cln_b_ref,
                wpw_ref, bpw_ref, wout_ref, ln1g_ref, ln1b_ref,
                x1_ref, h2_ref, ybuf_ref, *, tile, alpha):
    i = pl.program_id(1)
    nt = pl.num_programs(1)
    gw = zg_ref.shape[-1] // 2
    cw = zc_ref.shape[-1] // 2

    z = _gelu_tanh(zg_ref[0])
    u, v = z[:, :gw], z[:, gw:]
    vn = (_ln(v) * gln_g_ref[...] + gln_b_ref[...]).astype(BF16)
    grp = lax.broadcasted_iota(jnp.int32, (CHUNK, gw), 1) // (gw // GMLP_GROUPS)
    sg = []
    for n in range(tile // CHUNK):
        vc = vn[n * CHUNK:(n + 1) * CHUNK]
        s = jnp.zeros((CHUNK, gw), F32)
        for g in range(GMLP_GROUPS):
            s = jnp.where(grp == g, jnp.dot(ws_ref[g], vc, preferred_element_type=F32), s)
        sg.append(s + bs_ref[...])
    y_g = u * jnp.concatenate(sg, axis=0)

    def glu(zz):
        return zz[:, :cw] * jax.nn.sigmoid(zz[:, cw:])

    ybuf_ref[0:CONV_HALO] = glu(zp_ref[0]) * (i > 0).astype(F32)
    ybuf_ref[CONV_HALO:CONV_HALO + tile] = glu(zc_ref[0])
    ybuf_ref[CONV_HALO + tile:2 * CONV_HALO + tile] = glu(zn_ref[0]) * (i < nt - 1).astype(F32)
    base = CONV_HALO - CONV_K // 2
    conv = jnp.zeros((tile, cw), F32)
    for kk in range(CONV_K):
        conv = conv + ybuf_ref[base + kk:base + kk + tile, :] * wdw_ref[kk:kk + 1, :]
    conv = conv + bdw_ref[...]
    t = _ln(conv) * cln_g_ref[...] + cln_b_ref[...]
    t = t * jax.nn.sigmoid(t)
    y_c = jnp.dot(t.astype(BF16), wpw_ref[...], preferred_element_type=F32) + bpw_ref[...]

    aw = o_ref.shape[-1]
    y = (jnp.dot(o_ref[0], wout_ref[0:aw, :], preferred_element_type=F32)
         + jnp.dot(y_g.astype(BF16), wout_ref[aw:aw + gw, :], preferred_element_type=F32)
         + jnp.dot(y_c.astype(BF16), wout_ref[aw + gw:, :], preferred_element_type=F32))
    x1 = _ln(alpha * x_ref[0] + g1_ref[0] * y) * ln1g_ref[...] + ln1b_ref[...]
    x1_ref[0] = x1
    h2_ref[0] = _ln(x1) * (1.0 + sc2_ref[0]) + sh2_ref[0]


def _mixers(x, o, zg, zc, g1, sh2, sc2, prm, *, tile, alpha):
    B, L, D = x.shape
    nt = L // tile
    hb = tile // CONV_HALO
    nhb = L // CONV_HALO
    gw2, cw2 = zg.shape[-1], zc.shape[-1]
    tmap = lambda b, i: (b, i, 0)
    bmap = lambda b, i: (b, 0, 0)
    c2 = lambda b, i: (0, 0)
    c3 = lambda b, i: (0, 0, 0)
    in_specs = [pl.BlockSpec((1, tile, D), tmap),
                pl.BlockSpec((1, tile, o.shape[-1]), tmap),
                pl.BlockSpec((1, tile, gw2), tmap),
                pl.BlockSpec((1, tile, cw2), tmap),
                pl.BlockSpec((1, CONV_HALO, cw2), lambda b, i: (b, jnp.maximum(i * hb - 1, 0), 0)),
                pl.BlockSpec((1, CONV_HALO, cw2), lambda b, i: (b, jnp.minimum((i + 1) * hb, nhb - 1), 0)),
                pl.BlockSpec((1, 1, D), bmap), pl.BlockSpec((1, 1, D), bmap), pl.BlockSpec((1, 1, D), bmap)]
    for a in prm:
        in_specs.append(pl.BlockSpec(a.shape, c2 if a.ndim == 2 else c3))
    return pl.pallas_call(
        functools.partial(_mix_kernel, tile=tile, alpha=alpha),
        grid=(B, nt),
        in_specs=in_specs,
        out_specs=[pl.BlockSpec((1, tile, D), tmap), pl.BlockSpec((1, tile, D), tmap)],
        out_shape=[jax.ShapeDtypeStruct((B, L, D), F32), jax.ShapeDtypeStruct((B, L, D), F32)],
        scratch_shapes=[pltpu.VMEM((tile + 2 * CONV_HALO, cw2 // 2), F32)],
        compiler_params=_cparams(("arbitrary", "arbitrary")),
        name="mixers",
    )(x, o, zg, zc, zc, zc, g1, sh2, sc2, *prm)


def _oddeven_merge_sort_pairs(n):
    pairs = []
    p = 1
    while p < n:
        k = p
        while k >= 1:
            for j in range(k % p, n - k, 2 * k):
                for i in range(min(k, n - j - k)):
                    if (i + j) // (2 * p) == (i + j + k) // (2 * p):
                        pairs.append((i + j, i + j + k))
            k //= 2
        p *= 2
    return pairs


_SORT16 = _oddeven_merge_sort_pairs(16)
_BITONIC16 = [(i, i + d) for d in (8, 4, 2, 1) for i in range(16) if (i & d) == 0]


def _before(a, ia, b, ib):
    return (a > b) | ((a == b) & (ia < ib))


def _cmpx(vals, keys, pay, i, j):
    keep = _before(vals[i], keys[i], vals[j], keys[j])
    vals[i], vals[j] = jnp.where(keep, vals[i], vals[j]), jnp.where(keep, vals[j], vals[i])
    keys[i], keys[j] = jnp.where(keep, keys[i], keys[j]), jnp.where(keep, keys[j], keys[i])
    if pay is not None:
        pay[i], pay[j] = jnp.where(keep, pay[i], pay[j]), jnp.where(keep, pay[j], pay[i])


def _top16_merge(av, ak, ap, bv, bk, bp):
    n = len(av)
    ov, ok, op = [], [], []
    for i in range(n):
        keep = _before(av[i], ak[i], bv[n - 1 - i], bk[n - 1 - i])
        ov.append(jnp.where(keep, av[i], bv[n - 1 - i]))
        ok.append(jnp.where(keep, ak[i], bk[n - 1 - i]))
        if ap is not None:
            op.append(jnp.where(keep, ap[i], bp[n - 1 - i]))
    return ov, ok, (op if ap is not None else None)


def _stage2_candidates():
    return [(a, b) for a in range(PEER_TOPK) for b in range(PEER_TOPK) if (a + 1) * (b + 1) <= PEER_TOPK]


def _retr_kernel(h_ref, wq_ref, keys_ref, gate_ref, eidx_ref, q_s, sv_s, si_s, *, tile):
    q_s[...] = jnp.dot(h_ref[...].astype(BF16), wq_ref[...], preferred_element_type=F32)
    sub = lax.broadcasted_iota(jnp.int32, (8, tile), 0)
    sv_s[...] = jnp.zeros(sv_s.shape, F32)
    si_s[...] = jnp.zeros(si_s.shape, jnp.int32)

    def stage1(hp, _):
        qh = q_s[:, pl.ds(pl.multiple_of(hp * PEER_HALF, PEER_HALF), PEER_HALF)].astype(BF16)
        st = lax.dot_general(keys_ref[hp], qh, (((1,), (1,)), ((), ())),
                             preferred_element_type=F32)
        st = st.reshape(N_KEYS // 8, 8, tile)
        vals = [st[i] for i in range(16)]
        keys = [sub + 8 * i for i in range(16)]
        for (i, j) in _SORT16:
            _cmpx(vals, keys, None, i, j)
        for shift in (4, 2, 1):
            bv = [pltpu.roll(t, shift, axis=0) for t in vals]
            bk = [pltpu.roll(t, shift, axis=0) for t in keys]
            vals, keys, _ = _top16_merge(vals, keys, None, bv, bk, None)
            for (i, j) in _BITONIC16:
                _cmpx(vals, keys, None, i, j)
        head = hp // 2
        half = hp % 2
        for i in range(16):
            sv_s[half, i] = jnp.where(sub == head, vals[i], sv_s[half, i])
            si_s[half, i] = jnp.where(sub == head, keys[i], si_s[half, i])
        return 0

    lax.fori_loop(0, 2 * PEER_HEADS, stage1, 0)

    cands = _stage2_candidates()
    n_pad = 64
    vals, poss, exps = [], [], []
    for (a, b) in cands:
        vals.append(sv_s[0, a] + sv_s[1, b])
        poss.append(jnp.full((8, tile), a * PEER_TOPK + b, jnp.int32))
        exps.append(si_s[0, a] * N_KEYS + si_s[1, b])
    for t in range(n_pad - len(cands)):
        vals.append(jnp.full((8, tile), -jnp.inf, F32))
        poss.append(jnp.full((8, tile), PEER_TOPK * PEER_TOPK + t, jnp.int32))
        exps.append(jnp.zeros((8, tile), jnp.int32))
    groups = []
    for g in range(n_pad // 16):
        gv, gk, gp = vals[16 * g:16 * g + 16], poss[16 * g:16 * g + 16], exps[16 * g:16 * g + 16]
        for (i, j) in _SORT16:
            _cmpx(gv, gk, gp, i, j)
        groups.append((gv, gk, gp))
    while len(groups) > 1:
        nxt = []
        for g in range(0, len(groups), 2):
            mv, mk, mp = _top16_merge(*groups[g], *groups[g + 1])
            if len(groups) > 2:
                for (i, j) in _BITONIC16:
                    _cmpx(mv, mk, mp, i, j)
            nxt.append((mv, mk, mp))
        groups = nxt
    best, _, experts = groups[0]

    mx = functools.reduce(jnp.maximum, best)
    ex = [jnp.exp(b - mx) for b in best]
    inv = 1.0 / functools.reduce(lambda a, b: a + b, ex)
    gate_ref[...] = jnp.concatenate([e * inv for e in ex], axis=0).T
    eidx_ref[...] = jnp.concatenate(experts, axis=0).T


def _retrieval(h2, wq_bf, keys_bf, *, tile):
    N, D = h2.shape
    QW = wq_bf.shape[1]
    return pl.pallas_call(
        functools.partial(_retr_kernel, tile=tile),
        grid=(N // tile,),
        in_specs=[pl.BlockSpec((tile, D), lambda i: (i, 0)),
                  pl.BlockSpec((D, QW), lambda i: (0, 0)),
                  pl.BlockSpec(keys_bf.shape, lambda i: (0, 0, 0))],
        out_specs=[pl.BlockSpec((tile, PEER_TOPK * PEER_HEADS), lambda i: (i, 0)),
                   pl.BlockSpec((tile, PEER_TOPK * PEER_HEADS), lambda i: (i, 0))],
        out_shape=[jax.ShapeDtypeStruct((N, PEER_TOPK * PEER_HEADS), F32),
                   jax.ShapeDtypeStruct((N, PEER_TOPK * PEER_HEADS), jnp.int32)],
        scratch_shapes=[pltpu.VMEM((tile, QW), F32),
                        pltpu.VMEM((2, PEER_TOPK, PEER_HEADS, tile), F32),
                        pltpu.VMEM((2, PEER_TOPK, PEER_HEADS, tile), jnp.int32)],
        compiler_params=_cparams(("arbitrary",)),
        name="retrieval",
    )(h2, wq_bf, keys_bf)


def _expert_token(r, h_row, gate_col):
    u = pltpu.bitcast(r & jnp.uint32(0xFFFF0000), F32)
    v = pltpu.bitcast(r << 16, F32)
    d = jnp.sum(u * h_row, axis=-1, keepdims=True)
    w = jnp.broadcast_to(gate_col * _gelu_tanh(d), r.shape)
    return jnp.sum(v * w, axis=0, keepdims=True)


def _expert_kernel(*refs, gtok, alpha, dense):
    if dense:
        (idx0_ref, idxn_ref, tab_ref, h_ref, gate_ref, x1_ref, g2_ref, rows_ref, hd_ref, gated_ref, x1d_ref, g2d_ref,
         lng_ref, lnb_ref, o_ref, od_ref, buf_a, buf_b, f_ref, fd_ref, sem_ref) = refs
    else:
        (idx0_ref, idxn_ref, tab_ref, h_ref, gate_ref, x1_ref, g2_ref, lng_ref, lnb_ref,
         o_ref, buf_a, buf_b, f_ref, fd_ref, sem_ref) = refs
    i = pl.program_id(0)
    n_steps = pl.num_programs(0)
    per_tok = PEER_HEADS * PEER_TOPK
    d_model = h_ref.shape[-1]

    def start_token(idx_ref, buf, sem_row, j):
        for g in range(per_tok // 8):
            for t in range(8):
                e = idx_ref[0, 0, j * per_tok + g * 8 + t]
                pltpu.make_async_copy(tab_ref.at[e], buf.at[j, g, pl.ds(t, 1), :],
                                      sem_ref.at[sem_row, j]).start(priority=t % 2)

    def wait_token(buf, other, sem_row, j):
        pltpu.make_async_copy(other.at[j], buf.at[j], sem_ref.at[sem_row, j]).wait()

    def token(j, cur, nxt, cur_row):
        wait_token(cur, nxt, cur_row, j)
        start_token(idxn_ref, nxt, 1 - cur_row, j)
        f_ref[j:j + 1, :] = _expert_token(cur[j].reshape(per_tok, d_model), h_ref[j:j + 1, :],
                                          gate_t[:, j:j + 1])
        if dense:
            fd_ref[j:j + 1, :] = _expert_token(rows_ref[j * per_tok:(j + 1) * per_tok, :], hd_ref[j:j + 1, :],
                                               gated_t[:, j:j + 1])

    @pl.when(i == 0)
    def _():
        def prime(j, _):
            start_token(idx0_ref, buf_a, 0, j)
            return 0
        lax.fori_loop(0, gtok, prime, 0)

    gate_t = gate_ref[...].T
    gated_t = gated_ref[...].T if dense else None

    @pl.when(i % 2 == 0)
    def _():
        for j in range(gtok):
            token(j, buf_a, buf_b, 0)

    @pl.when(i % 2 == 1)
    def _():
        for j in range(gtok):
            token(j, buf_b, buf_a, 1)

    @pl.when((i == n_steps - 1) & (i % 2 == 0))
    def _():
        lax.fori_loop(0, gtok, lambda j, c: (wait_token(buf_b, buf_a, 1, j), 0)[1], 0)

    @pl.when((i == n_steps - 1) & (i % 2 == 1))
    def _():
        lax.fori_loop(0, gtok, lambda j, c: (wait_token(buf_a, buf_b, 0, j), 0)[1], 0)

    o_ref[...] = _ln(alpha * x1_ref[...] + g2_ref[0] * f_ref[...]) * lng_ref[...] + lnb_ref[...]

    if dense:
        od_ref[...] = _ln(alpha * x1d_ref[...] + g2d_ref[0] * fd_ref[...]) * lng_ref[...] + lnb_ref[...]


def _experts(eidx, gate, tab, h2, x1, g2, lng, lnb, *, tok0, n_tok, gtok, seq, alpha, dense_rows=None,
             dense_tok0=0):
    D = h2.shape[1]
    S = n_tok // gtok
    off = tok0 // gtok
    doff = dense_tok0 // gtok
    per_tok = PEER_HEADS * PEER_TOPK
    rows = gtok * per_tok
    spb = seq // gtok
    smem = pltpu.MemorySpace.SMEM
    dense = dense_rows is not None
    tok_spec = lambda o: pl.BlockSpec((gtok, D), lambda i: (i + o, 0))
    gate_spec = lambda o: pl.BlockSpec((gtok, per_tok), lambda i: (i + o, 0))
    g2_spec = lambda o: pl.BlockSpec((1, 1, D), lambda i: ((i + o) // spb, 0, 0))
    row_spec = pl.BlockSpec((1, D), lambda i: (0, 0))
    in_specs = [pl.BlockSpec((1, 1, rows), lambda i: (off, 0, 0), memory_space=smem),
                pl.BlockSpec((1, 1, rows), lambda i: (jnp.minimum(i + 1, S - 1) + off, 0, 0), memory_space=smem),
                pl.BlockSpec(memory_space=pl.ANY),
                tok_spec(off), gate_spec(off), tok_spec(off), g2_spec(off)]
    args = [eidx, eidx, tab, h2, gate, x1, g2]
    out_spec = pl.BlockSpec((gtok, D), lambda i: (i, 0))
    out_shape = jax.ShapeDtypeStruct((n_tok, D), F32)
    if dense:
        in_specs += [pl.BlockSpec((rows, D), lambda i: (i, 0)), tok_spec(doff), gate_spec(doff), tok_spec(doff),
                     g2_spec(doff)]
        args += [dense_rows, h2, gate, x1, g2]
    return pl.pallas_call(
        functools.partial(_expert_kernel, gtok=gtok, alpha=alpha, dense=dense),
        grid=(S,),
        in_specs=in_specs + [row_spec, row_spec],
        out_specs=[out_spec, out_spec] if dense else out_spec,
        out_shape=[out_shape, out_shape] if dense else out_shape,
        scratch_shapes=[pltpu.VMEM((gtok, per_tok // 8, 8, D), jnp.uint32),
                        pltpu.VMEM((gtok, per_tok // 8, 8, D), jnp.uint32),
                        pltpu.VMEM((gtok, D), F32),
                        pltpu.VMEM((gtok, D), F32),
                        pltpu.SemaphoreType.DMA((2, gtok))],
        compiler_params=_cparams(("arbitrary",)),
        name="experts",
    )(*args, lng, lnb)


def _expert_dense_kernel(rows_ref, h_ref, gate_ref, x1_ref, g2_ref, lng_ref, lnb_ref, o_ref, f_ref, *, gtok, alpha):
    per_tok = PEER_HEADS * PEER_TOPK
    gate_t = gate_ref[...].T
    for j in range(gtok):
        f_ref[j:j + 1, :] = _expert_token(rows_ref[j * per_tok:(j + 1) * per_tok, :], h_ref[j:j + 1, :],
                                          gate_t[:, j:j + 1])
    o_ref[...] = _ln(alpha * x1_ref[...] + g2_ref[0] * f_ref[...]) * lng_ref[...] + lnb_ref[...]


def _experts_dense(rows, gate, h2, x1, g2, lng, lnb, *, tok0, n_tok, gtok, seq, alpha):
    D = h2.shape[1]
    per_tok = PEER_HEADS * PEER_TOPK
    off = tok0 // gtok
    steps_per_batch = seq // gtok
    return pl.pallas_call(
        functools.partial(_expert_dense_kernel, gtok=gtok, alpha=alpha),
        grid=(n_tok // gtok,),
        in_specs=[pl.BlockSpec((gtok * per_tok, D), lambda i: (i, 0)),
                  pl.BlockSpec((gtok, D), lambda i: (i + off, 0)),
                  pl.BlockSpec((gtok, per_tok), lambda i: (i + off, 0)),
                  pl.BlockSpec((gtok, D), lambda i: (i + off, 0)),
                  pl.BlockSpec((1, 1, D), lambda i: ((i + off) // steps_per_batch, 0, 0)),
                  pl.BlockSpec((1, D), lambda i: (0, 0)),
                  pl.BlockSpec((1, D), lambda i: (0, 0))],
        out_specs=pl.BlockSpec((gtok, D), lambda i: (i, 0)),
        out_shape=jax.ShapeDtypeStruct((n_tok, D), F32),
        scratch_shapes=[pltpu.VMEM((gtok, D), F32)],
        compiler_params=_cparams(("arbitrary",)),
        name="experts_dense",
    )(rows, h2, gate, x1, g2, lng, lnb)


SC_WORKERS = 32
SC_IDX_BLOCK = 128
SC_CHUNK = 32
PEER_PHASES = 4
SC_SHARE_NUM, SC_SHARE_DEN = 7, 16


def _sc_gather_rows(tab, idx2d, row0, n_rows):
    from jax.experimental.pallas import tpu_sc as plsc
    D = tab.shape[1]
    steps = n_rows // SC_WORKERS
    n_chunks = SC_IDX_BLOCK // SC_CHUNK
    mesh = plsc.VectorSubcoreMesh(core_axis_name="core", subcore_axis_name="subcore")

    @pl.kernel(out_type=jax.ShapeDtypeStruct((n_rows * SC_IDX_BLOCK, D), tab.dtype), mesh=mesh,
               scratch_types=[pltpu.VMEM((1, SC_IDX_BLOCK), jnp.int32),
                              pltpu.VMEM((2, SC_CHUNK, D), tab.dtype),
                              pltpu.SemaphoreType.DMA((2,))])
    def k(x_hbm, i_hbm, o_hbm, i_vmem, buf, wsem):
        wid = lax.axis_index("core") * (SC_WORKERS // 2) + lax.axis_index("subcore")

        def write(row, c):
            return pltpu.make_async_copy(buf.at[c % 2], o_hbm.at[pl.ds(row * SC_IDX_BLOCK + c * SC_CHUNK, SC_CHUNK)],
                                         wsem.at[c % 2])

        @pl.loop(0, steps)
        def _(st):
            row = wid * steps + st
            pltpu.sync_copy(i_hbm.at[pl.ds(row0 + row, 1)], i_vmem)
            for c in range(n_chunks):
                if c >= 2:
                    write(row, c - 2).wait()
                pltpu.sync_copy(x_hbm.at[i_vmem.at[0, pl.ds(c * SC_CHUNK, SC_CHUNK)]], buf.at[c % 2])
                write(row, c).start()
            for c in range(n_chunks - 2, n_chunks):
                write(row, c).wait()
    return k(tab, idx2d)


def _rope_tables(L):
    half = HEAD_DIM // 4
    t = np.arange(L)
    freqs = ROPE_THETA ** (-np.arange(half, dtype=np.float32) / half)
    lane = np.arange(HEAD_DIM)
    pos = np.where(lane[None, :] < HEAD_DIM // 2, (t // GRID_W)[:, None], (t % GRID_W)[:, None]).astype(np.float32)
    ang = pos * freqs[lane % half][None, :].astype(np.float32)
    cos, sin = np.cos(ang), np.sin(ang)
    first = ((lane % (2 * half)) < half)[None, :]
    slo = np.where(first, -sin, 0.0)
    shi = np.where(first, 0.0, sin)
    tile = lambda a: jnp.asarray(np.tile(a, (1, N_KV_HEADS)), F32)
    return tile(cos), tile(slo), tile(shi)


def _pack_tables(u, v):
    ub = lax.bitcast_convert_type(u.astype(BF16), jnp.uint16).astype(jnp.uint32)
    vb = lax.bitcast_convert_type(v.astype(BF16), jnp.uint16).astype(jnp.uint32)
    return (ub << 16) | vb


def _pick(n, prefs):
    for p in prefs:
        if n % p == 0:
            return p
    return n


def kernel(x, c, ctx, c_ctx, w_mod, b_mod, w_in, q_norm_g, k_norm_g, gmlp_ln_g, gmlp_ln_b, gmlp_w_s, gmlp_b_s,
           conv_w_dw, conv_b_dw, conv_ln_g, conv_ln_b, conv_w_pw, conv_b_pw, w_out, ln1_g, ln1_b, ln2_g, ln2_b,
           peer_w_q, peer_sub_keys, peer_u, peer_v):
    B, L, D = x.shape
    LC = ctx.shape[1]
    depth = w_mod.shape[0]
    alpha = (2 * depth) ** 0.25
    gw = gmlp_ln_g.shape[-1]

    cos, slo, shi = _rope_tables(L)
    ones_c = jnp.ones((LC, KV_WIDTH), F32)
    zeros_c = jnp.zeros((LC, KV_WIDTH), F32)
    gidx = np.arange(ATTN_WIDTH) // HEAD_DIM
    gmat = jnp.asarray((gidx[:, None] == gidx[None, :]) / HEAD_DIM, BF16)

    cond = jnp.concatenate([c, c_ctx[None, :], jnp.zeros((-(B + 1) % 8, D), F32)], axis=0)

    t_in = _pick(L, (512, 256, 128))
    t_mix = _pick(L, (256, 128))
    t_inc = _pick(LC, (512, 256, 128))
    t_mixc = _pick(LC, (256, 128))
    tk_lat = _pick(L + LC, (768, 512, 384, 256, 128))
    tk_ctx = _pick(LC, (512, 256, 128))
    gtok = 16

    def peer(h2, x1, g2, l, seq, wq_bf, keys_bf, tab):
        n = h2.shape[0] * h2.shape[1]
        gate, eidx = _retrieval(h2.reshape(n, D), wq_bf, keys_bf, tile=_pick(n, (512, 256, 128)))
        h2f, x1f = h2.reshape(n, D), x1.reshape(n, D)
        lng, lnb = ln2_g[l].reshape(1, D), ln2_b[l].reshape(1, D)
        eidx3 = eidx.reshape(n // gtok, 1, -1)
        common = dict(gtok=gtok, seq=seq, alpha=alpha)
        if n < 8192:
            out = _experts(eidx3, gate, tab, h2f, x1f, g2, lng, lnb, tok0=0, n_tok=n, **common)
            return out.reshape(h2.shape)
        align = SC_WORKERS * gtok
        cn = n * SC_SHARE_NUM // SC_SHARE_DEN // PEER_PHASES // align * align
        sc0 = n - PEER_PHASES * cn
        a0 = sc0 - (PEER_PHASES - 1) * cn
        tab2 = tab.reshape(-1, D)
        outs_tc, outs_sc, rows_prev = [], [], None
        for p in range(PEER_PHASES):
            rows_p = _sc_gather_rows(tab2, eidx, sc0 + p * cn, cn)
            if rows_prev is None:
                outs_tc.append(_experts(eidx3, gate, tab, h2f, x1f, g2, lng, lnb, tok0=0, n_tok=a0, **common))
            else:
                res = _experts(eidx3, gate, tab, h2f, x1f, g2, lng, lnb, tok0=a0 + (p - 1) * cn, n_tok=cn,
                               dense_rows=rows_prev, dense_tok0=sc0 + (p - 1) * cn, **common)
                outs_tc.append(res[0])
                outs_sc.append(res[1])
            rows_prev = rows_p
        outs_sc.append(_experts_dense(rows_prev, gate, h2f, x1f, g2, lng, lnb, tok0=n - cn, n_tok=cn, **common))
        out = jnp.concatenate(outs_tc + outs_sc, axis=0)
        return out.reshape(h2.shape)

    x_lat, x_ctx = x, ctx
    for l in range(depth):
        last = l == depth - 1
        mod = _modulation(cond, w_mod[l], b_mod[l])
        m_lat = [mod[:B, k * D:(k + 1) * D].reshape(B, 1, D) for k in range(6)]
        m_ctx = [jnp.broadcast_to(mod[B:B + 1, k * D:(k + 1) * D].reshape(1, 1, D), (B, 1, D)) for k in range(6)]

        w_in_bf = w_in[l].astype(BF16)
        gq = jnp.tile(q_norm_g[l], N_Q_HEADS).reshape(1, ATTN_WIDTH)
        gk = jnp.tile(k_norm_g[l], N_KV_HEADS).reshape(1, KV_WIDTH)
        mix_prm = (gmlp_ln_g[l].reshape(1, gw), gmlp_ln_b[l].reshape(1, gw),
                   gmlp_w_s[l].astype(BF16),
                   jnp.repeat(gmlp_b_s[l].T, gw // GMLP_GROUPS, axis=1),
                   conv_w_dw[l], conv_b_dw[l].reshape(1, -1), conv_ln_g[l].reshape(1, -1),
                   conv_ln_b[l].reshape(1, -1), conv_w_pw[l].astype(BF16), conv_b_pw[l].reshape(1, -1),
                   w_out[l].astype(BF16), ln1_g[l].reshape(1, D), ln1_b[l].reshape(1, D))
        wq_bf = peer_w_q[l].astype(BF16)
        keys_bf = peer_sub_keys[l].reshape(2 * PEER_HEADS, N_KEYS, PEER_HALF).astype(BF16)
        tab = _pack_tables(peer_u[l], peer_v[l]).reshape(-1, 1, D)

        q_l, kt_l, v_l, zg_l, zc_l = _inproj(x_lat, m_lat[0], m_lat[1], w_in_bf, gq, gk, cos, slo, shi, gmat,
                                              tile=t_in)
        q_c, kt_c, v_c, zg_c, zc_c = _inproj(x_ctx, m_ctx[0], m_ctx[1], w_in_bf, gq, gk, ones_c, zeros_c, zeros_c,
                                              gmat, tile=t_inc)
        kt_all = jnp.concatenate([kt_c, kt_l], axis=3)
        v_all = jnp.concatenate([v_c, v_l], axis=1)
        o_lat = _attention(q_l, kt_all, v_all, tq=128, tk=tk_lat)
        x1, h2 = _mixers(x_lat, o_lat, zg_l, zc_l, m_lat[2], m_lat[3], m_lat[4], mix_prm, tile=t_mix, alpha=alpha)
        x_lat_new = peer(h2, x1, m_lat[5], l, L, wq_bf, keys_bf, tab)

        if not last:
            o_ctx = _attention(q_c, kt_c, v_c, tq=128, tk=tk_ctx)
            x1c, h2c = _mixers(x_ctx, o_ctx, zg_c, zc_c, m_ctx[2], m_ctx[3], m_ctx[4], mix_prm, tile=t_mixc,
                               alpha=alpha)
            x_ctx = peer(h2c, x1c, m_ctx[5], l, LC, wq_bf, keys_bf, tab)
        x_lat = x_lat_new
    return x_lat
```

```python
import functools
import math

import jax
import jax.numpy as jnp
import numpy as np
from jax import lax
from jax.experimental import pallas as pl
from jax.experimental.pallas import tpu as pltpu

F32 = jnp.float32
BF16 = jnp.bfloat16

EPS = 1e-6
GRID_W = 64
N_Q_HEADS = 8
N_KV_HEADS = 2
Q_PER_KV = N_Q_HEADS // N_KV_HEADS
HEAD_DIM = 64
ATTN_WIDTH = N_Q_HEADS * HEAD_DIM
KV_WIDTH = N_KV_HEADS * HEAD_DIM
QKV_WIDTH = ATTN_WIDTH + 2 * KV_WIDTH
ROPE_THETA = 10000.0
CHUNK = 128
GMLP_GROUPS = 4
CONV_K = 31
CONV_HALO = 16
PEER_HEADS = 8
N_KEYS = 128
PEER_TOPK = 16
PEER_HALF = 128

VMEM_LIMIT = 56 * 1024 * 1024


def _cparams(sem):
    return pltpu.CompilerParams(dimension_semantics=sem, vmem_limit_bytes=VMEM_LIMIT)


def _gelu_tanh(x):
    return 0.5 * x * (1.0 + jnp.tanh(math.sqrt(2.0 / math.pi) * (x + 0.044715 * (x * x * x))))


def _ln(x):
    mu = jnp.mean(x, axis=-1, keepdims=True)
    xc = x - mu
    var = jnp.mean(xc * xc, axis=-1, keepdims=True)
    return xc * lax.rsqrt(var + EPS)


def _mod_kernel(c_ref, w_ref, b_ref, o_ref):
    c = c_ref[...]
    s = c * jax.nn.sigmoid(c)
    o_ref[...] = jnp.dot(s, w_ref[...], preferred_element_type=F32,
                         precision=lax.Precision.HIGHEST) + b_ref[...]


def _modulation(cond, w_mod, b_mod):
    R, D = cond.shape
    N = w_mod.shape[1]
    tn = 1024
    return pl.pallas_call(
        _mod_kernel,
        grid=(N // tn,),
        in_specs=[pl.BlockSpec((R, D), lambda j: (0, 0)),
                  pl.BlockSpec((D, tn), lambda j: (0, j)),
                  pl.BlockSpec((1, tn), lambda j: (0, j))],
        out_specs=pl.BlockSpec((R, tn), lambda j: (0, j)),
        out_shape=jax.ShapeDtypeStruct((R, N), F32),
        compiler_params=_cparams(("arbitrary",)),
        name="modulation",
    )(cond, w_mod, b_mod.reshape(1, N))


def _group_mean_sq(t, gmat):
    sq = t * t
    hi = sq.astype(BF16)
    lo = (sq - hi.astype(F32)).astype(BF16)
    return (jnp.dot(hi, gmat, preferred_element_type=F32)
            + jnp.dot(lo, gmat, preferred_element_type=F32))


def _rope(t, cos, sin_lo, sin_hi):
    w = t.shape[-1]
    up = pltpu.roll(t, w - 16, axis=1)
    dn = pltpu.roll(t, 16, axis=1)
    return t * cos + up * sin_lo + dn * sin_hi


def _inproj_kernel(x_ref, sh_ref, sc_ref, w_ref, gq_ref, gk_ref, cos_ref, slo_ref, shi_ref, gmat_ref,
                   q_ref, kt_ref, v_ref, zg_ref, zc_ref):
    h = _ln(x_ref[0]) * (1.0 + sc_ref[0]) + sh_ref[0]
    p = jnp.dot(h.astype(BF16), w_ref[...], preferred_element_type=F32)
    q = p[:, :ATTN_WIDTH]
    k = p[:, ATTN_WIDTH:ATTN_WIDTH + KV_WIDTH]
    v = p[:, ATTN_WIDTH + KV_WIDTH:QKV_WIDTH]
    gm_w = (p.shape[1] - QKV_WIDTH) // 2
    zg_ref[0] = p[:, QKV_WIDTH:QKV_WIDTH + gm_w]
    zc_ref[0] = p[:, QKV_WIDTH + gm_w:]
    v_ref[0] = v.astype(BF16)

    gmat = gmat_ref[...]
    cos, slo, shi = cos_ref[...], slo_ref[...], shi_ref[...]
    qn = q * lax.rsqrt(_group_mean_sq(q, gmat) + EPS) * gq_ref[...]
    rep = ATTN_WIDTH // KV_WIDTH
    qr = _rope(qn, jnp.concatenate([cos] * rep, axis=1), jnp.concatenate([slo] * rep, axis=1),
               jnp.concatenate([shi] * rep, axis=1))
    qr = (qr * (HEAD_DIM ** -0.5 * math.log2(math.e))).astype(BF16)
    for hh in range(N_Q_HEADS):
        q_ref[0, hh] = qr[:, hh * HEAD_DIM:(hh + 1) * HEAD_DIM]

    kn = k * lax.rsqrt(_group_mean_sq(k, gmat[:KV_WIDTH, :KV_WIDTH]) + EPS) * gk_ref[...]
    kt = _rope(kn, cos, slo, shi).T
    for g in range(N_KV_HEADS):
        kt_ref[0, g] = kt[g * HEAD_DIM:(g + 1) * HEAD_DIM, :].astype(BF16)


def _inproj(x, sh, sc, w_in_bf, gq, gk, cos, slo, shi, gmat, *, tile):
    B, L, D = x.shape
    W = w_in_bf.shape[1]
    gm_w = (W - QKV_WIDTH) // 2
    nt = L // tile
    bmap = lambda b, i: (b, 0, 0)
    cmap = lambda b, i: (0, 0)
    return pl.pallas_call(
        _inproj_kernel,
        grid=(B, nt),
        in_specs=[pl.BlockSpec((1, tile, D), lambda b, i: (b, i, 0)),
                  pl.BlockSpec((1, 1, D), bmap), pl.BlockSpec((1, 1, D), bmap),
                  pl.BlockSpec((D, W), cmap),
                  pl.BlockSpec((1, ATTN_WIDTH), cmap), pl.BlockSpec((1, KV_WIDTH), cmap),
                  pl.BlockSpec((tile, KV_WIDTH), lambda b, i: (i, 0)),
                  pl.BlockSpec((tile, KV_WIDTH), lambda b, i: (i, 0)),
                  pl.BlockSpec((tile, KV_WIDTH), lambda b, i: (i, 0)),
                  pl.BlockSpec((ATTN_WIDTH, ATTN_WIDTH), cmap)],
        out_specs=[pl.BlockSpec((1, N_Q_HEADS, tile, HEAD_DIM), lambda b, i: (b, 0, i, 0)),
                   pl.BlockSpec((1, N_KV_HEADS, HEAD_DIM, tile), lambda b, i: (b, 0, 0, i)),
                   pl.BlockSpec((1, tile, KV_WIDTH), lambda b, i: (b, i, 0)),
                   pl.BlockSpec((1, tile, gm_w), lambda b, i: (b, i, 0)),
                   pl.BlockSpec((1, tile, gm_w), lambda b, i: (b, i, 0))],
        out_shape=[jax.ShapeDtypeStruct((B, N_Q_HEADS, L, HEAD_DIM), BF16),
                   jax.ShapeDtypeStruct((B, N_KV_HEADS, HEAD_DIM, L), BF16),
                   jax.ShapeDtypeStruct((B, L, KV_WIDTH), BF16),
                   jax.ShapeDtypeStruct((B, L, gm_w), F32),
                   jax.ShapeDtypeStruct((B, L, gm_w), F32)],
        compiler_params=_cparams(("arbitrary", "arbitrary")),
        name="inproj",
    )(x, sh, sc, w_in_bf, gq, gk, cos, slo, shi, gmat)


def _attn_kernel(q_ref, kt_ref, vx_ref, o_ref, p_ref, *, tq, tk, nk):
    m_rows = Q_PER_KV * tq
    qs = [q_ref[0, Q_PER_KV * g:Q_PER_KV * (g + 1)].reshape(m_rows, HEAD_DIM) for g in range(N_KV_HEADS)]

    def pv(c, par, g):
        off = pl.multiple_of(c * tk, tk)
        return jnp.dot(p_ref[par, g], vx_ref[0, g, pl.ds(off, tk), :], preferred_element_type=F32)

    def step(c, par, carry):
        off = pl.multiple_of(c * tk, tk)
        out = []
        for g in range(N_KV_HEADS):
            m, acc = carry[g]
            s = jnp.dot(qs[g], kt_ref[0, g, :, pl.ds(off, tk)], preferred_element_type=F32)
            m_new = jnp.maximum(m, jnp.max(s, axis=-1, keepdims=True))
            p_ref[1 - par, g] = jnp.exp2(s - m_new).astype(BF16)
            if acc is not None:
                acc = jnp.exp2(m - m_new) * (acc + pv(c - 1, par, g))
            out.append((m_new, acc))
        return tuple(out)

    first = step(0, 1, tuple((jnp.full((m_rows, 1), -1e30, F32), None) for _ in range(N_KV_HEADS)))
    carry = tuple((m, jnp.zeros((m_rows, 2 * HEAD_DIM), F32)) for m, _ in first)

    n_pairs = (nk - 1) // 2
    def pair(t, carry):
        carry = step(2 * t + 1, 0, carry)
        return step(2 * t + 2, 1, carry)
    carry = lax.fori_loop(0, n_pairs, pair, carry)
    par = 0
    if (nk - 1) % 2:
        carry = step(nk - 1, 0, carry)
        par = 1

    for g in range(N_KV_HEADS):
        acc = carry[g][1] + pv(nk - 1, par, g)
        og = acc / pltpu.roll(acc, HEAD_DIM, axis=1)
        for r in range(Q_PER_KV):
            hh = Q_PER_KV * g + r
            o_ref[0, :, hh * HEAD_DIM:(hh + 1) * HEAD_DIM] = og[r * tq:(r + 1) * tq, :HEAD_DIM].astype(o_ref.dtype)


def _attention(q, kt, v, *, tq, tk):
    B, _, L, _ = q.shape
    LK = kt.shape[-1]
    ones = jnp.ones((B, LK, HEAD_DIM), v.dtype)
    vx = jnp.stack([jnp.concatenate([v[..., g * HEAD_DIM:(g + 1) * HEAD_DIM], ones], axis=-1)
                    for g in range(N_KV_HEADS)], axis=1)
    return pl.pallas_call(
        functools.partial(_attn_kernel, tq=tq, tk=tk, nk=LK // tk),
        grid=(B, L // tq),
        in_specs=[pl.BlockSpec((1, N_Q_HEADS, tq, HEAD_DIM), lambda b, i: (b, 0, i, 0)),
                  pl.BlockSpec((1, N_KV_HEADS, HEAD_DIM, LK), lambda b, i: (b, 0, 0, 0)),
                  pl.BlockSpec((1, N_KV_HEADS, LK, 2 * HEAD_DIM), lambda b, i: (b, 0, 0, 0))],
        out_specs=pl.BlockSpec((1, tq, ATTN_WIDTH), lambda b, i: (b, i, 0)),
        out_shape=jax.ShapeDtypeStruct((B, L, ATTN_WIDTH), BF16),
        scratch_shapes=[pltpu.VMEM((2, N_KV_HEADS, Q_PER_KV * tq, tk), BF16)],
        compiler_params=_cparams(("arbitrary", "arbitrary")),
        name="attention",
    )(q, kt, vx)


def _mix_kernel(x_ref, o_ref, zg_ref, zc_ref, zp_ref, zn_ref, g1_ref, sh2_ref, sc2_ref,
                gln_g_ref, gln_b_ref, ws_ref, bs_ref, wdw_ref, bdw_ref, cln_g_ref, cln_b_ref,
                wpw_ref, bpw_ref, wout_ref, ln1g_ref, ln1b_ref,
                x1_ref, h2_ref, ybuf_ref, *, tile, alpha):
    i = pl.program_id(1)
    nt = pl.num_programs(1)
    gw = zg_ref.shape[-1] // 2
    cw = zc_ref.shape[-1] // 2

    z = _gelu_tanh(zg_ref[0])
    u, v = z[:, :gw], z[:, gw:]
    vn = (_ln(v) * gln_g_ref[...] + gln_b_ref[...]).astype(BF16)
    grp = lax.broadcasted_iota(jnp.int32, (CHUNK, gw), 1) // (gw // GMLP_GROUPS)
    sg = []
    for n in range(tile // CHUNK):
        vc = vn[n * CHUNK:(n + 1) * CHUNK]
        s = jnp.zeros((CHUNK, gw), F32)
        for g in range(GMLP_GROUPS):
            s = jnp.where(grp == g, jnp.dot(ws_ref[g], vc, preferred_element_type=F32), s)
        sg.append(s + bs_ref[...])
    y_g = u * jnp.concatenate(sg, axis=0)

    def glu(zz):
        return zz[:, :cw] * jax.nn.sigmoid(zz[:, cw:])

    ybuf_ref[0:CONV_HALO] = glu(zp_ref[0]) * (i > 0).astype(F32)
    ybuf_ref[CONV_HALO:CONV_HALO + tile] = glu(zc_ref[0])
    ybuf_ref[CONV_HALO + tile:2 * CONV_HALO + tile] = glu(zn_ref[0]) * (i < nt - 1).astype(F32)
    base = CONV_HALO - CONV_K // 2
    conv = jnp.zeros((tile, cw), F32)
    for kk in range(CONV_K):
        conv = conv + ybuf_ref[base + kk:base + kk + tile, :] * wdw_ref[kk:kk + 1, :]
    conv = conv + bdw_ref[...]
    t = _ln(conv) * cln_g_ref[...] + cln_b_ref[...]
    t = t * jax.nn.sigmoid(t)
    y_c = jnp.dot(t.astype(BF16), wpw_ref[...], preferred_element_type=F32) + bpw_ref[...]

    aw = o_ref.shape[-1]
    y = (jnp.dot(o_ref[0], wout_ref[0:aw, :], preferred_element_type=F32)
         + jnp.dot(y_g.astype(BF16), wout_ref[aw:aw + gw, :], preferred_element_type=F32)
         + jnp.dot(y_c.astype(BF16), wout_ref[aw + gw:, :], preferred_element_type=F32))
    x1 = _ln(alpha * x_ref[0] + g1_ref[0] * y) * ln1g_ref[...] + ln1b_ref[...]
    x1_ref[0] = x1
    h2_ref[0] = _ln(x1) * (1.0 + sc2_ref[0]) + sh2_ref[0]


def _mixers(x, o, zg, zc, g1, sh2, sc2, prm, *, tile, alpha):
    B, L, D = x.shape
    nt = L // tile
    hb = tile // CONV_HALO
    nhb = L // CONV_HALO
    gw2, cw2 = zg.shape[-1], zc.shape[-1]
    tmap = lambda b, i: (b, i, 0)
    bmap = lambda b, i: (b, 0, 0)
    c2 = lambda b, i: (0, 0)
    c3 = lambda b, i: (0, 0, 0)
    in_specs = [pl.BlockSpec((1, tile, D), tmap),
                pl.BlockSpec((1, tile, o.shape[-1]), tmap),
                pl.BlockSpec((1, tile, gw2), tmap),
                pl.BlockSpec((1, tile, cw2), tmap),
                pl.BlockSpec((1, CONV_HALO, cw2), lambda b, i: (b, jnp.maximum(i * hb - 1, 0), 0)),
                pl.BlockSpec((1, CONV_HALO, cw2), lambda b, i: (b, jnp.minimum((i + 1) * hb, nhb - 1), 0)),
                pl.BlockSpec((1, 1, D), bmap), pl.BlockSpec((1, 1, D), bmap), pl.BlockSpec((1, 1, D), bmap)]
    for a in prm:
        in_specs.append(pl.BlockSpec(a.shape, c2 if a.ndim == 2 else c3))
    return pl.pallas_call(
        functools.partial(_mix_kernel, tile=tile, alpha=alpha),
        grid=(B, nt),
        in_specs=in_specs,
        out_specs=[pl.BlockSpec((1, tile, D), tmap), pl.BlockSpec((1, tile, D), tmap)],
        out_shape=[jax.ShapeDtypeStruct((B, L, D), F32), jax.ShapeDtypeStruct((B, L, D), F32)],
        scratch_shapes=[pltpu.VMEM((tile + 2 * CONV_HALO, cw2 // 2), F32)],
        compiler_params=_cparams(("arbitrary", "arbitrary")),
        name="mixers",
    )(x, o, zg, zc, zc, zc, g1, sh2, sc2, *prm)


def _oddeven_merge_sort_pairs(n):
    pairs = []
    p = 1
    while p < n:
        k = p
        while k >= 1:
            for j in range(k % p, n - k, 2 * k):
                for i in range(min(k, n - j - k)):
                    if (i + j) // (2 * p) == (i + j + k) // (2 * p):
                        pairs.append((i + j, i + j + k))
            k //= 2
        p *= 2
    return pairs


_SORT16 = _oddeven_merge_sort_pairs(16)
_BITONIC16 = [(i, i + d) for d in (8, 4, 2, 1) for i in range(16) if (i & d) == 0]


def _before(a, ia, b, ib):
    return (a > b) | ((a == b) & (ia < ib))


def _cmpx(vals, keys, pay, i, j):
    keep = _before(vals[i], keys[i], vals[j], keys[j])
    vals[i], vals[j] = jnp.where(keep, vals[i], vals[j]), jnp.where(keep, vals[j], vals[i])
    keys[i], keys[j] = jnp.where(keep, keys[i], keys[j]), jnp.where(keep, keys[j], keys[i])
    if pay is not None:
        pay[i], pay[j] = jnp.where(keep, pay[i], pay[j]), jnp.where(keep, pay[j], pay[i])


def _top16_merge(av, ak, ap, bv, bk, bp):
    n = len(av)
    ov, ok, op = [], [], []
    for i in range(n):
        keep = _before(av[i], ak[i], bv[n - 1 - i], bk[n - 1 - i])
        ov.append(jnp.where(keep, av[i], bv[n - 1 - i]))
        ok.append(jnp.where(keep, ak[i], bk[n - 1 - i]))
        if ap is not None:
            op.append(jnp.where(keep, ap[i], bp[n - 1 - i]))
    return ov, ok, (op if ap is not None else None)


def _stage2_candidates():
    return [(a, b) for a in range(PEER_TOPK) for b in range(PEER_TOPK) if (a + 1) * (b + 1) <= PEER_TOPK]


def _retr_kernel(h_ref, wq_ref, keys_ref, gate_ref, eidx_ref, q_s, sv_s, si_s, *, tile):
    q_s[...] = jnp.dot(h_ref[...].astype(BF16), wq_ref[...], preferred_element_type=F32)
    sub = lax.broadcasted_iota(jnp.int32, (8, tile), 0)
    sv_s[...] = jnp.zeros(sv_s.shape, F32)
    si_s[...] = jnp.zeros(si_s.shape, jnp.int32)

    def stage1(hp, _):
        qh = q_s[:, pl.ds(pl.multiple_of(hp * PEER_HALF, PEER_HALF), PEER_HALF)].astype(BF16)
        st = lax.dot_general(keys_ref[hp], qh, (((1,), (1,)), ((), ())),
                             preferred_element_type=F32)
        st = st.reshape(N_KEYS // 8, 8, tile)
        vals = [st[i] for i in range(16)]
        keys = [sub + 8 * i for i in range(16)]
        for (i, j) in _SORT16:
            _cmpx(vals, keys, None, i, j)
        for shift in (4, 2, 1):
            bv = [pltpu.roll(t, shift, axis=0) for t in vals]
            bk = [pltpu.roll(t, shift, axis=0) for t in keys]
            vals, keys, _ = _top16_merge(vals, keys, None, bv, bk, None)
            for (i, j) in _BITONIC16:
                _cmpx(vals, keys, None, i, j)
        head = hp // 2
        half = hp % 2
        for i in range(16):
            sv_s[half, i] = jnp.where(sub == head, vals[i], sv_s[half, i])
            si_s[half, i] = jnp.where(sub == head, keys[i], si_s[half, i])
        return 0

    lax.fori_loop(0, 2 * PEER_HEADS, stage1, 0)

    cands = _stage2_candidates()
    n_pad = 64
    vals, poss, exps = [], [], []
    for (a, b) in cands:
        vals.append(sv_s[0, a] + sv_s[1, b])
        poss.append(jnp.full((8, tile), a * PEER_TOPK + b, jnp.int32))
        exps.append(si_s[0, a] * N_KEYS + si_s[1, b])
    for t in range(n_pad - len(cands)):
        vals.append(jnp.full((8, tile), -jnp.inf, F32))
        poss.append(jnp.full((8, tile), PEER_TOPK * PEER_TOPK + t, jnp.int32))
        exps.append(jnp.zeros((8, tile), jnp.int32))
    groups = []
    for g in range(n_pad // 16):
        gv, gk, gp = vals[16 * g:16 * g + 16], poss[16 * g:16 * g + 16], exps[16 * g:16 * g + 16]
        for (i, j) in _SORT16:
            _cmpx(gv, gk, gp, i, j)
        groups.append((gv, gk, gp))
    while len(groups) > 1:
        nxt = []
        for g in range(0, len(groups), 2):
            mv, mk, mp = _top16_merge(*groups[g], *groups[g + 1])
            if len(groups) > 2:
                for (i, j) in _BITONIC16:
                    _cmpx(mv, mk, mp, i, j)
            nxt.append((mv, mk, mp))
        groups = nxt
    best, _, experts = groups[0]

    mx = functools.reduce(jnp.maximum, best)
    ex = [jnp.exp(b - mx) for b in best]
    inv = 1.0 / functools.reduce(lambda a, b: a + b, ex)
    gate_ref[...] = jnp.concatenate([e * inv for e in ex], axis=0).T
    eidx_ref[...] = jnp.concatenate(experts, axis=0).T


def _retrieval(h2, wq_bf, keys_bf, *, tile):
    N, D = h2.shape
    QW = wq_bf.shape[1]
    return pl.pallas_call(
        functools.partial(_retr_kernel, tile=tile),
        grid=(N // tile,),
        in_specs=[pl.BlockSpec((tile, D), lambda i: (i, 0)),
                  pl.BlockSpec((D, QW), lambda i: (0, 0)),
                  pl.BlockSpec(keys_bf.shape, lambda i: (0, 0, 0))],
        out_specs=[pl.BlockSpec((tile, PEER_TOPK * PEER_HEADS), lambda i: (i, 0)),
                   pl.BlockSpec((tile, PEER_TOPK * PEER_HEADS), lambda i: (i, 0))],
        out_shape=[jax.ShapeDtypeStruct((N, PEER_TOPK * PEER_HEADS), F32),
                   jax.ShapeDtypeStruct((N, PEER_TOPK * PEER_HEADS), jnp.int32)],
        scratch_shapes=[pltpu.VMEM((tile, QW), F32),
                        pltpu.VMEM((2, PEER_TOPK, PEER_HEADS, tile), F32),
                        pltpu.VMEM((2, PEER_TOPK, PEER_HEADS, tile), jnp.int32)],
        compiler_params=_cparams(("arbitrary",)),
        name="retrieval",
    )(h2, wq_bf, keys_bf)


def _expert_token(r, h_row, gate_col):
    u = pltpu.bitcast(r & jnp.uint32(0xFFFF0000), F32)
    v = pltpu.bitcast(r << 16, F32)
    d = jnp.sum(u * h_row, axis=-1, keepdims=True)
    w = jnp.broadcast_to(gate_col * _gelu_tanh(d), r.shape)
    return jnp.sum(v * w, axis=0, keepdims=True)


def _expert_kernel(*refs, gtok, alpha, dense):
    if dense:
        (idx0_ref, idxn_ref, tab_ref, h_ref, gate_ref, x1_ref, g2_ref, rows_ref, hd_ref, gated_ref, x1d_ref, g2d_ref,
         lng_ref, lnb_ref, o_ref, od_ref, buf_a, buf_b, f_ref, fd_ref, sem_ref) = refs
    else:
        (idx0_ref, idxn_ref, tab_ref, h_ref, gate_ref, x1_ref, g2_ref, lng_ref, lnb_ref,
         o_ref, buf_a, buf_b, f_ref, fd_ref, sem_ref) = refs
    i = pl.program_id(0)
    n_steps = pl.num_programs(0)
    per_tok = PEER_HEADS * PEER_TOPK
    d_model = h_ref.shape[-1]

    def start_token(idx_ref, buf, sem_row, j):
        for g in range(per_tok // 8):
            for t in range(8):
                e = idx_ref[0, 0, j * per_tok + g * 8 + t]
                pltpu.make_async_copy(tab_ref.at[e], buf.at[j, g, pl.ds(t, 1), :],
                                      sem_ref.at[sem_row, j]).start(priority=t % 2)

    def wait_token(buf, other, sem_row, j):
        pltpu.make_async_copy(other.at[j], buf.at[j], sem_ref.at[sem_row, j]).wait()

    def token(j, cur, nxt, cur_row):
        wait_token(cur, nxt, cur_row, j)
        start_token(idxn_ref, nxt, 1 - cur_row, j)
        f_ref[j:j + 1, :] = _expert_token(cur[j].reshape(per_tok, d_model), h_ref[j:j + 1, :],
                                          gate_t[:, j:j + 1])
        if dense:
            fd_ref[j:j + 1, :] = _expert_token(rows_ref[j * per_tok:(j + 1) * per_tok, :], hd_ref[j:j + 1, :],
                                               gated_t[:, j:j + 1])

    @pl.when(i == 0)
    def _():
        def prime(j, _):
            start_token(idx0_ref, buf_a, 0, j)
            return 0
        lax.fori_loop(0, gtok, prime, 0)

    gate_t = gate_ref[...].T
    gated_t = gated_ref[...].T if dense else None

    @pl.when(i % 2 == 0)
    def _():
        for j in range(gtok):
            token(j, buf_a, buf_b, 0)

    @pl.when(i % 2 == 1)
    def _():
        for j in range(gtok):
            token(j, buf_b, buf_a, 1)

    @pl.when((i == n_steps - 1) & (i % 2 == 0))
    def _():
        lax.fori_loop(0, gtok, lambda j, c: (wait_token(buf_b, buf_a, 1, j), 0)[1], 0)

    @pl.when((i == n_steps - 1) & (i % 2 == 1))
    def _():
        lax.fori_loop(0, gtok, lambda j, c: (wait_token(buf_a, buf_b, 0, j), 0)[1], 0)

    o_ref[...] = _ln(alpha * x1_ref[...] + g2_ref[0] * f_ref[...]) * lng_ref[...] + lnb_ref[...]

    if dense:
        od_ref[...] = _ln(alpha * x1d_ref[...] + g2d_ref[0] * fd_ref[...]) * lng_ref[...] + lnb_ref[...]


def _experts(eidx, gate, tab, h2, x1, g2, lng, lnb, *, tok0, n_tok, gtok, seq, alpha, dense_rows=None,
             dense_tok0=0):
    D = h2.shape[1]
    S = n_tok // gtok
    off = tok0 // gtok
    doff = dense_tok0 // gtok
    per_tok = PEER_HEADS * PEER_TOPK
    rows = gtok * per_tok
    spb = seq // gtok
    smem = pltpu.MemorySpace.SMEM
    dense = dense_rows is not None
    tok_spec = lambda o: pl.BlockSpec((gtok, D), lambda i: (i + o, 0))
    gate_spec = lambda o: pl.BlockSpec((gtok, per_tok), lambda i: (i + o, 0))
    g2_spec = lambda o: pl.BlockSpec((1, 1, D), lambda i: ((i + o) // spb, 0, 0))
    row_spec = pl.BlockSpec((1, D), lambda i: (0, 0))
    in_specs = [pl.BlockSpec((1, 1, rows), lambda i: (off, 0, 0), memory_space=smem),
                pl.BlockSpec((1, 1, rows), lambda i: (jnp.minimum(i + 1, S - 1) + off, 0, 0), memory_space=smem),
                pl.BlockSpec(memory_space=pl.ANY),
                tok_spec(off), gate_spec(off), tok_spec(off), g2_spec(off)]
    args = [eidx, eidx, tab, h2, gate, x1, g2]
    out_spec = pl.BlockSpec((gtok, D), lambda i: (i, 0))
    out_shape = jax.ShapeDtypeStruct((n_tok, D), F32)
    if dense:
        in_specs += [pl.BlockSpec((rows, D), lambda i: (i, 0)), tok_spec(doff), gate_spec(doff), tok_spec(doff),
                     g2_spec(doff)]
        args += [dense_rows, h2, gate, x1, g2]
    return pl.pallas_call(
        functools.partial(_expert_kernel, gtok=gtok, alpha=alpha, dense=dense),
        grid=(S,),
        in_specs=in_specs + [row_spec, row_spec],
        out_specs=[out_spec, out_spec] if dense else out_spec,
        out_shape=[out_shape, out_shape] if dense else out_shape,
        scratch_shapes=[pltpu.VMEM((gtok, per_tok // 8, 8, D), jnp.uint32),
                        pltpu.VMEM((gtok, per_tok // 8, 8, D), jnp.uint32),
                        pltpu.VMEM((gtok, D), F32),
                        pltpu.VMEM((gtok, D), F32),
                        pltpu.SemaphoreType.DMA((2, gtok))],
        compiler_params=_cparams(("arbitrary",)),
        name="experts",
    )(*args, lng, lnb)


def _expert_dense_kernel(rows_ref, h_ref, gate_ref, x1_ref, g2_ref, lng_ref, lnb_ref, o_ref, f_ref, *, gtok, alpha):
    per_tok = PEER_HEADS * PEER_TOPK
    gate_t = gate_ref[...].T
    for j in range(gtok):
        f_ref[j:j + 1, :] = _expert_token(rows_ref[j * per_tok:(j + 1) * per_tok, :], h_ref[j:j + 1, :],
                                          gate_t[:, j:j + 1])
    o_ref[...] = _ln(alpha * x1_ref[...] + g2_ref[0] * f_ref[...]) * lng_ref[...] + lnb_ref[...]


def _experts_dense(rows, gate, h2, x1, g2, lng, lnb, *, tok0, n_tok, row_tok0, gtok, seq, alpha):
    D = h2.shape[1]
    per_tok = PEER_HEADS * PEER_TOPK
    off = tok0 // gtok
    steps_per_batch = seq // gtok
    return pl.pallas_call(
        functools.partial(_expert_dense_kernel, gtok=gtok, alpha=alpha),
        grid=(n_tok // gtok,),
        in_specs=[pl.BlockSpec((gtok * per_tok, D), lambda i: (i + row_tok0 // gtok, 0)),
                  pl.BlockSpec((gtok, D), lambda i: (i + off, 0)),
                  pl.BlockSpec((gtok, per_tok), lambda i: (i + off, 0)),
                  pl.BlockSpec((gtok, D), lambda i: (i + off, 0)),
                  pl.BlockSpec((1, 1, D), lambda i: ((i + off) // steps_per_batch, 0, 0)),
                  pl.BlockSpec((1, D), lambda i: (0, 0)),
                  pl.BlockSpec((1, D), lambda i: (0, 0))],
        out_specs=pl.BlockSpec((gtok, D), lambda i: (i, 0)),
        out_shape=jax.ShapeDtypeStruct((n_tok, D), F32),
        scratch_shapes=[pltpu.VMEM((gtok, D), F32)],
        compiler_params=_cparams(("arbitrary",)),
        name="experts_dense",
    )(rows, h2, gate, x1, g2, lng, lnb)


SC_WORKERS = 32
SC_IDX_BLOCK = 128
SC_CHUNK = 32
SC_SHARE_DEN = 16
SC_SHARE_FIRST = 5
SC_SHARE_LATER = 6


def _sc_gather_rows(tab, idx2d, row0, n_rows):
    from jax.experimental.pallas import tpu_sc as plsc
    D = tab.shape[1]
    steps = n_rows // SC_WORKERS
    n_chunks = SC_IDX_BLOCK // SC_CHUNK
    mesh = plsc.VectorSubcoreMesh(core_axis_name="core", subcore_axis_name="subcore")

    @pl.kernel(out_type=jax.ShapeDtypeStruct((n_rows * SC_IDX_BLOCK, D), tab.dtype), mesh=mesh,
               scratch_types=[pltpu.VMEM((1, SC_IDX_BLOCK), jnp.int32),
                              pltpu.VMEM((2, SC_CHUNK, D), tab.dtype),
                              pltpu.SemaphoreType.DMA((2,))])
    def k(x_hbm, i_hbm, o_hbm, i_vmem, buf, wsem):
        wid = lax.axis_index("core") * (SC_WORKERS // 2) + lax.axis_index("subcore")

        def write(row, c):
            return pltpu.make_async_copy(buf.at[c % 2], o_hbm.at[pl.ds(row * SC_IDX_BLOCK + c * SC_CHUNK, SC_CHUNK)],
                                         wsem.at[c % 2])

        @pl.loop(0, steps)
        def _(st):
            row = wid * steps + st
            pltpu.sync_copy(i_hbm.at[pl.ds(row0 + row, 1)], i_vmem)
            for c in range(n_chunks):
                if c >= 2:
                    write(row, c - 2).wait()
                pltpu.sync_copy(x_hbm.at[i_vmem.at[0, pl.ds(c * SC_CHUNK, SC_CHUNK)]], buf.at[c % 2])
                write(row, c).start()
            for c in range(n_chunks - 2, n_chunks):
                write(row, c).wait()
    return k(tab, idx2d)


def _rope_tables(L):
    half = HEAD_DIM // 4
    t = np.arange(L)
    freqs = ROPE_THETA ** (-np.arange(half, dtype=np.float32) / half)
    lane = np.arange(HEAD_DIM)
    pos = np.where(lane[None, :] < HEAD_DIM // 2, (t // GRID_W)[:, None], (t % GRID_W)[:, None]).astype(np.float32)
    ang = pos * freqs[lane % half][None, :].astype(np.float32)
    cos, sin = np.cos(ang), np.sin(ang)
    first = ((lane % (2 * half)) < half)[None, :]
    slo = np.where(first, -sin, 0.0)
    shi = np.where(first, 0.0, sin)
    tile = lambda a: jnp.asarray(np.tile(a, (1, N_KV_HEADS)), F32)
    return tile(cos), tile(slo), tile(shi)


def _pack_tables(u, v):
    ub = lax.bitcast_convert_type(u.astype(BF16), jnp.uint16).astype(jnp.uint32)
    vb = lax.bitcast_convert_type(v.astype(BF16), jnp.uint16).astype(jnp.uint32)
    return (ub << 16) | vb


def _pick(n, prefs):
    for p in prefs:
        if n % p == 0:
            return p
    return n


def kernel(x, c, ctx, c_ctx, w_mod, b_mod, w_in, q_norm_g, k_norm_g, gmlp_ln_g, gmlp_ln_b, gmlp_w_s, gmlp_b_s,
           conv_w_dw, conv_b_dw, conv_ln_g, conv_ln_b, conv_w_pw, conv_b_pw, w_out, ln1_g, ln1_b, ln2_g, ln2_b,
           peer_w_q, peer_sub_keys, peer_u, peer_v):
    B, L, D = x.shape
    LC = ctx.shape[1]
    depth = w_mod.shape[0]
    alpha = (2 * depth) ** 0.25
    gw = gmlp_ln_g.shape[-1]

    cos, slo, shi = _rope_tables(L)
    ones_c = jnp.ones((LC, KV_WIDTH), F32)
    zeros_c = jnp.zeros((LC, KV_WIDTH), F32)
    gidx = np.arange(ATTN_WIDTH) // HEAD_DIM
    gmat = jnp.asarray((gidx[:, None] == gidx[None, :]) / HEAD_DIM, BF16)

    cond = jnp.concatenate([c, c_ctx[None, :], jnp.zeros((-(B + 1) % 8, D), F32)], axis=0)

    t_in = _pick(L, (512, 256, 128))
    t_mix = _pick(L, (256, 128))
    t_inc = _pick(LC, (512, 256, 128))
    t_mixc = _pick(LC, (256, 128))
    tk_lat = _pick(L + LC, (768, 512, 384, 256, 128))
    tk_ctx = _pick(LC, (512, 256, 128))
    gtok = 16

    def retrieve(h2, wq_bf, keys_bf):
        n = h2.shape[0] * h2.shape[1]
        return _retrieval(h2.reshape(n, D), wq_bf, keys_bf, tile=_pick(n, (512, 256, 128)))

    def peer(h2, x1, g2, gate, eidx, l, seq, tab, n_sc):
        n = h2.shape[0] * h2.shape[1]
        h2f, x1f = h2.reshape(n, D), x1.reshape(n, D)
        lng, lnb = ln2_g[l].reshape(1, D), ln2_b[l].reshape(1, D)
        eidx3 = eidx.reshape(n // gtok, 1, -1)
        common = dict(gtok=gtok, seq=seq, alpha=alpha)
        args = (gate, tab, h2f, x1f, g2, lng, lnb)
        if n_sc == 0:
            return _experts(eidx3, *args, tok0=0, n_tok=n, **common).reshape(h2.shape)
        n_self = n - n_sc
        rows = _sc_gather_rows(tab.reshape(-1, D), eidx, n_self, n_sc)
        m = min(n_self, n_sc)
        both = _experts(eidx3, *args, tok0=0, n_tok=m, dense_rows=rows, dense_tok0=n_self, **common)
        outs_self, outs_sc = [both[0]], [both[1]]
        if n_self > m:
            outs_self.append(_experts(eidx3, *args, tok0=m, n_tok=n_self - m, **common))
        if n_sc > m:
            outs_sc.append(_experts_dense(rows, gate, h2f, x1f, g2, lng, lnb, tok0=n_self + m, n_tok=n_sc - m,
                                          row_tok0=m, **common))
        return jnp.concatenate(outs_self + outs_sc, axis=0).reshape(h2.shape)

    def sc_share(n, num):
        align = SC_WORKERS * gtok
        return n * num // SC_SHARE_DEN // align * align if n >= 8192 else 0

    n_groups = 2 if B % 2 == 0 else 1
    bg = B // n_groups
    groups = [slice(g * bg, (g + 1) * bg) for g in range(n_groups)]
    x_grp = [x[bs] for bs in groups]
    x_ctx = ctx
    for l in range(depth):
        last = l == depth - 1
        mod = _modulation(cond, w_mod[l], b_mod[l])
        m_lat = [mod[:B, k * D:(k + 1) * D].reshape(B, 1, D) for k in range(6)]
        m_ctx = [jnp.broadcast_to(mod[B:B + 1, k * D:(k + 1) * D].reshape(1, 1, D), (B, 1, D)) for k in range(6)]

        w_in_bf = w_in[l].astype(BF16)
        gq = jnp.tile(q_norm_g[l], N_Q_HEADS).reshape(1, ATTN_WIDTH)
        gk = jnp.tile(k_norm_g[l], N_KV_HEADS).reshape(1, KV_WIDTH)
        mix_prm = (gmlp_ln_g[l].reshape(1, gw), gmlp_ln_b[l].reshape(1, gw),
                   gmlp_w_s[l].astype(BF16),
                   jnp.repeat(gmlp_b_s[l].T, gw // GMLP_GROUPS, axis=1),
                   conv_w_dw[l], conv_b_dw[l].reshape(1, -1), conv_ln_g[l].reshape(1, -1),
                   conv_ln_b[l].reshape(1, -1), conv_w_pw[l].astype(BF16), conv_b_pw[l].reshape(1, -1),
                   w_out[l].astype(BF16), ln1_g[l].reshape(1, D), ln1_b[l].reshape(1, D))
        wq_bf = peer_w_q[l].astype(BF16)
        keys_bf = peer_sub_keys[l].reshape(2 * PEER_HEADS, N_KEYS, PEER_HALF).astype(BF16)
        tab = _pack_tables(peer_u[l], peer_v[l]).reshape(-1, 1, D)

        q_c, kt_c, v_c, zg_c, zc_c = _inproj(x_ctx, m_ctx[0], m_ctx[1], w_in_bf, gq, gk, ones_c, zeros_c, zeros_c,
                                              gmat, tile=t_inc)
        staged = []
        for g, bs in enumerate(groups):
            xg = x_grp[g]
            q_l, kt_l, v_l, zg_l, zc_l = _inproj(xg, m_lat[0][bs], m_lat[1][bs], w_in_bf, gq, gk, cos, slo, shi, gmat,
                                                  tile=t_in)
            kt_all = jnp.concatenate([kt_c[bs], kt_l], axis=3)
            v_all = jnp.concatenate([v_c[bs], v_l], axis=1)
            o_lat = _attention(q_l, kt_all, v_all, tq=128, tk=tk_lat)
            x1, h2 = _mixers(xg, o_lat, zg_l, zc_l, m_lat[2][bs], m_lat[3][bs], m_lat[4][bs], mix_prm, tile=t_mix,
                             alpha=alpha)
            staged.append((h2, x1) + tuple(retrieve(h2, wq_bf, keys_bf)))
        for g, bs in enumerate(groups):
            h2, x1, gate, eidx = staged[g]
            share = SC_SHARE_FIRST if g == 0 and n_groups > 1 else SC_SHARE_LATER
            x_grp[g] = peer(h2, x1, m_lat[5][bs], gate, eidx, l, L, tab, sc_share(bg * L, share))

        if not last:
            o_ctx = _attention(q_c, kt_c, v_c, tq=128, tk=tk_ctx)
            x1c, h2c = _mixers(x_ctx, o_ctx, zg_c, zc_c, m_ctx[2], m_ctx[3], m_ctx[4], mix_prm, tile=t_mixc,
                               alpha=alpha)
            gate_c, eidx_c = retrieve(h2c, wq_bf, keys_bf)
            x_ctx = peer(h2c, x1c, m_ctx[5], gate_c, eidx_c, l, LC, tab, 0)
    return jnp.concatenate(x_grp, axis=0)
```

```python
import functools
import math

import jax
import jax.numpy as jnp
import numpy as np
from jax import lax
from jax.experimental import pallas as pl
from jax.experimental.pallas import tpu as pltpu

F32 = jnp.float32
BF16 = jnp.bfloat16

EPS = 1e-6
GRID_W = 64
N_Q_HEADS = 8
N_KV_HEADS = 2
Q_PER_KV = N_Q_HEADS // N_KV_HEADS
HEAD_DIM = 64
ATTN_WIDTH = N_Q_HEADS * HEAD_DIM
KV_WIDTH = N_KV_HEADS * HEAD_DIM
QKV_WIDTH = ATTN_WIDTH + 2 * KV_WIDTH
ROPE_THETA = 10000.0
CHUNK = 128
GMLP_GROUPS = 4
CONV_K = 31
CONV_HALO = 16
PEER_HEADS = 8
N_KEYS = 128
PEER_TOPK = 16
PEER_HALF = 128

VMEM_LIMIT = 56 * 1024 * 1024


def _cparams(sem):
    return pltpu.CompilerParams(dimension_semantics=sem, vmem_limit_bytes=VMEM_LIMIT)


def _gelu_tanh(x):
    return 0.5 * x * (1.0 + jnp.tanh(math.sqrt(2.0 / math.pi) * (x + 0.044715 * (x * x * x))))


def _ln(x):
    mu = jnp.mean(x, axis=-1, keepdims=True)
    xc = x - mu
    var = jnp.mean(xc * xc, axis=-1, keepdims=True)
    return xc * lax.rsqrt(var + EPS)


def _mod_kernel(c_ref, w_ref, b_ref, o_ref):
    c = c_ref[...]
    s = c * jax.nn.sigmoid(c)
    o_ref[...] = jnp.dot(s, w_ref[...], preferred_element_type=F32,
                         precision=lax.Precision.HIGHEST) + b_ref[...]


def _modulation(cond, w_mod, b_mod):
    R, D = cond.shape
    N = w_mod.shape[1]
    tn = 1024
    return pl.pallas_call(
        _mod_kernel,
        grid=(N // tn,),
        in_specs=[pl.BlockSpec((R, D), lambda j: (0, 0)),
                  pl.BlockSpec((D, tn), lambda j: (0, j)),
                  pl.BlockSpec((1, tn), lambda j: (0, j))],
        out_specs=pl.BlockSpec((R, tn), lambda j: (0, j)),
        out_shape=jax.ShapeDtypeStruct((R, N), F32),
        compiler_params=_cparams(("arbitrary",)),
        name="modulation",
    )(cond, w_mod, b_mod.reshape(1, N))


def _group_mean_sq(t, gmat):
    sq = t * t
    hi = sq.astype(BF16)
    lo = (sq - hi.astype(F32)).astype(BF16)
    return (jnp.dot(hi, gmat, preferred_element_type=F32)
            + jnp.dot(lo, gmat, preferred_element_type=F32))


def _rope(t, cos, sin_lo, sin_hi):
    w = t.shape[-1]
    up = pltpu.roll(t, w - 16, axis=1)
    dn = pltpu.roll(t, 16, axis=1)
    return t * cos + up * sin_lo + dn * sin_hi


def _inproj_kernel(x_ref, sh_ref, sc_ref, w_ref, gq_ref, gk_ref, cos_ref, slo_ref, shi_ref, gmat_ref,
                   q_ref, kt_ref, v_ref, zg_ref, zc_ref):
    h = _ln(x_ref[0]) * (1.0 + sc_ref[0]) + sh_ref[0]
    p = jnp.dot(h.astype(BF16), w_ref[...], preferred_element_type=F32)
    q = p[:, :ATTN_WIDTH]
    k = p[:, ATTN_WIDTH:ATTN_WIDTH + KV_WIDTH]
    v = p[:, ATTN_WIDTH + KV_WIDTH:QKV_WIDTH]
    gm_w = (p.shape[1] - QKV_WIDTH) // 2
    zg_ref[0] = p[:, QKV_WIDTH:QKV_WIDTH + gm_w]
    zc_ref[0] = p[:, QKV_WIDTH + gm_w:]
    v_ref[0] = v.astype(BF16)

    gmat = gmat_ref[...]
    cos, slo, shi = cos_ref[...], slo_ref[...], shi_ref[...]
    qn = q * lax.rsqrt(_group_mean_sq(q, gmat) + EPS) * gq_ref[...]
    rep = ATTN_WIDTH // KV_WIDTH
    qr = _rope(qn, jnp.concatenate([cos] * rep, axis=1), jnp.concatenate([slo] * rep, axis=1),
               jnp.concatenate([shi] * rep, axis=1))
    qr = (qr * (HEAD_DIM ** -0.5 * math.log2(math.e))).astype(BF16)
    for hh in range(N_Q_HEADS):
        q_ref[0, hh] = qr[:, hh * HEAD_DIM:(hh + 1) * HEAD_DIM]

    kn = k * lax.rsqrt(_group_mean_sq(k, gmat[:KV_WIDTH, :KV_WIDTH]) + EPS) * gk_ref[...]
    kt = _rope(kn, cos, slo, shi).T
    for g in range(N_KV_HEADS):
        kt_ref[0, g] = kt[g * HEAD_DIM:(g + 1) * HEAD_DIM, :].astype(BF16)


def _inproj(x, sh, sc, w_in_bf, gq, gk, cos, slo, shi, gmat, *, tile):
    B, L, D = x.shape
    W = w_in_bf.shape[1]
    gm_w = (W - QKV_WIDTH) // 2
    nt = L // tile
    bmap = lambda b, i: (b, 0, 0)
    cmap = lambda b, i: (0, 0)
    return pl.pallas_call(
        _inproj_kernel,
        grid=(B, nt),
        in_specs=[pl.BlockSpec((1, tile, D), lambda b, i: (b, i, 0)),
                  pl.BlockSpec((1, 1, D), bmap), pl.BlockSpec((1, 1, D), bmap),
                  pl.BlockSpec((D, W), cmap),
                  pl.BlockSpec((1, ATTN_WIDTH), cmap), pl.BlockSpec((1, KV_WIDTH), cmap),
                  pl.BlockSpec((tile, KV_WIDTH), lambda b, i: (i, 0)),
                  pl.BlockSpec((tile, KV_WIDTH), lambda b, i: (i, 0)),
                  pl.BlockSpec((tile, KV_WIDTH), lambda b, i: (i, 0)),
                  pl.BlockSpec((ATTN_WIDTH, ATTN_WIDTH), cmap)],
        out_specs=[pl.BlockSpec((1, N_Q_HEADS, tile, HEAD_DIM), lambda b, i: (b, 0, i, 0)),
                   pl.BlockSpec((1, N_KV_HEADS, HEAD_DIM, tile), lambda b, i: (b, 0, 0, i)),
                   pl.BlockSpec((1, tile, KV_WIDTH), lambda b, i: (b, i, 0)),
                   pl.BlockSpec((1, tile, gm_w), lambda b, i: (b, i, 0)),
                   pl.BlockSpec((1, tile, gm_w), lambda b, i: (b, i, 0))],
        out_shape=[jax.ShapeDtypeStruct((B, N_Q_HEADS, L, HEAD_DIM), BF16),
                   jax.ShapeDtypeStruct((B, N_KV_HEADS, HEAD_DIM, L), BF16),
                   jax.ShapeDtypeStruct((B, L, KV_WIDTH), BF16),
                   jax.ShapeDtypeStruct((B, L, gm_w), F32),
                   jax.ShapeDtypeStruct((B, L, gm_w), F32)],
        compiler_params=_cparams(("arbitrary", "arbitrary")),
        name="inproj",
    )(x, sh, sc, w_in_bf, gq, gk, cos, slo, shi, gmat)


def _attn_kernel(q_ref, kt_ref, vx_ref, o_ref, p_ref, *, tq, tk, nk):
    m_rows = Q_PER_KV * tq
    qs = [q_ref[0, Q_PER_KV * g:Q_PER_KV * (g + 1)].reshape(m_rows, HEAD_DIM) for g in range(N_KV_HEADS)]

    def pv(c, par, g):
        off = pl.multiple_of(c * tk, tk)
        return jnp.dot(p_ref[par, g], vx_ref[0, g, pl.ds(off, tk), :], preferred_element_type=F32)

    def step(c, par, carry):
        off = pl.multiple_of(c * tk, tk)
        out = []
        for g in range(N_KV_HEADS):
            m, acc = carry[g]
            s = jnp.dot(qs[g], kt_ref[0, g, :, pl.ds(off, tk)], preferred_element_type=F32)
            m_new = jnp.maximum(m, jnp.max(s, axis=-1, keepdims=True))
            p_ref[1 - par, g] = jnp.exp2(s - m_new).astype(BF16)
            if acc is not None:
                acc = jnp.exp2(m - m_new) * (acc + pv(c - 1, par, g))
            out.append((m_new, acc))
        return tuple(out)

    first = step(0, 1, tuple((jnp.full((m_rows, 1), -1e30, F32), None) for _ in range(N_KV_HEADS)))
    carry = tuple((m, jnp.zeros((m_rows, 2 * HEAD_DIM), F32)) for m, _ in first)

    n_pairs = (nk - 1) // 2
    def pair(t, carry):
        carry = step(2 * t + 1, 0, carry)
        return step(2 * t + 2, 1, carry)
    carry = lax.fori_loop(0, n_pairs, pair, carry)
    par = 0
    if (nk - 1) % 2:
        carry = step(nk - 1, 0, carry)
        par = 1

    for g in range(N_KV_HEADS):
        acc = carry[g][1] + pv(nk - 1, par, g)
        og = acc / pltpu.roll(acc, HEAD_DIM, axis=1)
        for r in range(Q_PER_KV):
            hh = Q_PER_KV * g + r
            o_ref[0, :, hh * HEAD_DIM:(hh + 1) * HEAD_DIM] = og[r * tq:(r + 1) * tq, :HEAD_DIM].astype(o_ref.dtype)


def _attention(q, kt, v, *, tq, tk):
    B, _, L, _ = q.shape
    LK = kt.shape[-1]
    ones = jnp.ones((B, LK, HEAD_DIM), v.dtype)
    vx = jnp.stack([jnp.concatenate([v[..., g * HEAD_DIM:(g + 1) * HEAD_DIM], ones], axis=-1)
                    for g in range(N_KV_HEADS)], axis=1)
    return pl.pallas_call(
        functools.partial(_attn_kernel, tq=tq, tk=tk, nk=LK // tk),
        grid=(B, L // tq),
        in_specs=[pl.BlockSpec((1, N_Q_HEADS, tq, HEAD_DIM), lambda b, i: (b, 0, i, 0)),
                  pl.BlockSpec((1, N_KV_HEADS, HEAD_DIM, LK), lambda b, i: (b, 0, 0, 0)),
                  pl.BlockSpec((1, N_KV_HEADS, LK, 2 * HEAD_DIM), lambda b, i: (b, 0, 0, 0))],
        out_specs=pl.BlockSpec((1, tq, ATTN_WIDTH), lambda b, i: (b, i, 0)),
        out_shape=jax.ShapeDtypeStruct((B, L, ATTN_WIDTH), BF16),
        scratch_shapes=[pltpu.VMEM((2, N_KV_HEADS, Q_PER_KV * tq, tk), BF16)],
        compiler_params=_cparams(("arbitrary", "arbitrary")),
        name="attention",
    )(q, kt, vx)


def _mix_kernel(x_ref, o_ref, zg_ref, zc_ref, zp_ref, zn_ref, g1_ref, sh2_ref, sc2_ref,
                gln_g_ref, gln_b_ref, ws_ref, bs_ref, wdw_ref, bdw_ref, cln_g_ref, cln_b_ref,
                wpw_ref, bpw_ref, wout_ref, ln1g_ref, ln1b_ref,
                x1_ref, h2_ref, ybuf_ref, *, tile, alpha):
    i = pl.program_id(1)
    nt = pl.num_programs(1)
    gw = zg_ref.shape[-1] // 2
    cw = zc_ref.shape[-1] // 2

    z = _gelu_tanh(zg_ref[0])
    u, v = z[:, :gw], z[:, gw:]
    vn = (_ln(v) * gln_g_ref[...] + gln_b_ref[...]).astype(BF16)
    grp = lax.broadcasted_iota(jnp.int32, (CHUNK, gw), 1) // (gw // GMLP_GROUPS)
    sg = []
    for n in range(tile // CHUNK):
        vc = vn[n * CHUNK:(n + 1) * CHUNK]
        s = jnp.zeros((CHUNK, gw), F32)
        for g in range(GMLP_GROUPS):
            s = jnp.where(grp == g, jnp.dot(ws_ref[g], vc, preferred_element_type=F32), s)
        sg.append(s + bs_ref[...])
    y_g = u * jnp.concatenate(sg, axis=0)

    def glu(zz):
        return zz[:, :cw] * jax.nn.sigmoid(zz[:, cw:])

    ybuf_ref[0:CONV_HALO] = glu(zp_ref[0]) * (i > 0).astype(F32)
    ybuf_ref[CONV_HALO:CONV_HALO + tile] = glu(zc_ref[0])
    ybuf_ref[CONV_HALO + tile:2 * CONV_HALO + tile] = glu(zn_ref[0]) * (i < nt - 1).astype(F32)
    base = CONV_HALO - CONV_K // 2
    conv = jnp.zeros((tile, cw), F32)
    for kk in range(CONV_K):
        conv = conv + ybuf_ref[base + kk:base + kk + tile, :] * wdw_ref[kk:kk + 1, :]
    conv = conv + bdw_ref[...]
    t = _ln(conv) * cln_g_ref[...] + cln_b_ref[...]
    t = t * jax.nn.sigmoid(t)
    y_c = jnp.dot(t.astype(BF16), wpw_ref[...], preferred_element_type=F32) + bpw_ref[...]

    aw = o_ref.shape[-1]
    y = (jnp.dot(o_ref[0], wout_ref[0:aw, :], preferred_element_type=F32)
         + jnp.dot(y_g.astype(BF16), wout_ref[aw:aw + gw, :], preferred_element_type=F32)
         + jnp.dot(y_c.astype(BF16), wout_ref[aw + gw:, :], preferred_element_type=F32))
    x1 = _ln(alpha * x_ref[0] + g1_ref[0] * y) * ln1g_ref[...] + ln1b_ref[...]
    x1_ref[0] = x1
    h2_ref[0] = _ln(x1) * (1.0 + sc2_ref[0]) + sh2_ref[0]


def _mixers(x, o, zg, zc, g1, sh2, sc2, prm, *, tile, alpha):
    B, L, D = x.shape
    nt = L // tile
    hb = tile // CONV_HALO
    nhb = L // CONV_HALO
    gw2, cw2 = zg.shape[-1], zc.shape[-1]
    tmap = lambda b, i: (b, i, 0)
    bmap = lambda b, i: (b, 0, 0)
    c2 = lambda b, i: (0, 0)
    c3 = lambda b, i: (0, 0, 0)
    in_specs = [pl.BlockSpec((1, tile, D), tmap),
                pl.BlockSpec((1, tile, o.shape[-1]), tmap),
                pl.BlockSpec((1, tile, gw2), tmap),
                pl.BlockSpec((1, tile, cw2), tmap),
                pl.BlockSpec((1, CONV_HALO, cw2), lambda b, i: (b, jnp.maximum(i * hb - 1, 0), 0)),
                pl.BlockSpec((1, CONV_HALO, cw2), lambda b, i: (b, jnp.minimum((i + 1) * hb, nhb - 1), 0)),
                pl.BlockSpec((1, 1, D), bmap), pl.BlockSpec((1, 1, D), bmap), pl.BlockSpec((1, 1, D), bmap)]
    for a in prm:
        in_specs.append(pl.BlockSpec(a.shape, c2 if a.ndim == 2 else c3))
    return pl.pallas_call(
        functools.partial(_mix_kernel, tile=tile, alpha=alpha),
        grid=(B, nt),
        in_specs=in_specs,
        out_specs=[pl.BlockSpec((1, tile, D), tmap), pl.BlockSpec((1, tile, D), tmap)],
        out_shape=[jax.ShapeDtypeStruct((B, L, D), F32), jax.ShapeDtypeStruct((B, L, D), F32)],
        scratch_shapes=[pltpu.VMEM((tile + 2 * CONV_HALO, cw2 // 2), F32)],
        compiler_params=_cparams(("arbitrary", "arbitrary")),
        name="mixers",
    )(x, o, zg, zc, zc, zc, g1, sh2, sc2, *prm)


def _oddeven_merge_sort_pairs(n):
    pairs = []
    p = 1
    while p < n:
        k = p
        while k >= 1:
            for j in range(k % p, n - k, 2 * k):
                for i in range(min(k, n - j - k)):
                    if (i + j) // (2 * p) == (i + j + k) // (2 * p):
                        pairs.append((i + j, i + j + k))
            k //= 2
        p *= 2
    return pairs


_SORT16 = _oddeven_merge_sort_pairs(16)
_BITONIC16 = [(i, i + d) for d in (8, 4, 2, 1) for i in range(16) if (i & d) == 0]


def _before(a, ia, b, ib):
    return (a > b) | ((a == b) & (ia < ib))


def _cmpx(vals, keys, pay, i, j):
    keep = _before(vals[i], keys[i], vals[j], keys[j])
    vals[i], vals[j] = jnp.where(keep, vals[i], vals[j]), jnp.where(keep, vals[j], vals[i])
    keys[i], keys[j] = jnp.where(keep, keys[i], keys[j]), jnp.where(keep, keys[j], keys[i])
    if pay is not None:
        pay[i], pay[j] = jnp.where(keep, pay[i], pay[j]), jnp.where(keep, pay[j], pay[i])


def _top16_merge(av, ak, ap, bv, bk, bp):
    n = len(av)
    ov, ok, op = [], [], []
    for i in range(n):
        keep = _before(av[i], ak[i], bv[n - 1 - i], bk[n - 1 - i])
        ov.append(jnp.where(keep, av[i], bv[n - 1 - i]))
        ok.append(jnp.where(keep, ak[i], bk[n - 1 - i]))
        if ap is not None:
            op.append(jnp.where(keep, ap[i], bp[n - 1 - i]))
    return ov, ok, (op if ap is not None else None)


def _stage2_candidates():
    return [(a, b) for a in range(PEER_TOPK) for b in range(PEER_TOPK) if (a + 1) * (b + 1) <= PEER_TOPK]


def _retr_kernel(h_ref, wq_ref, keys_ref, gate_ref, eidx_ref, q_s, sv_s, si_s, *, tile):
    q_s[...] = jnp.dot(h_ref[...].astype(BF16), wq_ref[...], preferred_element_type=F32)
    sub = lax.broadcasted_iota(jnp.int32, (8, tile), 0)
    sv_s[...] = jnp.zeros(sv_s.shape, F32)
    si_s[...] = jnp.zeros(si_s.shape, jnp.int32)

    def stage1(hp, _):
        qh = q_s[:, pl.ds(pl.multiple_of(hp * PEER_HALF, PEER_HALF), PEER_HALF)].astype(BF16)
        st = lax.dot_general(keys_ref[hp], qh, (((1,), (1,)), ((), ())),
                             preferred_element_type=F32)
        st = st.reshape(N_KEYS // 8, 8, tile)
        vals = [st[i] for i in range(16)]
        keys = [sub + 8 * i for i in range(16)]
        for (i, j) in _SORT16:
            _cmpx(vals, keys, None, i, j)
        for shift in (4, 2, 1):
            bv = [pltpu.roll(t, shift, axis=0) for t in vals]
            bk = [pltpu.roll(t, shift, axis=0) for t in keys]
            vals, keys, _ = _top16_merge(vals, keys, None, bv, bk, None)
            for (i, j) in _BITONIC16:
                _cmpx(vals, keys, None, i, j)
        head = hp // 2
        half = hp % 2
        for i in range(16):
            sv_s[half, i] = jnp.where(sub == head, vals[i], sv_s[half, i])
            si_s[half, i] = jnp.where(sub == head, keys[i], si_s[half, i])
        return 0

    lax.fori_loop(0, 2 * PEER_HEADS, stage1, 0)

    cands = _stage2_candidates()
    n_pad = 64
    vals, poss, exps = [], [], []
    for (a, b) in cands:
        vals.append(sv_s[0, a] + sv_s[1, b])
        poss.append(jnp.full((8, tile), a * PEER_TOPK + b, jnp.int32))
        exps.append(si_s[0, a] * N_KEYS + si_s[1, b])
    for t in range(n_pad - len(cands)):
        vals.append(jnp.full((8, tile), -jnp.inf, F32))
        poss.append(jnp.full((8, tile), PEER_TOPK * PEER_TOPK + t, jnp.int32))
        exps.append(jnp.zeros((8, tile), jnp.int32))
    groups = []
    for g in range(n_pad // 16):
        gv, gk, gp = vals[16 * g:16 * g + 16], poss[16 * g:16 * g + 16], exps[16 * g:16 * g + 16]
        for (i, j) in _SORT16:
            _cmpx(gv, gk, gp, i, j)
        groups.append((gv, gk, gp))
    while len(groups) > 1:
        nxt = []
        for g in range(0, len(groups), 2):
            mv, mk, mp = _top16_merge(*groups[g], *groups[g + 1])
            if len(groups) > 2:
                for (i, j) in _BITONIC16:
                    _cmpx(mv, mk, mp, i, j)
            nxt.append((mv, mk, mp))
        groups = nxt
    best, _, experts = groups[0]

    mx = functools.reduce(jnp.maximum, best)
    ex = [jnp.exp(b - mx) for b in best]
    inv = 1.0 / functools.reduce(lambda a, b: a + b, ex)
    gate_ref[...] = jnp.concatenate([e * inv for e in ex], axis=0).T
    eidx_ref[...] = jnp.concatenate(experts, axis=0).T


def _retrieval(h2, wq_bf, keys_bf, *, tile):
    N, D = h2.shape
    QW = wq_bf.shape[1]
    return pl.pallas_call(
        functools.partial(_retr_kernel, tile=tile),
        grid=(N // tile,),
        in_specs=[pl.BlockSpec((tile, D), lambda i: (i, 0)),
                  pl.BlockSpec((D, QW), lambda i: (0, 0)),
                  pl.BlockSpec(keys_bf.shape, lambda i: (0, 0, 0))],
        out_specs=[pl.BlockSpec((tile, PEER_TOPK * PEER_HEADS), lambda i: (i, 0)),
                   pl.BlockSpec((tile, PEER_TOPK * PEER_HEADS), lambda i: (i, 0))],
        out_shape=[jax.ShapeDtypeStruct((N, PEER_TOPK * PEER_HEADS), F32),
                   jax.ShapeDtypeStruct((N, PEER_TOPK * PEER_HEADS), jnp.int32)],
        scratch_shapes=[pltpu.VMEM((tile, QW), F32),
                        pltpu.VMEM((2, PEER_TOPK, PEER_HEADS, tile), F32),
                        pltpu.VMEM((2, PEER_TOPK, PEER_HEADS, tile), jnp.int32)],
        compiler_params=_cparams(("arbitrary",)),
        name="retrieval",
    )(h2, wq_bf, keys_bf)


def _expert_token(r, h_row, gate_col):
    u = pltpu.bitcast(r & jnp.uint32(0xFFFF0000), F32)
    v = pltpu.bitcast(r << 16, F32)
    d = jnp.sum(u * h_row, axis=-1, keepdims=True)
    w = jnp.broadcast_to(gate_col * _gelu_tanh(d), r.shape)
    return jnp.sum(v * w, axis=0, keepdims=True)


def _expert_kernel(*refs, gtok, alpha, dense):
    if dense:
        (idx0_ref, idxn_ref, tab_ref, h_ref, gate_ref, x1_ref, g2_ref, rows_ref, hd_ref, gated_ref, x1d_ref, g2d_ref,
         lng_ref, lnb_ref, o_ref, od_ref, buf_a, buf_b, f_ref, fd_ref, sem_ref) = refs
    else:
        (idx0_ref, idxn_ref, tab_ref, h_ref, gate_ref, x1_ref, g2_ref, lng_ref, lnb_ref,
         o_ref, buf_a, buf_b, f_ref, fd_ref, sem_ref) = refs
    i = pl.program_id(0)
    n_steps = pl.num_programs(0)
    per_tok = PEER_HEADS * PEER_TOPK
    d_model = h_ref.shape[-1]

    def start_token(idx_ref, buf, sem_row, j):
        for g in range(per_tok // 8):
            for t in range(8):
                e = idx_ref[0, 0, j * per_tok + g * 8 + t]
                pltpu.make_async_copy(tab_ref.at[e], buf.at[j, g, pl.ds(t, 1), :],
                                      sem_ref.at[sem_row, j]).start(priority=t % 2)

    def wait_token(buf, other, sem_row, j):
        pltpu.make_async_copy(other.at[j], buf.at[j], sem_ref.at[sem_row, j]).wait()

    def token(j, cur, nxt, cur_row):
        wait_token(cur, nxt, cur_row, j)
        start_token(idxn_ref, nxt, 1 - cur_row, j)
        f_ref[j:j + 1, :] = _expert_token(cur[j].reshape(per_tok, d_model), h_ref[j:j + 1, :],
                                          gate_t[:, j:j + 1])
        if dense:
            fd_ref[j:j + 1, :] = _expert_token(rows_ref[j * per_tok:(j + 1) * per_tok, :], hd_ref[j:j + 1, :],
                                               gated_t[:, j:j + 1])

    @pl.when(i == 0)
    def _():
        def prime(j, _):
            start_token(idx0_ref, buf_a, 0, j)
            return 0
        lax.fori_loop(0, gtok, prime, 0)

    gate_t = gate_ref[...].T
    gated_t = gated_ref[...].T if dense else None

    @pl.when(i % 2 == 0)
    def _():
        for j in range(gtok):
            token(j, buf_a, buf_b, 0)

    @pl.when(i % 2 == 1)
    def _():
        for j in range(gtok):
            token(j, buf_b, buf_a, 1)

    @pl.when((i == n_steps - 1) & (i % 2 == 0))
    def _():
        lax.fori_loop(0, gtok, lambda j, c: (wait_token(buf_b, buf_a, 1, j), 0)[1], 0)

    @pl.when((i == n_steps - 1) & (i % 2 == 1))
    def _():
        lax.fori_loop(0, gtok, lambda j, c: (wait_token(buf_a, buf_b, 0, j), 0)[1], 0)

    o_ref[...] = _ln(alpha * x1_ref[...] + g2_ref[0] * f_ref[...]) * lng_ref[...] + lnb_ref[...]

    if dense:
        od_ref[...] = _ln(alpha * x1d_ref[...] + g2d_ref[0] * fd_ref[...]) * lng_ref[...] + lnb_ref[...]


def _experts(eidx, gate, tab, h2, x1, g2, lng, lnb, *, tok0, n_tok, gtok, seq, alpha, dense_rows=None,
             dense_tok0=0):
    D = h2.shape[1]
    S = n_tok // gtok
    off = tok0 // gtok
    doff = dense_tok0 // gtok
    per_tok = PEER_HEADS * PEER_TOPK
    rows = gtok * per_tok
    spb = seq // gtok
    smem = pltpu.MemorySpace.SMEM
    dense = dense_rows is not None
    tok_spec = lambda o: pl.BlockSpec((gtok, D), lambda i: (i + o, 0))
    gate_spec = lambda o: pl.BlockSpec((gtok, per_tok), lambda i: (i + o, 0))
    g2_spec = lambda o: pl.BlockSpec((1, 1, D), lambda i: ((i + o) // spb, 0, 0))
    row_spec = pl.BlockSpec((1, D), lambda i: (0, 0))
    in_specs = [pl.BlockSpec((1, 1, rows), lambda i: (off, 0, 0), memory_space=smem),
                pl.BlockSpec((1, 1, rows), lambda i: (jnp.minimum(i + 1, S - 1) + off, 0, 0), memory_space=smem),
                pl.BlockSpec(memory_space=pl.ANY),
                tok_spec(off), gate_spec(off), tok_spec(off), g2_spec(off)]
    args = [eidx, eidx, tab, h2, gate, x1, g2]
    out_spec = pl.BlockSpec((gtok, D), lambda i: (i, 0))
    out_shape = jax.ShapeDtypeStruct((n_tok, D), F32)
    if dense:
        in_specs += [pl.BlockSpec((rows, D), lambda i: (i, 0)), tok_spec(doff), gate_spec(doff), tok_spec(doff),
                     g2_spec(doff)]
        args += [dense_rows, h2, gate, x1, g2]
    return pl.pallas_call(
        functools.partial(_expert_kernel, gtok=gtok, alpha=alpha, dense=dense),
        grid=(S,),
        in_specs=in_specs + [row_spec, row_spec],
        out_specs=[out_spec, out_spec] if dense else out_spec,
        out_shape=[out_shape, out_shape] if dense else out_shape,
        scratch_shapes=[pltpu.VMEM((gtok, per_tok // 8, 8, D), jnp.uint32),
                        pltpu.VMEM((gtok, per_tok // 8, 8, D), jnp.uint32),
                        pltpu.VMEM((gtok, D), F32),
                        pltpu.VMEM((gtok, D), F32),
                        pltpu.SemaphoreType.DMA((2, gtok))],
        compiler_params=_cparams(("arbitrary",)),
        name="experts",
    )(*args, lng, lnb)


def _expert_dense_kernel(rows_ref, h_ref, gate_ref, x1_ref, g2_ref, lng_ref, lnb_ref, o_ref, f_ref, *, gtok, alpha):
    per_tok = PEER_HEADS * PEER_TOPK
    gate_t = gate_ref[...].T
    for j in range(gtok):
        f_ref[j:j + 1, :] = _expert_token(rows_ref[j * per_tok:(j + 1) * per_tok, :], h_ref[j:j + 1, :],
                                          gate_t[:, j:j + 1])
    o_ref[...] = _ln(alpha * x1_ref[...] + g2_ref[0] * f_ref[...]) * lng_ref[...] + lnb_ref[...]


def _experts_dense(rows, gate, h2, x1, g2, lng, lnb, *, tok0, n_tok, row_tok0, gtok, seq, alpha):
    D = h2.shape[1]
    per_tok = PEER_HEADS * PEER_TOPK
    off = tok0 // gtok
    steps_per_batch = seq // gtok
    return pl.pallas_call(
        functools.partial(_expert_dense_kernel, gtok=gtok, alpha=alpha),
        grid=(n_tok // gtok,),
        in_specs=[pl.BlockSpec((gtok * per_tok, D), lambda i: (i + row_tok0 // gtok, 0)),
                  pl.BlockSpec((gtok, D), lambda i: (i + off, 0)),
                  pl.BlockSpec((gtok, per_tok), lambda i: (i + off, 0)),
                  pl.BlockSpec((gtok, D), lambda i: (i + off, 0)),
                  pl.BlockSpec((1, 1, D), lambda i: ((i + off) // steps_per_batch, 0, 0)),
                  pl.BlockSpec((1, D), lambda i: (0, 0)),
                  pl.BlockSpec((1, D), lambda i: (0, 0))],
        out_specs=pl.BlockSpec((gtok, D), lambda i: (i, 0)),
        out_shape=jax.ShapeDtypeStruct((n_tok, D), F32),
        scratch_shapes=[pltpu.VMEM((gtok, D), F32)],
        compiler_params=_cparams(("arbitrary",)),
        name="experts_dense",
    )(rows, h2, gate, x1, g2, lng, lnb)


SC_WORKERS = 32
SC_IDX_BLOCK = 128
SC_CHUNK = 32
SC_SHARE_DEN = 16
SC_SHARES_FIRST = (5,)
SC_SHARES_LATER = (6, 3)


def _sc_gather_rows(tab, idx2d, row0, n_rows):
    from jax.experimental.pallas import tpu_sc as plsc
    D = tab.shape[1]
    steps = n_rows // SC_WORKERS
    n_chunks = SC_IDX_BLOCK // SC_CHUNK
    mesh = plsc.VectorSubcoreMesh(core_axis_name="core", subcore_axis_name="subcore")

    @pl.kernel(out_type=jax.ShapeDtypeStruct((n_rows * SC_IDX_BLOCK, D), tab.dtype), mesh=mesh,
               scratch_types=[pltpu.VMEM((1, SC_IDX_BLOCK), jnp.int32),
                              pltpu.VMEM((2, SC_CHUNK, D), tab.dtype),
                              pltpu.SemaphoreType.DMA((2,))])
    def k(x_hbm, i_hbm, o_hbm, i_vmem, buf, wsem):
        wid = lax.axis_index("core") * (SC_WORKERS // 2) + lax.axis_index("subcore")

        def write(row, c):
            return pltpu.make_async_copy(buf.at[c % 2], o_hbm.at[pl.ds(row * SC_IDX_BLOCK + c * SC_CHUNK, SC_CHUNK)],
                                         wsem.at[c % 2])

        @pl.loop(0, steps)
        def _(st):
            row = wid * steps + st
            pltpu.sync_copy(i_hbm.at[pl.ds(row0 + row, 1)], i_vmem)
            for c in range(n_chunks):
                if c >= 2:
                    write(row, c - 2).wait()
                pltpu.sync_copy(x_hbm.at[i_vmem.at[0, pl.ds(c * SC_CHUNK, SC_CHUNK)]], buf.at[c % 2])
                write(row, c).start()
            for c in range(n_chunks - 2, n_chunks):
                write(row, c).wait()
    return k(tab, idx2d)


def _rope_tables(L):
    half = HEAD_DIM // 4
    t = np.arange(L)
    freqs = ROPE_THETA ** (-np.arange(half, dtype=np.float32) / half)
    lane = np.arange(HEAD_DIM)
    pos = np.where(lane[None, :] < HEAD_DIM // 2, (t // GRID_W)[:, None], (t % GRID_W)[:, None]).astype(np.float32)
    ang = pos * freqs[lane % half][None, :].astype(np.float32)
    cos, sin = np.cos(ang), np.sin(ang)
    first = ((lane % (2 * half)) < half)[None, :]
    slo = np.where(first, -sin, 0.0)
    shi = np.where(first, 0.0, sin)
    tile = lambda a: jnp.asarray(np.tile(a, (1, N_KV_HEADS)), F32)
    return tile(cos), tile(slo), tile(shi)


def _pack_tables(u, v):
    ub = lax.bitcast_convert_type(u.astype(BF16), jnp.uint16).astype(jnp.uint32)
    vb = lax.bitcast_convert_type(v.astype(BF16), jnp.uint16).astype(jnp.uint32)
    return (ub << 16) | vb


def _pick(n, prefs):
    for p in prefs:
        if n % p == 0:
            return p
    return n


def kernel(x, c, ctx, c_ctx, w_mod, b_mod, w_in, q_norm_g, k_norm_g, gmlp_ln_g, gmlp_ln_b, gmlp_w_s, gmlp_b_s,
           conv_w_dw, conv_b_dw, conv_ln_g, conv_ln_b, conv_w_pw, conv_b_pw, w_out, ln1_g, ln1_b, ln2_g, ln2_b,
           peer_w_q, peer_sub_keys, peer_u, peer_v):
    B, L, D = x.shape
    LC = ctx.shape[1]
    depth = w_mod.shape[0]
    alpha = (2 * depth) ** 0.25
    gw = gmlp_ln_g.shape[-1]

    cos, slo, shi = _rope_tables(L)
    ones_c = jnp.ones((LC, KV_WIDTH), F32)
    zeros_c = jnp.zeros((LC, KV_WIDTH), F32)
    gidx = np.arange(ATTN_WIDTH) // HEAD_DIM
    gmat = jnp.asarray((gidx[:, None] == gidx[None, :]) / HEAD_DIM, BF16)

    cond = jnp.concatenate([c, c_ctx[None, :], jnp.zeros((-(B + 1) % 8, D), F32)], axis=0)

    t_in = _pick(L, (512, 256, 128))
    t_mix = _pick(L, (256, 128))
    t_inc = _pick(LC, (512, 256, 128))
    t_mixc = _pick(LC, (256, 128))
    tk_lat = _pick(L + LC, (768, 512, 384, 256, 128))
    tk_ctx = _pick(LC, (512, 256, 128))
    gtok = 16

    def retrieve(h2, wq_bf, keys_bf):
        n = h2.shape[0] * h2.shape[1]
        return _retrieval(h2.reshape(n, D), wq_bf, keys_bf, tile=_pick(n, (512, 256, 128)))

    def peer(h2, x1, g2, gate, eidx, l, seq, tab, sc_jobs):
        n = h2.shape[0] * h2.shape[1]
        h2f, x1f = h2.reshape(n, D), x1.reshape(n, D)
        lng, lnb = ln2_g[l].reshape(1, D), ln2_b[l].reshape(1, D)
        eidx3 = eidx.reshape(n // gtok, 1, -1)
        common = dict(gtok=gtok, seq=seq, alpha=alpha)
        args = (gate, tab, h2f, x1f, g2, lng, lnb)
        sc_jobs = [j for j in sc_jobs if j > 0]
        n_self = n - sum(sc_jobs)
        starts = [n_self + sum(sc_jobs[:k]) for k in range(len(sc_jobs))]
        rows = [_sc_gather_rows(tab.reshape(-1, D), eidx, s, j) for s, j in zip(starts, sc_jobs)]
        outs_self, outs_sc, done = [], [[] for _ in sc_jobs], 0
        for k, (s, j) in enumerate(zip(starts, sc_jobs)):
            m = min(n_self - done, j)
            if m:
                both = _experts(eidx3, *args, tok0=done, n_tok=m, dense_rows=rows[k], dense_tok0=s, **common)
                outs_self.append(both[0])
                outs_sc[k].append(both[1])
                done += m
            if j > m:
                outs_sc[k].append(_experts_dense(rows[k], gate, h2f, x1f, g2, lng, lnb, tok0=s + m, n_tok=j - m,
                                                 row_tok0=m, **common))
        if n_self > done:
            outs_self.append(_experts(eidx3, *args, tok0=done, n_tok=n_self - done, **common))
        outs = outs_self + [o for per_job in outs_sc for o in per_job]
        return (outs[0] if len(outs) == 1 else jnp.concatenate(outs, axis=0)).reshape(h2.shape)

    def sc_share(n, num):
        align = SC_WORKERS * gtok
        return n * num // SC_SHARE_DEN // align * align if n >= 8192 else 0

    n_groups = 2 if B % 2 == 0 else 1
    bg = B // n_groups
    groups = [slice(g * bg, (g + 1) * bg) for g in range(n_groups)]
    x_grp = [x[bs] for bs in groups]
    x_ctx = ctx
    for l in range(depth):
        last = l == depth - 1
        mod = _modulation(cond, w_mod[l], b_mod[l])
        m_lat = [mod[:B, k * D:(k + 1) * D].reshape(B, 1, D) for k in range(6)]
        m_ctx = [jnp.broadcast_to(mod[B:B + 1, k * D:(k + 1) * D].reshape(1, 1, D), (B, 1, D)) for k in range(6)]

        w_in_bf = w_in[l].astype(BF16)
        gq = jnp.tile(q_norm_g[l], N_Q_HEADS).reshape(1, ATTN_WIDTH)
        gk = jnp.tile(k_norm_g[l], N_KV_HEADS).reshape(1, KV_WIDTH)
        mix_prm = (gmlp_ln_g[l].reshape(1, gw), gmlp_ln_b[l].reshape(1, gw),
                   gmlp_w_s[l].astype(BF16),
                   jnp.repeat(gmlp_b_s[l].T, gw // GMLP_GROUPS, axis=1),
                   conv_w_dw[l], conv_b_dw[l].reshape(1, -1), conv_ln_g[l].reshape(1, -1),
                   conv_ln_b[l].reshape(1, -1), conv_w_pw[l].astype(BF16), conv_b_pw[l].reshape(1, -1),
                   w_out[l].astype(BF16), ln1_g[l].reshape(1, D), ln1_b[l].reshape(1, D))
        wq_bf = peer_w_q[l].astype(BF16)
        keys_bf = peer_sub_keys[l].reshape(2 * PEER_HEADS, N_KEYS, PEER_HALF).astype(BF16)
        tab = _pack_tables(peer_u[l], peer_v[l]).reshape(-1, 1, D)

        q_c, kt_c, v_c, zg_c, zc_c = _inproj(x_ctx, m_ctx[0], m_ctx[1], w_in_bf, gq, gk, ones_c, zeros_c, zeros_c,
                                              gmat, tile=t_inc)
        staged = []
        for g, bs in enumerate(groups):
            xg = x_grp[g]
            q_l, kt_l, v_l, zg_l, zc_l = _inproj(xg, m_lat[0][bs], m_lat[1][bs], w_in_bf, gq, gk, cos, slo, shi, gmat,
                                                  tile=t_in)
            kt_all = jnp.concatenate([kt_c[bs], kt_l], axis=3)
            v_all = jnp.concatenate([v_c[bs], v_l], axis=1)
            o_lat = _attention(q_l, kt_all, v_all, tq=128, tk=tk_lat)
            x1, h2 = _mixers(xg, o_lat, zg_l, zc_l, m_lat[2][bs], m_lat[3][bs], m_lat[4][bs], mix_prm, tile=t_mix,
                             alpha=alpha)
            staged.append((h2, x1) + tuple(retrieve(h2, wq_bf, keys_bf)))
        for g, bs in enumerate(groups):
            h2, x1, gate, eidx = staged[g]
            shares = SC_SHARES_FIRST if g == 0 and n_groups > 1 else SC_SHARES_LATER
            x_grp[g] = peer(h2, x1, m_lat[5][bs], gate, eidx, l, L, tab, [sc_share(bg * L, k) for k in shares])

        if not last:
            o_ctx = _attention(q_c, kt_c, v_c, tq=128, tk=tk_ctx)
            x1c, h2c = _mixers(x_ctx, o_ctx, zg_c, zc_c, m_ctx[2], m_ctx[3], m_ctx[4], mix_prm, tile=t_mixc,
                               alpha=alpha)
            gate_c, eidx_c = retrieve(h2c, wq_bf, keys_bf)
            x_ctx = peer(h2c, x1c, m_ctx[5], gate_c, eidx_c, l, LC, tab, [])
    return jnp.concatenate(x_grp, axis=0)
```

```python
import functools
import math

import jax
import jax.numpy as jnp
import numpy as np
from jax import lax
from jax.experimental import pallas as pl
from jax.experimental.pallas import tpu as pltpu

F32 = jnp.float32
BF16 = jnp.bfloat16

EPS = 1e-6
GRID_W = 64
N_Q_HEADS = 8
N_KV_HEADS = 2
Q_PER_KV = N_Q_HEADS // N_KV_HEADS
HEAD_DIM = 64
ATTN_WIDTH = N_Q_HEADS * HEAD_DIM
KV_WIDTH = N_KV_HEADS * HEAD_DIM
QKV_WIDTH = ATTN_WIDTH + 2 * KV_WIDTH
ROPE_THETA = 10000.0
CHUNK = 128
GMLP_GROUPS = 4
CONV_K = 31
CONV_HALO = 16
PEER_HEADS = 8
N_KEYS = 128
PEER_TOPK = 16
PEER_HALF = 128

VMEM_LIMIT = 56 * 1024 * 1024


def _cparams(sem):
    return pltpu.CompilerParams(dimension_semantics=sem, vmem_limit_bytes=VMEM_LIMIT)


def _gelu_tanh(x):
    return 0.5 * x * (1.0 + jnp.tanh(math.sqrt(2.0 / math.pi) * (x + 0.044715 * (x * x * x))))


def _ln(x):
    mu = jnp.mean(x, axis=-1, keepdims=True)
    xc = x - mu
    var = jnp.mean(xc * xc, axis=-1, keepdims=True)
    return xc * lax.rsqrt(var + EPS)


def _mod_kernel(c_ref, w_ref, b_ref, o_ref):
    c = c_ref[...]
    s = c * jax.nn.sigmoid(c)
    o_ref[...] = jnp.dot(s, w_ref[...], preferred_element_type=F32,
                         precision=lax.Precision.HIGHEST) + b_ref[...]


def _modulation(cond, w_mod, b_mod):
    R, D = cond.shape
    N = w_mod.shape[1]
    tn = 1024
    return pl.pallas_call(
        _mod_kernel,
        grid=(N // tn,),
        in_specs=[pl.BlockSpec((R, D), lambda j: (0, 0)),
                  pl.BlockSpec((D, tn), lambda j: (0, j)),
                  pl.BlockSpec((1, tn), lambda j: (0, j))],
        out_specs=pl.BlockSpec((R, tn), lambda j: (0, j)),
        out_shape=jax.ShapeDtypeStruct((R, N), F32),
        compiler_params=_cparams(("arbitrary",)),
        name="modulation",
    )(cond, w_mod, b_mod.reshape(1, N))


def _group_mean_sq(t, gmat):
    sq = t * t
    hi = sq.astype(BF16)
    lo = (sq - hi.astype(F32)).astype(BF16)
    return (jnp.dot(hi, gmat, preferred_element_type=F32)
            + jnp.dot(lo, gmat, preferred_element_type=F32))


def _rope(t, cos, sin_lo, sin_hi):
    w = t.shape[-1]
    up = pltpu.roll(t, w - 16, axis=1)
    dn = pltpu.roll(t, 16, axis=1)
    return t * cos + up * sin_lo + dn * sin_hi


def _inproj_kernel(x_ref, sh_ref, sc_ref, w_ref, gq_ref, gk_ref, cos_ref, slo_ref, shi_ref, gmat_ref,
                   q_ref, kt_ref, v_ref, zg_ref, zc_ref):
    h = _ln(x_ref[0]) * (1.0 + sc_ref[0]) + sh_ref[0]
    p = jnp.dot(h.astype(BF16), w_ref[...], preferred_element_type=F32)
    q = p[:, :ATTN_WIDTH]
    k = p[:, ATTN_WIDTH:ATTN_WIDTH + KV_WIDTH]
    v = p[:, ATTN_WIDTH + KV_WIDTH:QKV_WIDTH]
    gm_w = (p.shape[1] - QKV_WIDTH) // 2
    zg_ref[0] = p[:, QKV_WIDTH:QKV_WIDTH + gm_w]
    zc_ref[0] = p[:, QKV_WIDTH + gm_w:]
    v_ref[0] = v.astype(BF16)

    gmat = gmat_ref[...]
    cos, slo, shi = cos_ref[...], slo_ref[...], shi_ref[...]
    qn = q * lax.rsqrt(_group_mean_sq(q, gmat) + EPS) * gq_ref[...]
    rep = ATTN_WIDTH // KV_WIDTH
    qr = _rope(qn, jnp.concatenate([cos] * rep, axis=1), jnp.concatenate([slo] * rep, axis=1),
               jnp.concatenate([shi] * rep, axis=1))
    qr = (qr * (HEAD_DIM ** -0.5 * math.log2(math.e))).astype(BF16)
    for hh in range(N_Q_HEADS):
        q_ref[0, hh] = qr[:, hh * HEAD_DIM:(hh + 1) * HEAD_DIM]

    kn = k * lax.rsqrt(_group_mean_sq(k, gmat[:KV_WIDTH, :KV_WIDTH]) + EPS) * gk_ref[...]
    kt = _rope(kn, cos, slo, shi).T
    for g in range(N_KV_HEADS):
        kt_ref[0, g] = kt[g * HEAD_DIM:(g + 1) * HEAD_DIM, :].astype(BF16)


def _inproj(x, sh, sc, w_in_bf, gq, gk, cos, slo, shi, gmat, *, tile):
    B, L, D = x.shape
    W = w_in_bf.shape[1]
    gm_w = (W - QKV_WIDTH) // 2
    nt = L // tile
    bmap = lambda b, i: (b, 0, 0)
    cmap = lambda b, i: (0, 0)
    return pl.pallas_call(
        _inproj_kernel,
        grid=(B, nt),
        in_specs=[pl.BlockSpec((1, tile, D), lambda b, i: (b, i, 0)),
                  pl.BlockSpec((1, 1, D), bmap), pl.BlockSpec((1, 1, D), bmap),
                  pl.BlockSpec((D, W), cmap),
                  pl.BlockSpec((1, ATTN_WIDTH), cmap), pl.BlockSpec((1, KV_WIDTH), cmap),
                  pl.BlockSpec((tile, KV_WIDTH), lambda b, i: (i, 0)),
                  pl.BlockSpec((tile, KV_WIDTH), lambda b, i: (i, 0)),
                  pl.BlockSpec((tile, KV_WIDTH), lambda b, i: (i, 0)),
                  pl.BlockSpec((ATTN_WIDTH, ATTN_WIDTH), cmap)],
        out_specs=[pl.BlockSpec((1, N_Q_HEADS, tile, HEAD_DIM), lambda b, i: (b, 0, i, 0)),
                   pl.BlockSpec((1, N_KV_HEADS, HEAD_DIM, tile), lambda b, i: (b, 0, 0, i)),
                   pl.BlockSpec((1, tile, KV_WIDTH), lambda b, i: (b, i, 0)),
                   pl.BlockSpec((1, tile, gm_w), lambda b, i: (b, i, 0)),
                   pl.BlockSpec((1, tile, gm_w), lambda b, i: (b, i, 0))],
        out_shape=[jax.ShapeDtypeStruct((B, N_Q_HEADS, L, HEAD_DIM), BF16),
                   jax.ShapeDtypeStruct((B, N_KV_HEADS, HEAD_DIM, L), BF16),
                   jax.ShapeDtypeStruct((B, L, KV_WIDTH), BF16),
                   jax.ShapeDtypeStruct((B, L, gm_w), F32),
                   jax.ShapeDtypeStruct((B, L, gm_w), F32)],
        compiler_params=_cparams(("arbitrary", "arbitrary")),
        name="inproj",
    )(x, sh, sc, w_in_bf, gq, gk, cos, slo, shi, gmat)


def _attn_kernel(q_ref, kt_ref, vx_ref, o_ref, p_ref, *, tq, tk, nk):
    m_rows = Q_PER_KV * tq
    qs = [q_ref[0, Q_PER_KV * g:Q_PER_KV * (g + 1)].reshape(m_rows, HEAD_DIM) for g in range(N_KV_HEADS)]

    def pv(c, par, g):
        off = pl.multiple_of(c * tk, tk)
        return jnp.dot(p_ref[par, g], vx_ref[0, g, pl.ds(off, tk), :], preferred_element_type=F32)

    def step(c, par, carry):
        off = pl.multiple_of(c * tk, tk)
        out = []
        for g in range(N_KV_HEADS):
            m, acc = carry[g]
            s = jnp.dot(qs[g], kt_ref[0, g, :, pl.ds(off, tk)], preferred_element_type=F32)
            m_new = jnp.maximum(m, jnp.max(s, axis=-1, keepdims=True))
            p_ref[1 - par, g] = jnp.exp2(s - m_new).astype(BF16)
            if acc is not None:
                acc = jnp.exp2(m - m_new) * (acc + pv(c - 1, par, g))
            out.append((m_new, acc))
        return tuple(out)

    first = step(0, 1, tuple((jnp.full((m_rows, 1), -1e30, F32), None) for _ in range(N_KV_HEADS)))
    carry = tuple((m, jnp.zeros((m_rows, 2 * HEAD_DIM), F32)) for m, _ in first)

    n_pairs = (nk - 1) // 2
    def pair(t, carry):
        carry = step(2 * t + 1, 0, carry)
        return step(2 * t + 2, 1, carry)
    carry = lax.fori_loop(0, n_pairs, pair, carry)
    par = 0
    if (nk - 1) % 2:
        carry = step(nk - 1, 0, carry)
        par = 1

    for g in range(N_KV_HEADS):
        acc = carry[g][1] + pv(nk - 1, par, g)
        og = acc / pltpu.roll(acc, HEAD_DIM, axis=1)
        for r in range(Q_PER_KV):
            hh = Q_PER_KV * g + r
            o_ref[0, :, hh * HEAD_DIM:(hh + 1) * HEAD_DIM] = og[r * tq:(r + 1) * tq, :HEAD_DIM].astype(o_ref.dtype)


def _attention(q, kt, v, *, tq, tk):
    B, _, L, _ = q.shape
    LK = kt.shape[-1]
    ones = jnp.ones((B, LK, HEAD_DIM), v.dtype)
    vx = jnp.stack([jnp.concatenate([v[..., g * HEAD_DIM:(g + 1) * HEAD_DIM], ones], axis=-1)
                    for g in range(N_KV_HEADS)], axis=1)
    return pl.pallas_call(
        functools.partial(_attn_kernel, tq=tq, tk=tk, nk=LK // tk),
        grid=(B, L // tq),
        in_specs=[pl.BlockSpec((1, N_Q_HEADS, tq, HEAD_DIM), lambda b, i: (b, 0, i, 0)),
                  pl.BlockSpec((1, N_KV_HEADS, HEAD_DIM, LK), lambda b, i: (b, 0, 0, 0)),
                  pl.BlockSpec((1, N_KV_HEADS, LK, 2 * HEAD_DIM), lambda b, i: (b, 0, 0, 0))],
        out_specs=pl.BlockSpec((1, tq, ATTN_WIDTH), lambda b, i: (b, i, 0)),
        out_shape=jax.ShapeDtypeStruct((B, L, ATTN_WIDTH), BF16),
        scratch_shapes=[pltpu.VMEM((2, N_KV_HEADS, Q_PER_KV * tq, tk), BF16)],
        compiler_params=_cparams(("arbitrary", "arbitrary")),
        name="attention",
    )(q, kt, vx)


def _mix_kernel(x_ref, o_ref, zg_ref, zc_ref, zp_ref, zn_ref, g1_ref, sh2_ref, sc2_ref,
                gln_g_ref, gln_b_ref, ws_ref, bs_ref, wdw_ref, bdw_ref, cln_g_ref, cln_b_ref,
                wpw_ref, bpw_ref, wout_ref, ln1g_ref, ln1b_ref,
                x1_ref, h2_ref, ybuf_ref, *, tile, alpha):
    i = pl.program_id(1)
    nt = pl.num_programs(1)
    gw = zg_ref.shape[-1] // 2
    cw = zc_ref.shape[-1] // 2

    z = _gelu_tanh(zg_ref[0])
    u, v = z[:, :gw], z[:, gw:]
    vn = (_ln(v) * gln_g_ref[...] + gln_b_ref[...]).astype(BF16)
    grp = lax.broadcasted_iota(jnp.int32, (CHUNK, gw), 1) // (gw // GMLP_GROUPS)
    sg = []
    for n in range(tile // CHUNK):
        vc = vn[n * CHUNK:(n + 1) * CHUNK]
        s = jnp.zeros((CHUNK, gw), F32)
        for g in range(GMLP_GROUPS):
            s = jnp.where(grp == g, jnp.dot(ws_ref[g], vc, preferred_element_type=F32), s)
        sg.append(s + bs_ref[...])
    y_g = u * jnp.concatenate(sg, axis=0)

    def glu(zz):
        return zz[:, :cw] * jax.nn.sigmoid(zz[:, cw:])

    ybuf_ref[0:CONV_HALO] = glu(zp_ref[0]) * (i > 0).astype(F32)
    ybuf_ref[CONV_HALO:CONV_HALO + tile] = glu(zc_ref[0])
    ybuf_ref[CONV_HALO + tile:2 * CONV_HALO + tile] = glu(zn_ref[0]) * (i < nt - 1).astype(F32)
    base = CONV_HALO - CONV_K // 2
    conv = jnp.zeros((tile, cw), F32)
    for kk in range(CONV_K):
        conv = conv + ybuf_ref[base + kk:base + kk + tile, :] * wdw_ref[kk:kk + 1, :]
    conv = conv + bdw_ref[...]
    t = _ln(conv) * cln_g_ref[...] + cln_b_ref[...]
    t = t * jax.nn.sigmoid(t)
    y_c = jnp.dot(t.astype(BF16), wpw_ref[...], preferred_element_type=F32) + bpw_ref[...]

    aw = o_ref.shape[-1]
    y = (jnp.dot(o_ref[0], wout_ref[0:aw, :], preferred_element_type=F32)
         + jnp.dot(y_g.astype(BF16), wout_ref[aw:aw + gw, :], preferred_element_type=F32)
         + jnp.dot(y_c.astype(BF16), wout_ref[aw + gw:, :], preferred_element_type=F32))
    x1 = _ln(alpha * x_ref[0] + g1_ref[0] * y) * ln1g_ref[...] + ln1b_ref[...]
    x1_ref[0] = x1
    h2_ref[0] = _ln(x1) * (1.0 + sc2_ref[0]) + sh2_ref[0]


def _mixers(x, o, zg, zc, g1, sh2, sc2, prm, *, tile, alpha):
    B, L, D = x.shape
    nt = L // tile
    hb = tile // CONV_HALO
    nhb = L // CONV_HALO
    gw2, cw2 = zg.shape[-1], zc.shape[-1]
    tmap = lambda b, i: (b, i, 0)
    bmap = lambda b, i: (b, 0, 0)
    c2 = lambda b, i: (0, 0)
    c3 = lambda b, i: (0, 0, 0)
    in_specs = [pl.BlockSpec((1, tile, D), tmap),
                pl.BlockSpec((1, tile, o.shape[-1]), tmap),
                pl.BlockSpec((1, tile, gw2), tmap),
                pl.BlockSpec((1, tile, cw2), tmap),
                pl.BlockSpec((1, CONV_HALO, cw2), lambda b, i: (b, jnp.maximum(i * hb - 1, 0), 0)),
                pl.BlockSpec((1, CONV_HALO, cw2), lambda b, i: (b, jnp.minimum((i + 1) * hb, nhb - 1), 0)),
                pl.BlockSpec((1, 1, D), bmap), pl.BlockSpec((1, 1, D), bmap), pl.BlockSpec((1, 1, D), bmap)]
    for a in prm:
        in_specs.append(pl.BlockSpec(a.shape, c2 if a.ndim == 2 else c3))
    return pl.pallas_call(
        functools.partial(_mix_kernel, tile=tile, alpha=alpha),
        grid=(B, nt),
        in_specs=in_specs,
        out_specs=[pl.BlockSpec((1, tile, D), tmap), pl.BlockSpec((1, tile, D), tmap)],
        out_shape=[jax.ShapeDtypeStruct((B, L, D), F32), jax.ShapeDtypeStruct((B, L, D), F32)],
        scratch_shapes=[pltpu.VMEM((tile + 2 * CONV_HALO, cw2 // 2), F32)],
        compiler_params=_cparams(("arbitrary", "arbitrary")),
        name="mixers",
    )(x, o, zg, zc, zc, zc, g1, sh2, sc2, *prm)


def _oddeven_merge_sort_pairs(n):
    pairs = []
    p = 1
    while p < n:
        k = p
        while k >= 1:
            for j in range(k % p, n - k, 2 * k):
                for i in range(min(k, n - j - k)):
                    if (i + j) // (2 * p) == (i + j + k) // (2 * p):
                        pairs.append((i + j, i + j + k))
            k //= 2
        p *= 2
    return pairs


_SORT16 = _oddeven_merge_sort_pairs(16)
_BITONIC16 = [(i, i + d) for d in (8, 4, 2, 1) for i in range(16) if (i & d) == 0]


def _before(a, ia, b, ib):
    return (a > b) | ((a == b) & (ia < ib))


def _cmpx(vals, keys, pay, i, j):
    keep = _before(vals[i], keys[i], vals[j], keys[j])
    vals[i], vals[j] = jnp.where(keep, vals[i], vals[j]), jnp.where(keep, vals[j], vals[i])
    keys[i], keys[j] = jnp.where(keep, keys[i], keys[j]), jnp.where(keep, keys[j], keys[i])
    if pay is not None:
        pay[i], pay[j] = jnp.where(keep, pay[i], pay[j]), jnp.where(keep, pay[j], pay[i])


def _top16_merge(av, ak, ap, bv, bk, bp):
    n = len(av)
    ov, ok, op = [], [], []
    for i in range(n):
        keep = _before(av[i], ak[i], bv[n - 1 - i], bk[n - 1 - i])
        ov.append(jnp.where(keep, av[i], bv[n - 1 - i]))
        ok.append(jnp.where(keep, ak[i], bk[n - 1 - i]))
        if ap is not None:
            op.append(jnp.where(keep, ap[i], bp[n - 1 - i]))
    return ov, ok, (op if ap is not None else None)


def _stage2_candidates():
    return [(a, b) for a in range(PEER_TOPK) for b in range(PEER_TOPK) if (a + 1) * (b + 1) <= PEER_TOPK]


def _retr_kernel(h_ref, wq_ref, keys_ref, gate_ref, eidx_ref, q_s, sv_s, si_s, *, tile):
    q_s[...] = jnp.dot(h_ref[...].astype(BF16), wq_ref[...], preferred_element_type=F32)
    sub = lax.broadcasted_iota(jnp.int32, (8, tile), 0)
    sv_s[...] = jnp.zeros(sv_s.shape, F32)
    si_s[...] = jnp.zeros(si_s.shape, jnp.int32)

    def stage1(hp, _):
        qh = q_s[:, pl.ds(pl.multiple_of(hp * PEER_HALF, PEER_HALF), PEER_HALF)].astype(BF16)
        st = lax.dot_general(keys_ref[hp], qh, (((1,), (1,)), ((), ())),
                             preferred_element_type=F32)
        st = st.reshape(N_KEYS // 8, 8, tile)
        vals = [st[i] for i in range(16)]
        keys = [sub + 8 * i for i in range(16)]
        for (i, j) in _SORT16:
            _cmpx(vals, keys, None, i, j)
        for shift in (4, 2, 1):
            bv = [pltpu.roll(t, shift, axis=0) for t in vals]
            bk = [pltpu.roll(t, shift, axis=0) for t in keys]
            vals, keys, _ = _top16_merge(vals, keys, None, bv, bk, None)
            for (i, j) in _BITONIC16:
                _cmpx(vals, keys, None, i, j)
        head = hp // 2
        half = hp % 2
        for i in range(16):
            sv_s[half, i] = jnp.where(sub == head, vals[i], sv_s[half, i])
            si_s[half, i] = jnp.where(sub == head, keys[i], si_s[half, i])
        return 0

    lax.fori_loop(0, 2 * PEER_HEADS, stage1, 0)

    cands = _stage2_candidates()
    n_pad = 64
    vals, poss, exps = [], [], []
    for (a, b) in cands:
        vals.append(sv_s[0, a] + sv_s[1, b])
        poss.append(jnp.full((8, tile), a * PEER_TOPK + b, jnp.int32))
        exps.append(si_s[0, a] * N_KEYS + si_s[1, b])
    for t in range(n_pad - len(cands)):
        vals.append(jnp.full((8, tile), -jnp.inf, F32))
        poss.append(jnp.full((8, tile), PEER_TOPK * PEER_TOPK + t, jnp.int32))
        exps.append(jnp.zeros((8, tile), jnp.int32))
    groups = []
    for g in range(n_pad // 16):
        gv, gk, gp = vals[16 * g:16 * g + 16], poss[16 * g:16 * g + 16], exps[16 * g:16 * g + 16]
        for (i, j) in _SORT16:
            _cmpx(gv, gk, gp, i, j)
        groups.append((gv, gk, gp))
    while len(groups) > 1:
        nxt = []
        for g in range(0, len(groups), 2):
            mv, mk, mp = _top16_merge(*groups[g], *groups[g + 1])
            if len(groups) > 2:
                for (i, j) in _BITONIC16:
                    _cmpx(mv, mk, mp, i, j)
            nxt.append((mv, mk, mp))
        groups = nxt
    best, _, experts = groups[0]

    mx = functools.reduce(jnp.maximum, best)
    ex = [jnp.exp(b - mx) for b in best]
    inv = 1.0 / functools.reduce(lambda a, b: a + b, ex)
    gate_ref[...] = jnp.concatenate([e * inv for e in ex], axis=0).T
    eidx_ref[...] = jnp.concatenate(experts, axis=0).T


def _retrieval(h2, wq_bf, keys_bf, *, tile):
    N, D = h2.shape
    QW = wq_bf.shape[1]
    return pl.pallas_call(
        functools.partial(_retr_kernel, tile=tile),
        grid=(N // tile,),
        in_specs=[pl.BlockSpec((tile, D), lambda i: (i, 0)),
                  pl.BlockSpec((D, QW), lambda i: (0, 0)),
                  pl.BlockSpec(keys_bf.shape, lambda i: (0, 0, 0))],
        out_specs=[pl.BlockSpec((tile, PEER_TOPK * PEER_HEADS), lambda i: (i, 0)),
                   pl.BlockSpec((tile, PEER_TOPK * PEER_HEADS), lambda i: (i, 0))],
        out_shape=[jax.ShapeDtypeStruct((N, PEER_TOPK * PEER_HEADS), F32),
                   jax.ShapeDtypeStruct((N, PEER_TOPK * PEER_HEADS), jnp.int32)],
        scratch_shapes=[pltpu.VMEM((tile, QW), F32),
                        pltpu.VMEM((2, PEER_TOPK, PEER_HEADS, tile), F32),
                        pltpu.VMEM((2, PEER_TOPK, PEER_HEADS, tile), jnp.int32)],
        compiler_params=_cparams(("arbitrary",)),
        name="retrieval",
    )(h2, wq_bf, keys_bf)


def _expert_token(r, h_row, gate_col):
    u = pltpu.bitcast(r & jnp.uint32(0xFFFF0000), F32)
    v = pltpu.bitcast(r << 16, F32)
    d = jnp.sum(u * h_row, axis=-1, keepdims=True)
    w = jnp.broadcast_to(gate_col * _gelu_tanh(d), r.shape)
    return jnp.sum(v * w, axis=0, keepdims=True)


def _expert_kernel(*refs, gtok, alpha, dense):
    if dense:
        (idx0_ref, idxn_ref, tab_ref, h_ref, gate_ref, x1_ref, g2_ref, rows_ref, hd_ref, gated_ref, x1d_ref, g2d_ref,
         lng_ref, lnb_ref, o_ref, od_ref, buf_a, buf_b, f_ref, fd_ref, sem_ref) = refs
    else:
        (idx0_ref, idxn_ref, tab_ref, h_ref, gate_ref, x1_ref, g2_ref, lng_ref, lnb_ref,
         o_ref, buf_a, buf_b, f_ref, fd_ref, sem_ref) = refs
    i = pl.program_id(0)
    n_steps = pl.num_programs(0)
    per_tok = PEER_HEADS * PEER_TOPK
    d_model = h_ref.shape[-1]

    def start_token(idx_ref, buf, sem_row, j):
        for g in range(per_tok // 8):
            for t in range(8):
                e = idx_ref[0, 0, j * per_tok + g * 8 + t]
                pltpu.make_async_copy(tab_ref.at[e], buf.at[j, g, pl.ds(t, 1), :],
                                      sem_ref.at[sem_row, j]).start(priority=t % 2)

    def wait_token(buf, other, sem_row, j):
        pltpu.make_async_copy(other.at[j], buf.at[j], sem_ref.at[sem_row, j]).wait()

    def token(j, cur, nxt, cur_row):
        wait_token(cur, nxt, cur_row, j)
        start_token(idxn_ref, nxt, 1 - cur_row, j)
        f_ref[j:j + 1, :] = _expert_token(cur[j].reshape(per_tok, d_model), h_ref[j:j + 1, :],
                                          gate_t[:, j:j + 1])
        if dense:
            fd_ref[j:j + 1, :] = _expert_token(rows_ref[j * per_tok:(j + 1) * per_tok, :], hd_ref[j:j + 1, :],
                                               gated_t[:, j:j + 1])

    @pl.when(i == 0)
    def _():
        def prime(j, _):
            start_token(idx0_ref, buf_a, 0, j)
            return 0
        lax.fori_loop(0, gtok, prime, 0)

    gate_t = gate_ref[...].T
    gated_t = gated_ref[...].T if dense else None

    @pl.when(i % 2 == 0)
    def _():
        for j in range(gtok):
            token(j, buf_a, buf_b, 0)

    @pl.when(i % 2 == 1)
    def _():
        for j in range(gtok):
            token(j, buf_b, buf_a, 1)

    @pl.when((i == n_steps - 1) & (i % 2 == 0))
    def _():
        lax.fori_loop(0, gtok, lambda j, c: (wait_token(buf_b, buf_a, 1, j), 0)[1], 0)

    @pl.when((i == n_steps - 1) & (i % 2 == 1))
    def _():
        lax.fori_loop(0, gtok, lambda j, c: (wait_token(buf_a, buf_b, 0, j), 0)[1], 0)

    o_ref[...] = _ln(alpha * x1_ref[...] + g2_ref[0] * f_ref[...]) * lng_ref[...] + lnb_ref[...]

    if dense:
        od_ref[...] = _ln(alpha * x1d_ref[...] + g2d_ref[0] * fd_ref[...]) * lng_ref[...] + lnb_ref[...]


def _experts(eidx, gate, tab, h2, x1, g2, lng, lnb, *, tok0, n_tok, gtok, seq, alpha, dense_rows=None,
             dense_tok0=0):
    D = h2.shape[1]
    S = n_tok // gtok
    off = tok0 // gtok
    doff = dense_tok0 // gtok
    per_tok = PEER_HEADS * PEER_TOPK
    rows = gtok * per_tok
    spb = seq // gtok
    smem = pltpu.MemorySpace.SMEM
    dense = dense_rows is not None
    tok_spec = lambda o: pl.BlockSpec((gtok, D), lambda i: (i + o, 0))
    gate_spec = lambda o: pl.BlockSpec((gtok, per_tok), lambda i: (i + o, 0))
    g2_spec = lambda o: pl.BlockSpec((1, 1, D), lambda i: ((i + o) // spb, 0, 0))
    row_spec = pl.BlockSpec((1, D), lambda i: (0, 0))
    in_specs = [pl.BlockSpec((1, 1, rows), lambda i: (off, 0, 0), memory_space=smem),
                pl.BlockSpec((1, 1, rows), lambda i: (jnp.minimum(i + 1, S - 1) + off, 0, 0), memory_space=smem),
                pl.BlockSpec(memory_space=pl.ANY),
                tok_spec(off), gate_spec(off), tok_spec(off), g2_spec(off)]
    args = [eidx, eidx, tab, h2, gate, x1, g2]
    out_spec = pl.BlockSpec((gtok, D), lambda i: (i, 0))
    out_shape = jax.ShapeDtypeStruct((n_tok, D), F32)
    if dense:
        in_specs += [pl.BlockSpec((rows, D), lambda i: (i, 0)), tok_spec(doff), gate_spec(doff), tok_spec(doff),
                     g2_spec(doff)]
        args += [dense_rows, h2, gate, x1, g2]
    return pl.pallas_call(
        functools.partial(_expert_kernel, gtok=gtok, alpha=alpha, dense=dense),
        grid=(S,),
        in_specs=in_specs + [row_spec, row_spec],
        out_specs=[out_spec, out_spec] if dense else out_spec,
        out_shape=[out_shape, out_shape] if dense else out_shape,
        scratch_shapes=[pltpu.VMEM((gtok, per_tok // 8, 8, D), jnp.uint32),
                        pltpu.VMEM((gtok, per_tok // 8, 8, D), jnp.uint32),
                        pltpu.VMEM((gtok, D), F32),
                        pltpu.VMEM((gtok, D), F32),
                        pltpu.SemaphoreType.DMA((2, gtok))],
        compiler_params=_cparams(("arbitrary",)),
        name="experts",
    )(*args, lng, lnb)


def _expert_dense_kernel(rows_ref, h_ref, gate_ref, x1_ref, g2_ref, lng_ref, lnb_ref, o_ref, f_ref, *, gtok, alpha):
    per_tok = PEER_HEADS * PEER_TOPK
    gate_t = gate_ref[...].T
    for j in range(gtok):
        f_ref[j:j + 1, :] = _expert_token(rows_ref[j * per_tok:(j + 1) * per_tok, :], h_ref[j:j + 1, :],
                                          gate_t[:, j:j + 1])
    o_ref[...] = _ln(alpha * x1_ref[...] + g2_ref[0] * f_ref[...]) * lng_ref[...] + lnb_ref[...]


def _experts_dense(rows, gate, h2, x1, g2, lng, lnb, *, tok0, n_tok, row_tok0, gtok, seq, alpha):
    D = h2.shape[1]
    per_tok = PEER_HEADS * PEER_TOPK
    off = tok0 // gtok
    steps_per_batch = seq // gtok
    return pl.pallas_call(
        functools.partial(_expert_dense_kernel, gtok=gtok, alpha=alpha),
        grid=(n_tok // gtok,),
        in_specs=[pl.BlockSpec((gtok * per_tok, D), lambda i: (i + row_tok0 // gtok, 0)),
                  pl.BlockSpec((gtok, D), lambda i: (i + off, 0)),
                  pl.BlockSpec((gtok, per_tok), lambda i: (i + off, 0)),
                  pl.BlockSpec((gtok, D), lambda i: (i + off, 0)),
                  pl.BlockSpec((1, 1, D), lambda i: ((i + off) // steps_per_batch, 0, 0)),
                  pl.BlockSpec((1, D), lambda i: (0, 0)),
                  pl.BlockSpec((1, D), lambda i: (0, 0))],
        out_specs=pl.BlockSpec((gtok, D), lambda i: (i, 0)),
        out_shape=jax.ShapeDtypeStruct((n_tok, D), F32),
        scratch_shapes=[pltpu.VMEM((gtok, D), F32)],
        compiler_params=_cparams(("arbitrary",)),
        name="experts_dense",
    )(rows, h2, gate, x1, g2, lng, lnb)


SC_WORKERS = 32
SC_IDX_BLOCK = 128
SC_CHUNK = 32
SC_SHARE_DEN = 16
SC_SHARES_FIRST = (5,)
SC_SHARES_LATER = (5, 5)
SC_SHARES_LAST = (6,)


def _sc_gather_rows(tab, idx2d, row0, n_rows):
    from jax.experimental.pallas import tpu_sc as plsc
    D = tab.shape[1]
    steps = n_rows // SC_WORKERS
    n_chunks = SC_IDX_BLOCK // SC_CHUNK
    mesh = plsc.VectorSubcoreMesh(core_axis_name="core", subcore_axis_name="subcore")

    @pl.kernel(out_type=jax.ShapeDtypeStruct((n_rows * SC_IDX_BLOCK, D), tab.dtype), mesh=mesh,
               scratch_types=[pltpu.VMEM((1, SC_IDX_BLOCK), jnp.int32),
                              pltpu.VMEM((2, SC_CHUNK, D), tab.dtype),
                              pltpu.SemaphoreType.DMA((2,))])
    def k(x_hbm, i_hbm, o_hbm, i_vmem, buf, wsem):
        wid = lax.axis_index("core") * (SC_WORKERS // 2) + lax.axis_index("subcore")

        def write(row, c):
            return pltpu.make_async_copy(buf.at[c % 2], o_hbm.at[pl.ds(row * SC_IDX_BLOCK + c * SC_CHUNK, SC_CHUNK)],
                                         wsem.at[c % 2])

        @pl.loop(0, steps)
        def _(st):
            row = wid * steps + st
            pltpu.sync_copy(i_hbm.at[pl.ds(row0 + row, 1)], i_vmem)
            for c in range(n_chunks):
                if c >= 2:
                    write(row, c - 2).wait()
                pltpu.sync_copy(x_hbm.at[i_vmem.at[0, pl.ds(c * SC_CHUNK, SC_CHUNK)]], buf.at[c % 2])
                write(row, c).start()
            for c in range(n_chunks - 2, n_chunks):
                write(row, c).wait()
    return k(tab, idx2d)


def _rope_tables(L):
    half = HEAD_DIM // 4
    t = np.arange(L)
    freqs = ROPE_THETA ** (-np.arange(half, dtype=np.float32) / half)
    lane = np.arange(HEAD_DIM)
    pos = np.where(lane[None, :] < HEAD_DIM // 2, (t // GRID_W)[:, None], (t % GRID_W)[:, None]).astype(np.float32)
    ang = pos * freqs[lane % half][None, :].astype(np.float32)
    cos, sin = np.cos(ang), np.sin(ang)
    first = ((lane % (2 * half)) < half)[None, :]
    slo = np.where(first, -sin, 0.0)
    shi = np.where(first, 0.0, sin)
    tile = lambda a: jnp.asarray(np.tile(a, (1, N_KV_HEADS)), F32)
    return tile(cos), tile(slo), tile(shi)


def _pack_tables(u, v):
    ub = lax.bitcast_convert_type(u.astype(BF16), jnp.uint16).astype(jnp.uint32)
    vb = lax.bitcast_convert_type(v.astype(BF16), jnp.uint16).astype(jnp.uint32)
    return (ub << 16) | vb


def _pick(n, prefs):
    for p in prefs:
        if n % p == 0:
            return p
    return n


def kernel(x, c, ctx, c_ctx, w_mod, b_mod, w_in, q_norm_g, k_norm_g, gmlp_ln_g, gmlp_ln_b, gmlp_w_s, gmlp_b_s,
           conv_w_dw, conv_b_dw, conv_ln_g, conv_ln_b, conv_w_pw, conv_b_pw, w_out, ln1_g, ln1_b, ln2_g, ln2_b,
           peer_w_q, peer_sub_keys, peer_u, peer_v):
    B, L, D = x.shape
    LC = ctx.shape[1]
    depth = w_mod.shape[0]
    alpha = (2 * depth) ** 0.25
    gw = gmlp_ln_g.shape[-1]

    cos, slo, shi = _rope_tables(L)
    ones_c = jnp.ones((LC, KV_WIDTH), F32)
    zeros_c = jnp.zeros((LC, KV_WIDTH), F32)
    gidx = np.arange(ATTN_WIDTH) // HEAD_DIM
    gmat = jnp.asarray((gidx[:, None] == gidx[None, :]) / HEAD_DIM, BF16)

    cond = jnp.concatenate([c, c_ctx[None, :], jnp.zeros((-(B + 1) % 8, D), F32)], axis=0)

    t_in = _pick(L, (512, 256, 128))
    t_mix = _pick(L, (256, 128))
    t_inc = _pick(LC, (512, 256, 128))
    t_mixc = _pick(LC, (256, 128))
    tk_lat = _pick(L + LC, (768, 512, 384, 256, 128))
    tk_ctx = _pick(LC, (512, 256, 128))
    gtok = 16

    def retrieve(h2, wq_bf, keys_bf):
        n = h2.shape[0] * h2.shape[1]
        return _retrieval(h2.reshape(n, D), wq_bf, keys_bf, tile=_pick(n, (512, 256, 128)))

    def peer_start(h2, x1, g2, gate, eidx, l, seq, tab, sc_jobs):
        n = h2.shape[0] * h2.shape[1]
        sc_jobs = [j for j in sc_jobs if j > 0]
        n_self = n - sum(sc_jobs)
        starts = [n_self + sum(sc_jobs[:k]) for k in range(len(sc_jobs))]
        rows = [_sc_gather_rows(tab.reshape(-1, D), eidx, s, j) for s, j in zip(starts, sc_jobs)]
        return dict(h2=h2, x1=x1, g2=g2, gate=gate, eidx=eidx, l=l, seq=seq, tab=tab, jobs=sc_jobs, starts=starts,
                    rows=rows, n_self=n_self)

    def peer_finish(st):
        h2, l = st["h2"], st["l"]
        n = h2.shape[0] * h2.shape[1]
        h2f, x1f = h2.reshape(n, D), st["x1"].reshape(n, D)
        lng, lnb = ln2_g[l].reshape(1, D), ln2_b[l].reshape(1, D)
        gate, g2, n_self = st["gate"], st["g2"], st["n_self"]
        eidx3 = st["eidx"].reshape(n // gtok, 1, -1)
        common = dict(gtok=gtok, seq=st["seq"], alpha=alpha)
        args = (gate, st["tab"], h2f, x1f, g2, lng, lnb)
        outs_self, outs_sc, done = [], [[] for _ in st["jobs"]], 0
        for k, (s, j) in enumerate(zip(st["starts"], st["jobs"])):
            m = min(n_self - done, j)
            if m:
                both = _experts(eidx3, *args, tok0=done, n_tok=m, dense_rows=st["rows"][k], dense_tok0=s, **common)
                outs_self.append(both[0])
                outs_sc[k].append(both[1])
                done += m
            if j > m:
                outs_sc[k].append(_experts_dense(st["rows"][k], gate, h2f, x1f, g2, lng, lnb, tok0=s + m,
                                                 n_tok=j - m, row_tok0=m, **common))
        if n_self > done:
            outs_self.append(_experts(eidx3, *args, tok0=done, n_tok=n_self - done, **common))
        outs = outs_self + [o for per_job in outs_sc for o in per_job]
        return (outs[0] if len(outs) == 1 else jnp.concatenate(outs, axis=0)).reshape(h2.shape)

    def sc_share(n, num):
        align = SC_WORKERS * gtok
        return n * num // SC_SHARE_DEN // align * align if n >= 8192 else 0

    n_groups = 2 if B % 2 == 0 else 1
    bg = B // n_groups
    groups = [slice(g * bg, (g + 1) * bg) for g in range(n_groups)]
    x_grp = [x[bs] for bs in groups]
    x_ctx = ctx
    pending = None
    for l in range(depth):
        last = l == depth - 1
        mod = _modulation(cond, w_mod[l], b_mod[l])
        m_lat = [mod[:B, k * D:(k + 1) * D].reshape(B, 1, D) for k in range(6)]
        m_ctx = [jnp.broadcast_to(mod[B:B + 1, k * D:(k + 1) * D].reshape(1, 1, D), (B, 1, D)) for k in range(6)]

        w_in_bf = w_in[l].astype(BF16)
        gq = jnp.tile(q_norm_g[l], N_Q_HEADS).reshape(1, ATTN_WIDTH)
        gk = jnp.tile(k_norm_g[l], N_KV_HEADS).reshape(1, KV_WIDTH)
        mix_prm = (gmlp_ln_g[l].reshape(1, gw), gmlp_ln_b[l].reshape(1, gw),
                   gmlp_w_s[l].astype(BF16),
                   jnp.repeat(gmlp_b_s[l].T, gw // GMLP_GROUPS, axis=1),
                   conv_w_dw[l], conv_b_dw[l].reshape(1, -1), conv_ln_g[l].reshape(1, -1),
                   conv_ln_b[l].reshape(1, -1), conv_w_pw[l].astype(BF16), conv_b_pw[l].reshape(1, -1),
                   w_out[l].astype(BF16), ln1_g[l].reshape(1, D), ln1_b[l].reshape(1, D))
        wq_bf = peer_w_q[l].astype(BF16)
        keys_bf = peer_sub_keys[l].reshape(2 * PEER_HEADS, N_KEYS, PEER_HALF).astype(BF16)
        tab = _pack_tables(peer_u[l], peer_v[l]).reshape(-1, 1, D)

        q_c, kt_c, v_c, zg_c, zc_c = _inproj(x_ctx, m_ctx[0], m_ctx[1], w_in_bf, gq, gk, ones_c, zeros_c, zeros_c,
                                              gmat, tile=t_inc)
        started = []
        for g, bs in enumerate(groups):
            if g == n_groups - 1 and pending is not None:
                x_grp[g] = peer_finish(pending)
                pending = None
            xg = x_grp[g]
            q_l, kt_l, v_l, zg_l, zc_l = _inproj(xg, m_lat[0][bs], m_lat[1][bs], w_in_bf, gq, gk, cos, slo, shi, gmat,
                                                  tile=t_in)
            kt_all = jnp.concatenate([kt_c[bs], kt_l], axis=3)
            v_all = jnp.concatenate([v_c[bs], v_l], axis=1)
            o_lat = _attention(q_l, kt_all, v_all, tq=128, tk=tk_lat)
            x1, h2 = _mixers(xg, o_lat, zg_l, zc_l, m_lat[2][bs], m_lat[3][bs], m_lat[4][bs], mix_prm, tile=t_mix,
                             alpha=alpha)
            gate, eidx = retrieve(h2, wq_bf, keys_bf)
            if g < n_groups - 1:
                shares = SC_SHARES_FIRST
            else:
                shares = SC_SHARES_LAST if last or n_groups == 1 else SC_SHARES_LATER
            started.append(peer_start(h2, x1, m_lat[5][bs], gate, eidx, l, L, tab,
                                      [sc_share(bg * L, k) for k in shares]))
        for g in range(n_groups - 1):
            x_grp[g] = peer_finish(started[g])
        if last or n_groups == 1:
            x_grp[n_groups - 1] = peer_finish(started[n_groups - 1])
        else:
            pending = started[n_groups - 1]

        if not last:
            o_ctx = _attention(q_c, kt_c, v_c, tq=128, tk=tk_ctx)
            x1c, h2c = _mixers(x_ctx, o_ctx, zg_c, zc_c, m_ctx[2], m_ctx[3], m_ctx[4], mix_prm, tile=t_mixc,
                               alpha=alpha)
            gate_c, eidx_c = retrieve(h2c, wq_bf, keys_bf)
            x_ctx = peer_finish(peer_start(h2c, x1c, m_ctx[5], gate_c, eidx_c, l, LC, tab, []))
    return jnp.concatenate(x_grp, axis=0)
```

```python
import functools
import math

import jax
import jax.numpy as jnp
import numpy as np
from jax import lax
from jax.experimental import pallas as pl
from jax.experimental.pallas import tpu as pltpu

F32 = jnp.float32
BF16 = jnp.bfloat16

EPS = 1e-6
GRID_W = 64
N_Q_HEADS = 8
N_KV_HEADS = 2
Q_PER_KV = N_Q_HEADS // N_KV_HEADS
HEAD_DIM = 64
ATTN_WIDTH = N_Q_HEADS * HEAD_DIM
KV_WIDTH = N_KV_HEADS * HEAD_DIM
QKV_WIDTH = ATTN_WIDTH + 2 * KV_WIDTH
ROPE_THETA = 10000.0
CHUNK = 128
GMLP_GROUPS = 4
CONV_K = 31
CONV_HALO = 16
PEER_HEADS = 8
N_KEYS = 128
PEER_TOPK = 16
PEER_HALF = 128

VMEM_LIMIT = 56 * 1024 * 1024


def _cparams(sem):
    return pltpu.CompilerParams(dimension_semantics=sem, vmem_limit_bytes=VMEM_LIMIT)


def _gelu_tanh(x):
    return 0.5 * x * (1.0 + jnp.tanh(math.sqrt(2.0 / math.pi) * (x + 0.044715 * (x * x * x))))


def _ln(x):
    mu = jnp.mean(x, axis=-1, keepdims=True)
    xc = x - mu
    var = jnp.mean(xc * xc, axis=-1, keepdims=True)
    return xc * lax.rsqrt(var + EPS)


def _mod_kernel(c_ref, w_ref, b_ref, o_ref):
    c = c_ref[...]
    s = c * jax.nn.sigmoid(c)
    o_ref[...] = jnp.dot(s, w_ref[...], preferred_element_type=F32,
                         precision=lax.Precision.HIGHEST) + b_ref[...]


def _modulation(cond, w_mod, b_mod):
    R, D = cond.shape
    N = w_mod.shape[1]
    tn = 1024
    return pl.pallas_call(
        _mod_kernel,
        grid=(N // tn,),
        in_specs=[pl.BlockSpec((R, D), lambda j: (0, 0)),
                  pl.BlockSpec((D, tn), lambda j: (0, j)),
                  pl.BlockSpec((1, tn), lambda j: (0, j))],
        out_specs=pl.BlockSpec((R, tn), lambda j: (0, j)),
        out_shape=jax.ShapeDtypeStruct((R, N), F32),
        compiler_params=_cparams(("arbitrary",)),
        name="modulation",
    )(cond, w_mod, b_mod.reshape(1, N))


def _group_mean_sq(t, gmat):
    sq = t * t
    hi = sq.astype(BF16)
    lo = (sq - hi.astype(F32)).astype(BF16)
    return (jnp.dot(hi, gmat, preferred_element_type=F32)
            + jnp.dot(lo, gmat, preferred_element_type=F32))


def _rope(t, cos, sin_lo, sin_hi):
    w = t.shape[-1]
    up = pltpu.roll(t, w - 16, axis=1)
    dn = pltpu.roll(t, 16, axis=1)
    return t * cos + up * sin_lo + dn * sin_hi


def _inproj_kernel(x_ref, sh_ref, sc_ref, w_ref, gq_ref, gk_ref, cos_ref, slo_ref, shi_ref, gmat_ref,
                   q_ref, kt_ref, v_ref, zg_ref, zc_ref):
    h = _ln(x_ref[0]) * (1.0 + sc_ref[0]) + sh_ref[0]
    p = jnp.dot(h.astype(BF16), w_ref[...], preferred_element_type=F32)
    q = p[:, :ATTN_WIDTH]
    k = p[:, ATTN_WIDTH:ATTN_WIDTH + KV_WIDTH]
    v = p[:, ATTN_WIDTH + KV_WIDTH:QKV_WIDTH]
    gm_w = (p.shape[1] - QKV_WIDTH) // 2
    zg_ref[0] = p[:, QKV_WIDTH:QKV_WIDTH + gm_w]
    zc_ref[0] = p[:, QKV_WIDTH + gm_w:]
    v_ref[0] = v.astype(BF16)

    gmat = gmat_ref[...]
    cos, slo, shi = cos_ref[...], slo_ref[...], shi_ref[...]
    qn = q * lax.rsqrt(_group_mean_sq(q, gmat) + EPS) * gq_ref[...]
    rep = ATTN_WIDTH // KV_WIDTH
    qr = _rope(qn, jnp.concatenate([cos] * rep, axis=1), jnp.concatenate([slo] * rep, axis=1),
               jnp.concatenate([shi] * rep, axis=1))
    qr = (qr * (HEAD_DIM ** -0.5 * math.log2(math.e))).astype(BF16)
    for hh in range(N_Q_HEADS):
        q_ref[0, hh] = qr[:, hh * HEAD_DIM:(hh + 1) * HEAD_DIM]

    kn = k * lax.rsqrt(_group_mean_sq(k, gmat[:KV_WIDTH, :KV_WIDTH]) + EPS) * gk_ref[...]
    kt = _rope(kn, cos, slo, shi).T
    for g in range(N_KV_HEADS):
        kt_ref[0, g] = kt[g * HEAD_DIM:(g + 1) * HEAD_DIM, :].astype(BF16)


def _inproj(x, sh, sc, w_in_bf, gq, gk, cos, slo, shi, gmat, *, tile):
    B, L, D = x.shape
    W = w_in_bf.shape[1]
    gm_w = (W - QKV_WIDTH) // 2
    nt = L // tile
    bmap = lambda b, i: (b, 0, 0)
    cmap = lambda b, i: (0, 0)
    return pl.pallas_call(
        _inproj_kernel,
        grid=(B, nt),
        in_specs=[pl.BlockSpec((1, tile, D), lambda b, i: (b, i, 0)),
                  pl.BlockSpec((1, 1, D), bmap), pl.BlockSpec((1, 1, D), bmap),
                  pl.BlockSpec((D, W), cmap),
                  pl.BlockSpec((1, ATTN_WIDTH), cmap), pl.BlockSpec((1, KV_WIDTH), cmap),
                  pl.BlockSpec((tile, KV_WIDTH), lambda b, i: (i, 0)),
                  pl.BlockSpec((tile, KV_WIDTH), lambda b, i: (i, 0)),
                  pl.BlockSpec((tile, KV_WIDTH), lambda b, i: (i, 0)),
                  pl.BlockSpec((ATTN_WIDTH, ATTN_WIDTH), cmap)],
        out_specs=[pl.BlockSpec((1, N_Q_HEADS, tile, HEAD_DIM), lambda b, i: (b, 0, i, 0)),
                   pl.BlockSpec((1, N_KV_HEADS, HEAD_DIM, tile), lambda b, i: (b, 0, 0, i)),
                   pl.BlockSpec((1, tile, KV_WIDTH), lambda b, i: (b, i, 0)),
                   pl.BlockSpec((1, tile, gm_w), lambda b, i: (b, i, 0)),
                   pl.BlockSpec((1, tile, gm_w), lambda b, i: (b, i, 0))],
        out_shape=[jax.ShapeDtypeStruct((B, N_Q_HEADS, L, HEAD_DIM), BF16),
                   jax.ShapeDtypeStruct((B, N_KV_HEADS, HEAD_DIM, L), BF16),
                   jax.ShapeDtypeStruct((B, L, KV_WIDTH), BF16),
                   jax.ShapeDtypeStruct((B, L, gm_w), F32),
                   jax.ShapeDtypeStruct((B, L, gm_w), F32)],
        compiler_params=_cparams(("arbitrary", "arbitrary")),
        name="inproj",
    )(x, sh, sc, w_in_bf, gq, gk, cos, slo, shi, gmat)


def _attn_kernel(q_ref, kt_ref, vx_ref, o_ref, p_ref, *, tq, tk, nk):
    m_rows = Q_PER_KV * tq
    qs = [q_ref[0, Q_PER_KV * g:Q_PER_KV * (g + 1)].reshape(m_rows, HEAD_DIM) for g in range(N_KV_HEADS)]

    def pv(c, par, g):
        off = pl.multiple_of(c * tk, tk)
        return jnp.dot(p_ref[par, g], vx_ref[0, g, pl.ds(off, tk), :], preferred_element_type=F32)

    def step(c, par, carry):
        off = pl.multiple_of(c * tk, tk)
        out = []
        for g in range(N_KV_HEADS):
            m, acc = carry[g]
            s = jnp.dot(qs[g], kt_ref[0, g, :, pl.ds(off, tk)], preferred_element_type=F32)
            m_new = jnp.maximum(m, jnp.max(s, axis=-1, keepdims=True))
            p_ref[1 - par, g] = jnp.exp2(s - m_new).astype(BF16)
            if acc is not None:
                acc = jnp.exp2(m - m_new) * (acc + pv(c - 1, par, g))
            out.append((m_new, acc))
        return tuple(out)

    first = step(0, 1, tuple((jnp.full((m_rows, 1), -1e30, F32), None) for _ in range(N_KV_HEADS)))
    carry = tuple((m, jnp.zeros((m_rows, 2 * HEAD_DIM), F32)) for m, _ in first)

    n_pairs = (nk - 1) // 2
    def pair(t, carry):
        carry = step(2 * t + 1, 0, carry)
        return step(2 * t + 2, 1, carry)
    carry = lax.fori_loop(0, n_pairs, pair, carry)
    par = 0
    if (nk - 1) % 2:
        carry = step(nk - 1, 0, carry)
        par = 1

    for g in range(N_KV_HEADS):
        acc = carry[g][1] + pv(nk - 1, par, g)
        og = acc / pltpu.roll(acc, HEAD_DIM, axis=1)
        for r in range(Q_PER_KV):
            hh = Q_PER_KV * g + r
            o_ref[0, :, hh * HEAD_DIM:(hh + 1) * HEAD_DIM] = og[r * tq:(r + 1) * tq, :HEAD_DIM].astype(o_ref.dtype)


def _attention(q, kt, v, *, tq, tk):
    B, _, L, _ = q.shape
    LK = kt.shape[-1]
    ones = jnp.ones((B, LK, HEAD_DIM), v.dtype)
    vx = jnp.stack([jnp.concatenate([v[..., g * HEAD_DIM:(g + 1) * HEAD_DIM], ones], axis=-1)
                    for g in range(N_KV_HEADS)], axis=1)
    return pl.pallas_call(
        functools.partial(_attn_kernel, tq=tq, tk=tk, nk=LK // tk),
        grid=(B, L // tq),
        in_specs=[pl.BlockSpec((1, N_Q_HEADS, tq, HEAD_DIM), lambda b, i: (b, 0, i, 0)),
                  pl.BlockSpec((1, N_KV_HEADS, HEAD_DIM, LK), lambda b, i: (b, 0, 0, 0)),
                  pl.BlockSpec((1, N_KV_HEADS, LK, 2 * HEAD_DIM), lambda b, i: (b, 0, 0, 0))],
        out_specs=pl.BlockSpec((1, tq, ATTN_WIDTH), lambda b, i: (b, i, 0)),
        out_shape=jax.ShapeDtypeStruct((B, L, ATTN_WIDTH), BF16),
        scratch_shapes=[pltpu.VMEM((2, N_KV_HEADS, Q_PER_KV * tq, tk), BF16)],
        compiler_params=_cparams(("arbitrary", "arbitrary")),
        name="attention",
    )(q, kt, vx)


def _mix_kernel(x_ref, o_ref, zg_ref, zc_ref, zp_ref, zn_ref, g1_ref, sh2_ref, sc2_ref,
                gln_g_ref, gln_b_ref, ws_ref, bs_ref, wdw_ref, bdw_ref, cln_g_ref, cln_b_ref,
                wpw_ref, bpw_ref, wout_ref, ln1g_ref, ln1b_ref,
                x1_ref, h2_ref, ybuf_ref, *, tile, alpha):
    i = pl.program_id(1)
    nt = pl.num_programs(1)
    gw = zg_ref.shape[-1] // 2
    cw = zc_ref.shape[-1] // 2

    z = _gelu_tanh(zg_ref[0])
    u, v = z[:, :gw], z[:, gw:]
    vn = (_ln(v) * gln_g_ref[...] + gln_b_ref[...]).astype(BF16)
    grp = lax.broadcasted_iota(jnp.int32, (CHUNK, gw), 1) // (gw // GMLP_GROUPS)
    sg = []
    for n in range(tile // CHUNK):
        vc = vn[n * CHUNK:(n + 1) * CHUNK]
        s = jnp.zeros((CHUNK, gw), F32)
        for g in range(GMLP_GROUPS):
            s = jnp.where(grp == g, jnp.dot(ws_ref[g], vc, preferred_element_type=F32), s)
        sg.append(s + bs_ref[...])
    y_g = u * jnp.concatenate(sg, axis=0)

    def glu(zz):
        return zz[:, :cw] * jax.nn.sigmoid(zz[:, cw:])

    ybuf_ref[0:CONV_HALO] = glu(zp_ref[0]) * (i > 0).astype(F32)
    ybuf_ref[CONV_HALO:CONV_HALO + tile] = glu(zc_ref[0])
    ybuf_ref[CONV_HALO + tile:2 * CONV_HALO + tile] = glu(zn_ref[0]) * (i < nt - 1).astype(F32)
    base = CONV_HALO - CONV_K // 2
    conv = jnp.zeros((tile, cw), F32)
    for kk in range(CONV_K):
        conv = conv + ybuf_ref[base + kk:base + kk + tile, :] * wdw_ref[kk:kk + 1, :]
    conv = conv + bdw_ref[...]
    t = _ln(conv) * cln_g_ref[...] + cln_b_ref[...]
    t = t * jax.nn.sigmoid(t)
    y_c = jnp.dot(t.astype(BF16), wpw_ref[...], preferred_element_type=F32) + bpw_ref[...]

    aw = o_ref.shape[-1]
    y = (jnp.dot(o_ref[0], wout_ref[0:aw, :], preferred_element_type=F32)
         + jnp.dot(y_g.astype(BF16), wout_ref[aw:aw + gw, :], preferred_element_type=F32)
         + jnp.dot(y_c.astype(BF16), wout_ref[aw + gw:, :], preferred_element_type=F32))
    x1 = _ln(alpha * x_ref[0] + g1_ref[0] * y) * ln1g_ref[...] + ln1b_ref[...]
    x1_ref[0] = x1
    h2_ref[0] = _ln(x1) * (1.0 + sc2_ref[0]) + sh2_ref[0]


def _mixers(x, o, zg, zc, g1, sh2, sc2, prm, *, tile, alpha):
    B, L, D = x.shape
    nt = L // tile
    hb = tile // CONV_HALO
    nhb = L // CONV_HALO
    gw2, cw2 = zg.shape[-1], zc.shape[-1]
    tmap = lambda b, i: (b, i, 0)
    bmap = lambda b, i: (b, 0, 0)
    c2 = lambda b, i: (0, 0)
    c3 = lambda b, i: (0, 0, 0)
    in_specs = [pl.BlockSpec((1, tile, D), tmap),
                pl.BlockSpec((1, tile, o.shape[-1]), tmap),
                pl.BlockSpec((1, tile, gw2), tmap),
                pl.BlockSpec((1, tile, cw2), tmap),
                pl.BlockSpec((1, CONV_HALO, cw2), lambda b, i: (b, jnp.maximum(i * hb - 1, 0), 0)),
                pl.BlockSpec((1, CONV_HALO, cw2), lambda b, i: (b, jnp.minimum((i + 1) * hb, nhb - 1), 0)),
                pl.BlockSpec((1, 1, D), bmap), pl.BlockSpec((1, 1, D), bmap), pl.BlockSpec((1, 1, D), bmap)]
    for a in prm:
        in_specs.append(pl.BlockSpec(a.shape, c2 if a.ndim == 2 else c3))
    return pl.pallas_call(
        functools.partial(_mix_kernel, tile=tile, alpha=alpha),
        grid=(B, nt),
        in_specs=in_specs,
        out_specs=[pl.BlockSpec((1, tile, D), tmap), pl.BlockSpec((1, tile, D), tmap)],
        out_shape=[jax.ShapeDtypeStruct((B, L, D), F32), jax.ShapeDtypeStruct((B, L, D), F32)],
        scratch_shapes=[pltpu.VMEM((tile + 2 * CONV_HALO, cw2 // 2), F32)],
        compiler_params=_cparams(("arbitrary", "arbitrary")),
        name="mixers",
    )(x, o, zg, zc, zc, zc, g1, sh2, sc2, *prm)


def _oddeven_merge_sort_pairs(n):
    pairs = []
    p = 1
    while p < n:
        k = p
        while k >= 1:
            for j in range(k % p, n - k, 2 * k):
                for i in range(min(k, n - j - k)):
                    if (i + j) // (2 * p) == (i + j + k) // (2 * p):
                        pairs.append((i + j, i + j + k))
            k //= 2
        p *= 2
    return pairs


_SORT16 = _oddeven_merge_sort_pairs(16)
_BITONIC16 = [(i, i + d) for d in (8, 4, 2, 1) for i in range(16) if (i & d) == 0]


def _before(a, ia, b, ib):
    return (a > b) | ((a == b) & (ia < ib))


def _cmpx(vals, keys, pay, i, j):
    keep = _before(vals[i], keys[i], vals[j], keys[j])
    vals[i], vals[j] = jnp.where(keep, vals[i], vals[j]), jnp.where(keep, vals[j], vals[i])
    keys[i], keys[j] = jnp.where(keep, keys[i], keys[j]), jnp.where(keep, keys[j], keys[i])
    if pay is not None:
        pay[i], pay[j] = jnp.where(keep, pay[i], pay[j]), jnp.where(keep, pay[j], pay[i])


def _top16_merge(av, ak, ap, bv, bk, bp):
    n = len(av)
    ov, ok, op = [], [], []
    for i in range(n):
        keep = _before(av[i], ak[i], bv[n - 1 - i], bk[n - 1 - i])
        ov.append(jnp.where(keep, av[i], bv[n - 1 - i]))
        ok.append(jnp.where(keep, ak[i], bk[n - 1 - i]))
        if ap is not None:
            op.append(jnp.where(keep, ap[i], bp[n - 1 - i]))
    return ov, ok, (op if ap is not None else None)


def _stage2_candidates():
    return [(a, b) for a in range(PEER_TOPK) for b in range(PEER_TOPK) if (a + 1) * (b + 1) <= PEER_TOPK]


def _retr_kernel(h_ref, wq_ref, keys_ref, gate_ref, eidx_ref, q_s, sv_s, si_s, *, tile):
    q_s[...] = jnp.dot(h_ref[...].astype(BF16), wq_ref[...], preferred_element_type=F32)
    sub = lax.broadcasted_iota(jnp.int32, (8, tile), 0)
    sv_s[...] = jnp.zeros(sv_s.shape, F32)
    si_s[...] = jnp.zeros(si_s.shape, jnp.int32)

    def stage1(hp, _):
        qh = q_s[:, pl.ds(pl.multiple_of(hp * PEER_HALF, PEER_HALF), PEER_HALF)].astype(BF16)
        st = lax.dot_general(keys_ref[hp], qh, (((1,), (1,)), ((), ())),
                             preferred_element_type=F32)
        st = st.reshape(N_KEYS // 8, 8, tile)
        vals = [st[i] for i in range(16)]
        keys = [sub + 8 * i for i in range(16)]
        for (i, j) in _SORT16:
            _cmpx(vals, keys, None, i, j)
        for shift in (4, 2, 1):
            bv = [pltpu.roll(t, shift, axis=0) for t in vals]
            bk = [pltpu.roll(t, shift, axis=0) for t in keys]
            vals, keys, _ = _top16_merge(vals, keys, None, bv, bk, None)
            for (i, j) in _BITONIC16:
                _cmpx(vals, keys, None, i, j)
        head = hp // 2
        half = hp % 2
        for i in range(16):
            sv_s[half, i] = jnp.where(sub == head, vals[i], sv_s[half, i])
            si_s[half, i] = jnp.where(sub == head, keys[i], si_s[half, i])
        return 0

    lax.fori_loop(0, 2 * PEER_HEADS, stage1, 0)

    cands = _stage2_candidates()
    n_pad = 64
    vals, poss, exps = [], [], []
    for (a, b) in cands:
        vals.append(sv_s[0, a] + sv_s[1, b])
        poss.append(jnp.full((8, tile), a * PEER_TOPK + b, jnp.int32))
        exps.append(si_s[0, a] * N_KEYS + si_s[1, b])
    for t in range(n_pad - len(cands)):
        vals.append(jnp.full((8, tile), -jnp.inf, F32))
        poss.append(jnp.full((8, tile), PEER_TOPK * PEER_TOPK + t, jnp.int32))
        exps.append(jnp.zeros((8, tile), jnp.int32))
    groups = []
    for g in range(n_pad // 16):
        gv, gk, gp = vals[16 * g:16 * g + 16], poss[16 * g:16 * g + 16], exps[16 * g:16 * g + 16]
        for (i, j) in _SORT16:
            _cmpx(gv, gk, gp, i, j)
        groups.append((gv, gk, gp))
    while len(groups) > 1:
        nxt = []
        for g in range(0, len(groups), 2):
            mv, mk, mp = _top16_merge(*groups[g], *groups[g + 1])
            if len(groups) > 2:
                for (i, j) in _BITONIC16:
                    _cmpx(mv, mk, mp, i, j)
            nxt.append((mv, mk, mp))
        groups = nxt
    best, _, experts = groups[0]

    mx = functools.reduce(jnp.maximum, best)
    ex = [jnp.exp(b - mx) for b in best]
    inv = 1.0 / functools.reduce(lambda a, b: a + b, ex)
    gate_ref[...] = jnp.concatenate([e * inv for e in ex], axis=0).T
    eidx_ref[...] = jnp.concatenate(experts, axis=0).T


def _retrieval(h2, wq_bf, keys_bf, *, tile):
    N, D = h2.shape
    QW = wq_bf.shape[1]
    return pl.pallas_call(
        functools.partial(_retr_kernel, tile=tile),
        grid=(N // tile,),
        in_specs=[pl.BlockSpec((tile, D), lambda i: (i, 0)),
                  pl.BlockSpec((D, QW), lambda i: (0, 0)),
                  pl.BlockSpec(keys_bf.shape, lambda i: (0, 0, 0))],
        out_specs=[pl.BlockSpec((tile, PEER_TOPK * PEER_HEADS), lambda i: (i, 0)),
                   pl.BlockSpec((tile, PEER_TOPK * PEER_HEADS), lambda i: (i, 0))],
        out_shape=[jax.ShapeDtypeStruct((N, PEER_TOPK * PEER_HEADS), F32),
                   jax.ShapeDtypeStruct((N, PEER_TOPK * PEER_HEADS), jnp.int32)],
        scratch_shapes=[pltpu.VMEM((tile, QW), F32),
                        pltpu.VMEM((2, PEER_TOPK, PEER_HEADS, tile), F32),
                        pltpu.VMEM((2, PEER_TOPK, PEER_HEADS, tile), jnp.int32)],
        compiler_params=_cparams(("arbitrary",)),
        name="retrieval",
    )(h2, wq_bf, keys_bf)


def _expert_token(r, h_row, gate_col):
    u = pltpu.bitcast(r & jnp.uint32(0xFFFF0000), F32)
    v = pltpu.bitcast(r << 16, F32)
    d = jnp.sum(u * h_row, axis=-1, keepdims=True)
    w = jnp.broadcast_to(gate_col * _gelu_tanh(d), r.shape)
    return jnp.sum(v * w, axis=0, keepdims=True)


def _expert_kernel(*refs, gtok, alpha, dense):
    if dense:
        (idx0_ref, idxn_ref, tab_ref, h_ref, gate_ref, x1_ref, g2_ref, rows_ref, hd_ref, gated_ref, x1d_ref, g2d_ref,
         lng_ref, lnb_ref, o_ref, od_ref, buf_a, buf_b, f_ref, fd_ref, sem_ref) = refs
    else:
        (idx0_ref, idxn_ref, tab_ref, h_ref, gate_ref, x1_ref, g2_ref, lng_ref, lnb_ref,
         o_ref, buf_a, buf_b, f_ref, fd_ref, sem_ref) = refs
    i = pl.program_id(0)
    n_steps = pl.num_programs(0)
    per_tok = PEER_HEADS * PEER_TOPK
    d_model = h_ref.shape[-1]

    def start_token(idx_ref, buf, sem_row, j):
        for g in range(per_tok // 8):
            for t in range(8):
                e = idx_ref[0, 0, j * per_tok + g * 8 + t]
                pltpu.make_async_copy(tab_ref.at[e], buf.at[j, g, pl.ds(t, 1), :],
                                      sem_ref.at[sem_row, j]).start(priority=t % 2)

    def wait_token(buf, other, sem_row, j):
        pltpu.make_async_copy(other.at[j], buf.at[j], sem_ref.at[sem_row, j]).wait()

    def token(j, cur, nxt, cur_row):
        wait_token(cur, nxt, cur_row, j)
        start_token(idxn_ref, nxt, 1 - cur_row, j)
        f_ref[j:j + 1, :] = _expert_token(cur[j].reshape(per_tok, d_model), h_ref[j:j + 1, :],
                                          gate_t[:, j:j + 1])
        if dense:
            fd_ref[j:j + 1, :] = _expert_token(rows_ref[j * per_tok:(j + 1) * per_tok, :], hd_ref[j:j + 1, :],
                                               gated_t[:, j:j + 1])

    @pl.when(i == 0)
    def _():
        def prime(j, _):
            start_token(idx0_ref, buf_a, 0, j)
            return 0
        lax.fori_loop(0, gtok, prime, 0)

    gate_t = gate_ref[...].T
    gated_t = gated_ref[...].T if dense else None

    @pl.when(i % 2 == 0)
    def _():
        for j in range(gtok):
            token(j, buf_a, buf_b, 0)

    @pl.when(i % 2 == 1)
    def _():
        for j in range(gtok):
            token(j, buf_b, buf_a, 1)

    @pl.when((i == n_steps - 1) & (i % 2 == 0))
    def _():
        lax.fori_loop(0, gtok, lambda j, c: (wait_token(buf_b, buf_a, 1, j), 0)[1], 0)

    @pl.when((i == n_steps - 1) & (i % 2 == 1))
    def _():
        lax.fori_loop(0, gtok, lambda j, c: (wait_token(buf_a, buf_b, 0, j), 0)[1], 0)

    o_ref[...] = _ln(alpha * x1_ref[...] + g2_ref[0] * f_ref[...]) * lng_ref[...] + lnb_ref[...]

    if dense:
        od_ref[...] = _ln(alpha * x1d_ref[...] + g2d_ref[0] * fd_ref[...]) * lng_ref[...] + lnb_ref[...]


def _experts(eidx, gate, tab, h2, x1, g2, lng, lnb, *, tok0, n_tok, gtok, seq, alpha, dense_rows=None,
             dense_tok0=0):
    D = h2.shape[1]
    S = n_tok // gtok
    off = tok0 // gtok
    doff = dense_tok0 // gtok
    per_tok = PEER_HEADS * PEER_TOPK
    rows = gtok * per_tok
    spb = seq // gtok
    smem = pltpu.MemorySpace.SMEM
    dense = dense_rows is not None
    tok_spec = lambda o: pl.BlockSpec((gtok, D), lambda i: (i + o, 0))
    gate_spec = lambda o: pl.BlockSpec((gtok, per_tok), lambda i: (i + o, 0))
    g2_spec = lambda o: pl.BlockSpec((1, 1, D), lambda i: ((i + o) // spb, 0, 0))
    row_spec = pl.BlockSpec((1, D), lambda i: (0, 0))
    in_specs = [pl.BlockSpec((1, 1, rows), lambda i: (off, 0, 0), memory_space=smem),
                pl.BlockSpec((1, 1, rows), lambda i: (jnp.minimum(i + 1, S - 1) + off, 0, 0), memory_space=smem),
                pl.BlockSpec(memory_space=pl.ANY),
                tok_spec(off), gate_spec(off), tok_spec(off), g2_spec(off)]
    args = [eidx, eidx, tab, h2, gate, x1, g2]
    out_spec = pl.BlockSpec((gtok, D), lambda i: (i, 0))
    out_shape = jax.ShapeDtypeStruct((n_tok, D), F32)
    if dense:
        in_specs += [pl.BlockSpec((rows, D), lambda i: (i, 0)), tok_spec(doff), gate_spec(doff), tok_spec(doff),
                     g2_spec(doff)]
        args += [dense_rows, h2, gate, x1, g2]
    return pl.pallas_call(
        functools.partial(_expert_kernel, gtok=gtok, alpha=alpha, dense=dense),
        grid=(S,),
        in_specs=in_specs + [row_spec, row_spec],
        out_specs=[out_spec, out_spec] if dense else out_spec,
        out_shape=[out_shape, out_shape] if dense else out_shape,
        scratch_shapes=[pltpu.VMEM((gtok, per_tok // 8, 8, D), jnp.uint32),
                        pltpu.VMEM((gtok, per_tok // 8, 8, D), jnp.uint32),
                        pltpu.VMEM((gtok, D), F32),
                        pltpu.VMEM((gtok, D), F32),
                        pltpu.SemaphoreType.DMA((2, gtok))],
        compiler_params=_cparams(("arbitrary",)),
        name="experts",
    )(*args, lng, lnb)


def _expert_dense_kernel(rows_ref, h_ref, gate_ref, x1_ref, g2_ref, lng_ref, lnb_ref, o_ref, f_ref, *, gtok, alpha):
    per_tok = PEER_HEADS * PEER_TOPK
    gate_t = gate_ref[...].T
    for j in range(gtok):
        f_ref[j:j + 1, :] = _expert_token(rows_ref[j * per_tok:(j + 1) * per_tok, :], h_ref[j:j + 1, :],
                                          gate_t[:, j:j + 1])
    o_ref[...] = _ln(alpha * x1_ref[...] + g2_ref[0] * f_ref[...]) * lng_ref[...] + lnb_ref[...]


def _experts_dense(rows, gate, h2, x1, g2, lng, lnb, *, tok0, n_tok, row_tok0, gtok, seq, alpha):
    D = h2.shape[1]
    per_tok = PEER_HEADS * PEER_TOPK
    off = tok0 // gtok
    steps_per_batch = seq // gtok
    return pl.pallas_call(
        functools.partial(_expert_dense_kernel, gtok=gtok, alpha=alpha),
        grid=(n_tok // gtok,),
        in_specs=[pl.BlockSpec((gtok * per_tok, D), lambda i: (i + row_tok0 // gtok, 0)),
                  pl.BlockSpec((gtok, D), lambda i: (i + off, 0)),
                  pl.BlockSpec((gtok, per_tok), lambda i: (i + off, 0)),
                  pl.BlockSpec((gtok, D), lambda i: (i + off, 0)),
                  pl.BlockSpec((1, 1, D), lambda i: ((i + off) // steps_per_batch, 0, 0)),
                  pl.BlockSpec((1, D), lambda i: (0, 0)),
                  pl.BlockSpec((1, D), lambda i: (0, 0))],
        out_specs=pl.BlockSpec((gtok, D), lambda i: (i, 0)),
        out_shape=jax.ShapeDtypeStruct((n_tok, D), F32),
        scratch_shapes=[pltpu.VMEM((gtok, D), F32)],
        compiler_params=_cparams(("arbitrary",)),
        name="experts_dense",
    )(rows, h2, gate, x1, g2, lng, lnb)


SC_WORKERS = 32
SC_IDX_BLOCK = 128
SC_CHUNK = 32
SC_SHARE_DEN = 16
SC_SHARE_FIRST = 5
SC_SHARE_LATER = 6


def _sc_gather_rows(tab, idx2d, row0, n_rows):
    from jax.experimental.pallas import tpu_sc as plsc
    D = tab.shape[1]
    steps = n_rows // SC_WORKERS
    n_chunks = SC_IDX_BLOCK // SC_CHUNK
    mesh = plsc.VectorSubcoreMesh(core_axis_name="core", subcore_axis_name="subcore")

    @pl.kernel(out_type=jax.ShapeDtypeStruct((n_rows * SC_IDX_BLOCK, D), tab.dtype), mesh=mesh,
               scratch_types=[pltpu.VMEM((1, SC_IDX_BLOCK), jnp.int32),
                              pltpu.VMEM((2, SC_CHUNK, D), tab.dtype),
                              pltpu.SemaphoreType.DMA((2,))])
    def k(x_hbm, i_hbm, o_hbm, i_vmem, buf, wsem):
        wid = lax.axis_index("core") * (SC_WORKERS // 2) + lax.axis_index("subcore")

        def write(row, c):
            return pltpu.make_async_copy(buf.at[c % 2], o_hbm.at[pl.ds(row * SC_IDX_BLOCK + c * SC_CHUNK, SC_CHUNK)],
                                         wsem.at[c % 2])

        @pl.loop(0, steps)
        def _(st):
            row = wid * steps + st
            pltpu.sync_copy(i_hbm.at[pl.ds(row0 + row, 1)], i_vmem)
            for c in range(n_chunks):
                if c >= 2:
                    write(row, c - 2).wait()
                pltpu.sync_copy(x_hbm.at[i_vmem.at[0, pl.ds(c * SC_CHUNK, SC_CHUNK)]], buf.at[c % 2])
                write(row, c).start()
            for c in range(n_chunks - 2, n_chunks):
                write(row, c).wait()
    return k(tab, idx2d)


def _rope_tables(L):
    half = HEAD_DIM // 4
    t = np.arange(L)
    freqs = ROPE_THETA ** (-np.arange(half, dtype=np.float32) / half)
    lane = np.arange(HEAD_DIM)
    pos = np.where(lane[None, :] < HEAD_DIM // 2, (t // GRID_W)[:, None], (t % GRID_W)[:, None]).astype(np.float32)
    ang = pos * freqs[lane % half][None, :].astype(np.float32)
    cos, sin = np.cos(ang), np.sin(ang)
    first = ((lane % (2 * half)) < half)[None, :]
    slo = np.where(first, -sin, 0.0)
    shi = np.where(first, 0.0, sin)
    tile = lambda a: jnp.asarray(np.tile(a, (1, N_KV_HEADS)), F32)
    return tile(cos), tile(slo), tile(shi)


def _pack_tables(u, v):
    ub = lax.bitcast_convert_type(u.astype(BF16), jnp.uint16).astype(jnp.uint32)
    vb = lax.bitcast_convert_type(v.astype(BF16), jnp.uint16).astype(jnp.uint32)
    return (ub << 16) | vb


def _pick(n, prefs):
    for p in prefs:
        if n % p == 0:
            return p
    return n


def kernel(x, c, ctx, c_ctx, w_mod, b_mod, w_in, q_norm_g, k_norm_g, gmlp_ln_g, gmlp_ln_b, gmlp_w_s, gmlp_b_s,
           conv_w_dw, conv_b_dw, conv_ln_g, conv_ln_b, conv_w_pw, conv_b_pw, w_out, ln1_g, ln1_b, ln2_g, ln2_b,
           peer_w_q, peer_sub_keys, peer_u, peer_v):
    B, L, D = x.shape
    LC = ctx.shape[1]
    depth = w_mod.shape[0]
    alpha = (2 * depth) ** 0.25
    gw = gmlp_ln_g.shape[-1]

    cos, slo, shi = _rope_tables(L)
    ones_c = jnp.ones((LC, KV_WIDTH), F32)
    zeros_c = jnp.zeros((LC, KV_WIDTH), F32)
    gidx = np.arange(ATTN_WIDTH) // HEAD_DIM
    gmat = jnp.asarray((gidx[:, None] == gidx[None, :]) / HEAD_DIM, BF16)

    cond = jnp.concatenate([c, c_ctx[None, :], jnp.zeros((-(B + 1) % 8, D), F32)], axis=0)

    t_in = _pick(L, (512, 256, 128))
    t_mix = _pick(L, (256, 128))
    t_inc = _pick(LC, (512, 256, 128))
    t_mixc = _pick(LC, (256, 128))
    tk_lat = _pick(L + LC, (768, 512, 384, 256, 128))
    tk_ctx = _pick(LC, (512, 256, 128))
    gtok = 16

    def retrieve(h2, wq_bf, keys_bf):
        n = h2.shape[0] * h2.shape[1]
        return _retrieval(h2.reshape(n, D), wq_bf, keys_bf, tile=_pick(n, (512, 256, 128)))

    def peer(h2, x1, g2, gate, eidx, l, seq, tab, n_sc):
        n = h2.shape[0] * h2.shape[1]
        h2f, x1f = h2.reshape(n, D), x1.reshape(n, D)
        lng, lnb = ln2_g[l].reshape(1, D), ln2_b[l].reshape(1, D)
        eidx3 = eidx.reshape(n // gtok, 1, -1)
        common = dict(gtok=gtok, seq=seq, alpha=alpha)
        args = (gate, tab, h2f, x1f, g2, lng, lnb)
        if n_sc == 0:
            return _experts(eidx3, *args, tok0=0, n_tok=n, **common).reshape(h2.shape)
        n_self = n - n_sc
        rows = _sc_gather_rows(tab.reshape(-1, D), eidx, n_self, n_sc)
        m = min(n_self, n_sc)
        both = _experts(eidx3, *args, tok0=0, n_tok=m, dense_rows=rows, dense_tok0=n_self, **common)
        outs_self, outs_sc = [both[0]], [both[1]]
        if n_self > m:
            outs_self.append(_experts(eidx3, *args, tok0=m, n_tok=n_self - m, **common))
        if n_sc > m:
            outs_sc.append(_experts_dense(rows, gate, h2f, x1f, g2, lng, lnb, tok0=n_self + m, n_tok=n_sc - m,
                                          row_tok0=m, **common))
        return jnp.concatenate(outs_self + outs_sc, axis=0).reshape(h2.shape)

    def sc_share(n, num):
        align = SC_WORKERS * gtok
        return n * num // SC_SHARE_DEN // align * align if n >= 8192 else 0

    n_groups = 2 if B % 2 == 0 else 1
    bg = B // n_groups
    groups = [slice(g * bg, (g + 1) * bg) for g in range(n_groups)]
    x_grp = [x[bs] for bs in groups]
    x_ctx = ctx
    for l in range(depth):
        last = l == depth - 1
        mod = _modulation(cond, w_mod[l], b_mod[l])
        m_lat = [mod[:B, k * D:(k + 1) * D].reshape(B, 1, D) for k in range(6)]
        m_ctx = [jnp.broadcast_to(mod[B:B + 1, k * D:(k + 1) * D].reshape(1, 1, D), (B, 1, D)) for k in range(6)]

        w_in_bf = w_in[l].astype(BF16)
        gq = jnp.tile(q_norm_g[l], N_Q_HEADS).reshape(1, ATTN_WIDTH)
        gk = jnp.tile(k_norm_g[l], N_KV_HEADS).reshape(1, KV_WIDTH)
        mix_prm = (gmlp_ln_g[l].reshape(1, gw), gmlp_ln_b[l].reshape(1, gw),
                   gmlp_w_s[l].astype(BF16),
                   jnp.repeat(gmlp_b_s[l].T, gw // GMLP_GROUPS, axis=1),
                   conv_w_dw[l], conv_b_dw[l].reshape(1, -1), conv_ln_g[l].reshape(1, -1),
                   conv_ln_b[l].reshape(1, -1), conv_w_pw[l].astype(BF16), conv_b_pw[l].reshape(1, -1),
                   w_out[l].astype(BF16), ln1_g[l].reshape(1, D), ln1_b[l].reshape(1, D))
        wq_bf = peer_w_q[l].astype(BF16)
        keys_bf = peer_sub_keys[l].reshape(2 * PEER_HEADS, N_KEYS, PEER_HALF).astype(BF16)
        tab = _pack_tables(peer_u[l], peer_v[l]).reshape(-1, 1, D)

        q_c, kt_c, v_c, zg_c, zc_c = _inproj(x_ctx, m_ctx[0], m_ctx[1], w_in_bf, gq, gk, ones_c, zeros_c, zeros_c,
                                              gmat, tile=t_inc)
        staged = []
        for g, bs in enumerate(groups):
            xg = x_grp[g]
            q_l, kt_l, v_l, zg_l, zc_l = _inproj(xg, m_lat[0][bs], m_lat[1][bs], w_in_bf, gq, gk, cos, slo, shi, gmat,
                                                  tile=t_in)
            kt_all = jnp.concatenate([kt_c[bs], kt_l], axis=3)
            v_all = jnp.concatenate([v_c[bs], v_l], axis=1)
            o_lat = _attention(q_l, kt_all, v_all, tq=128, tk=tk_lat)
            x1, h2 = _mixers(xg, o_lat, zg_l, zc_l, m_lat[2][bs], m_lat[3][bs], m_lat[4][bs], mix_prm, tile=t_mix,
                             alpha=alpha)
            staged.append((h2, x1) + tuple(retrieve(h2, wq_bf, keys_bf)))
        for g, bs in enumerate(groups):
            h2, x1, gate, eidx = staged[g]
            share = SC_SHARE_FIRST if g == 0 and n_groups > 1 else SC_SHARE_LATER
            x_grp[g] = peer(h2, x1, m_lat[5][bs], gate, eidx, l, L, tab, sc_share(bg * L, share))

        if not last:
            o_ctx = _attention(q_c, kt_c, v_c, tq=128, tk=tk_ctx)
            x1c, h2c = _mixers(x_ctx, o_ctx, zg_c, zc_c, m_ctx[2], m_ctx[3], m_ctx[4], mix_prm, tile=t_mixc,
                               alpha=alpha)
            gate_c, eidx_c = retrieve(h2c, wq_bf, keys_bf)
            x_ctx = peer(h2c, x1c, m_ctx[5], gate_c, eidx_c, l, LC, tab, 0)
    return jnp.concatenate(x_grp, axis=0)
```

```python
import functools
import math

import jax
import jax.numpy as jnp
import numpy as np
from jax import lax
from jax.experimental import pallas as pl
from jax.experimental.pallas import tpu as pltpu

F32 = jnp.float32
BF16 = jnp.bfloat16

EPS = 1e-6
GRID_W = 64
N_Q_HEADS = 8
N_KV_HEADS = 2
Q_PER_KV = N_Q_HEADS // N_KV_HEADS
HEAD_DIM = 64
ATTN_WIDTH = N_Q_HEADS * HEAD_DIM
KV_WIDTH = N_KV_HEADS * HEAD_DIM
QKV_WIDTH = ATTN_WIDTH + 2 * KV_WIDTH
ROPE_THETA = 10000.0
CHUNK = 128
GMLP_GROUPS = 4
CONV_K = 31
CONV_HALO = 16
PEER_HEADS = 8
N_KEYS = 128
PEER_TOPK = 16
PEER_HALF = 128

VMEM_LIMIT = 56 * 1024 * 1024


def _cparams(sem):
    return pltpu.CompilerParams(dimension_semantics=sem, vmem_limit_bytes=VMEM_LIMIT)


def _gelu_tanh(x):
    return 0.5 * x * (1.0 + jnp.tanh(math.sqrt(2.0 / math.pi) * (x + 0.044715 * (x * x * x))))


def _ln(x):
    mu = jnp.mean(x, axis=-1, keepdims=True)
    xc = x - mu
    var = jnp.mean(xc * xc, axis=-1, keepdims=True)
    return xc * lax.rsqrt(var + EPS)


def _mod_kernel(c_ref, w_ref, b_ref, o_ref):
    c = c_ref[...]
    s = c * jax.nn.sigmoid(c)
    o_ref[...] = jnp.dot(s, w_ref[...], preferred_element_type=F32,
                         precision=lax.Precision.HIGHEST) + b_ref[...]


def _modulation(cond, w_mod, b_mod):
    R, D = cond.shape
    N = w_mod.shape[1]
    tn = 1024
    return pl.pallas_call(
        _mod_kernel,
        grid=(N // tn,),
        in_specs=[pl.BlockSpec((R, D), lambda j: (0, 0)),
                  pl.BlockSpec((D, tn), lambda j: (0, j)),
                  pl.BlockSpec((1, tn), lambda j: (0, j))],
        out_specs=pl.BlockSpec((R, tn), lambda j: (0, j)),
        out_shape=jax.ShapeDtypeStruct((R, N), F32),
        compiler_params=_cparams(("arbitrary",)),
        name="modulation",
    )(cond, w_mod, b_mod.reshape(1, N))


def _group_mean_sq(t, gmat):
    sq = t * t
    hi = sq.astype(BF16)
    lo = (sq - hi.astype(F32)).astype(BF16)
    return (jnp.dot(hi, gmat, preferred_element_type=F32)
            + jnp.dot(lo, gmat, preferred_element_type=F32))


def _rope(t, cos, sin_lo, sin_hi):
    w = t.shape[-1]
    up = pltpu.roll(t, w - 16, axis=1)
    dn = pltpu.roll(t, 16, axis=1)
    return t * cos + up * sin_lo + dn * sin_hi


def _inproj_kernel(x_ref, sh_ref, sc_ref, w_ref, gq_ref, gk_ref, cos_ref, slo_ref, shi_ref, gmat_ref,
                   q_ref, kt_ref, v_ref, zg_ref, zc_ref):
    h = _ln(x_ref[0]) * (1.0 + sc_ref[0]) + sh_ref[0]
    p = jnp.dot(h.astype(BF16), w_ref[...], preferred_element_type=F32)
    q = p[:, :ATTN_WIDTH]
    k = p[:, ATTN_WIDTH:ATTN_WIDTH + KV_WIDTH]
    v = p[:, ATTN_WIDTH + KV_WIDTH:QKV_WIDTH]
    gm_w = (p.shape[1] - QKV_WIDTH) // 2
    zg_ref[0] = p[:, QKV_WIDTH:QKV_WIDTH + gm_w]
    zc_ref[0] = p[:, QKV_WIDTH + gm_w:]
    v_ref[0] = v.astype(BF16)

    gmat = gmat_ref[...]
    cos, slo, shi = cos_ref[...], slo_ref[...], shi_ref[...]
    qn = q * lax.rsqrt(_group_mean_sq(q, gmat) + EPS) * gq_ref[...]
    rep = ATTN_WIDTH // KV_WIDTH
    qr = _rope(qn, jnp.concatenate([cos] * rep, axis=1), jnp.concatenate([slo] * rep, axis=1),
               jnp.concatenate([shi] * rep, axis=1))
    qr = (qr * (HEAD_DIM ** -0.5 * math.log2(math.e))).astype(BF16)
    for hh in range(N_Q_HEADS):
        q_ref[0, hh] = qr[:, hh * HEAD_DIM:(hh + 1) * HEAD_DIM]

    kn = k * lax.rsqrt(_group_mean_sq(k, gmat[:KV_WIDTH, :KV_WIDTH]) + EPS) * gk_ref[...]
    kt = _rope(kn, cos, slo, shi).T
    for g in range(N_KV_HEADS):
        kt_ref[0, g] = kt[g * HEAD_DIM:(g + 1) * HEAD_DIM, :].astype(BF16)


def _inproj(x, sh, sc, w_in_bf, gq, gk, cos, slo, shi, gmat, *, tile):
    B, L, D = x.shape
    W = w_in_bf.shape[1]
    gm_w = (W - QKV_WIDTH) // 2
    nt = L // tile
    bmap = lambda b, i: (b, 0, 0)
    cmap = lambda b, i: (0, 0)
    return pl.pallas_call(
        _inproj_kernel,
        grid=(B, nt),
        in_specs=[pl.BlockSpec((1, tile, D), lambda b, i: (b, i, 0)),
                  pl.BlockSpec((1, 1, D), bmap), pl.BlockSpec((1, 1, D), bmap),
                  pl.BlockSpec((D, W), cmap),
                  pl.BlockSpec((1, ATTN_WIDTH), cmap), pl.BlockSpec((1, KV_WIDTH), cmap),
                  pl.BlockSpec((tile, KV_WIDTH), lambda b, i: (i, 0)),
                  pl.BlockSpec((tile, KV_WIDTH), lambda b, i: (i, 0)),
                  pl.BlockSpec((tile, KV_WIDTH), lambda b, i: (i, 0)),
                  pl.BlockSpec((ATTN_WIDTH, ATTN_WIDTH), cmap)],
        out_specs=[pl.BlockSpec((1, N_Q_HEADS, tile, HEAD_DIM), lambda b, i: (b, 0, i, 0)),
                   pl.BlockSpec((1, N_KV_HEADS, HEAD_DIM, tile), lambda b, i: (b, 0, 0, i)),
                   pl.BlockSpec((1, tile, KV_WIDTH), lambda b, i: (b, i, 0)),
                   pl.BlockSpec((1, tile, gm_w), lambda b, i: (b, i, 0)),
                   pl.BlockSpec((1, tile, gm_w), lambda b, i: (b, i, 0))],
        out_shape=[jax.ShapeDtypeStruct((B, N_Q_HEADS, L, HEAD_DIM), BF16),
                   jax.ShapeDtypeStruct((B, N_KV_HEADS, HEAD_DIM, L), BF16),
                   jax.ShapeDtypeStruct((B, L, KV_WIDTH), BF16),
                   jax.ShapeDtypeStruct((B, L, gm_w), F32),
                   jax.ShapeDtypeStruct((B, L, gm_w), F32)],
        compiler_params=_cparams(("arbitrary", "arbitrary")),
        name="inproj",
    )(x, sh, sc, w_in_bf, gq, gk, cos, slo, shi, gmat)


def _attn_kernel(q_ref, kt_ref, vx_ref, o_ref, p_ref, *, tq, tk, nk):
    m_rows = Q_PER_KV * tq
    qs = [q_ref[0, Q_PER_KV * g:Q_PER_KV * (g + 1)].reshape(m_rows, HEAD_DIM) for g in range(N_KV_HEADS)]

    def pv(c, par, g):
        off = pl.multiple_of(c * tk, tk)
        return jnp.dot(p_ref[par, g], vx_ref[0, g, pl.ds(off, tk), :], preferred_element_type=F32)

    def step(c, par, carry):
        off = pl.multiple_of(c * tk, tk)
        out = []
        for g in range(N_KV_HEADS):
            m, acc = carry[g]
            s = jnp.dot(qs[g], kt_ref[0, g, :, pl.ds(off, tk)], preferred_element_type=F32)
            m_new = jnp.maximum(m, jnp.max(s, axis=-1, keepdims=True))
            p_ref[1 - par, g] = jnp.exp2(s - m_new).astype(BF16)
            if acc is not None:
                acc = jnp.exp2(m - m_new) * (acc + pv(c - 1, par, g))
            out.append((m_new, acc))
        return tuple(out)

    first = step(0, 1, tuple((jnp.full((m_rows, 1), -1e30, F32), None) for _ in range(N_KV_HEADS)))
    carry = tuple((m, jnp.zeros((m_rows, 2 * HEAD_DIM), F32)) for m, _ in first)

    n_pairs = (nk - 1) // 2
    def pair(t, carry):
        carry = step(2 * t + 1, 0, carry)
        return step(2 * t + 2, 1, carry)
    carry = lax.fori_loop(0, n_pairs, pair, carry)
    par = 0
    if (nk - 1) % 2:
        carry = step(nk - 1, 0, carry)
        par = 1

    for g in range(N_KV_HEADS):
        acc = carry[g][1] + pv(nk - 1, par, g)
        og = acc / pltpu.roll(acc, HEAD_DIM, axis=1)
        for r in range(Q_PER_KV):
            hh = Q_PER_KV * g + r
            o_ref[0, :, hh * HEAD_DIM:(hh + 1) * HEAD_DIM] = og[r * tq:(r + 1) * tq, :HEAD_DIM].astype(o_ref.dtype)


def _attention(q, kt, v, *, tq, tk):
    B, _, L, _ = q.shape
    LK = kt.shape[-1]
    ones = jnp.ones((B, LK, HEAD_DIM), v.dtype)
    vx = jnp.stack([jnp.concatenate([v[..., g * HEAD_DIM:(g + 1) * HEAD_DIM], ones], axis=-1)
                    for g in range(N_KV_HEADS)], axis=1)
    return pl.pallas_call(
        functools.partial(_attn_kernel, tq=tq, tk=tk, nk=LK // tk),
        grid=(B, L // tq),
        in_specs=[pl.BlockSpec((1, N_Q_HEADS, tq, HEAD_DIM), lambda b, i: (b, 0, i, 0)),
                  pl.BlockSpec((1, N_KV_HEADS, HEAD_DIM, LK), lambda b, i: (b, 0, 0, 0)),
                  pl.BlockSpec((1, N_KV_HEADS, LK, 2 * HEAD_DIM), lambda b, i: (b, 0, 0, 0))],
        out_specs=pl.BlockSpec((1, tq, ATTN_WIDTH), lambda b, i: (b, i, 0)),
        out_shape=jax.ShapeDtypeStruct((B, L, ATTN_WIDTH), BF16),
        scratch_shapes=[pltpu.VMEM((2, N_KV_HEADS, Q_PER_KV * tq, tk), BF16)],
        compiler_params=_cparams(("arbitrary", "arbitrary")),
        name="attention",
    )(q, kt, vx)


def _mix_kernel(x_ref, o_ref, zg_ref, zc_ref, zp_ref, zn_ref, g1_ref, sh2_ref, sc2_ref,
                gln_g_ref, gln_b_ref, ws_ref, bs_ref, wdw_ref, bdw_ref, cln_g_ref, cln_b_ref,
                wpw_ref, bpw_ref, wout_ref, ln1g_ref, ln1b_ref,
                x1_ref, h2_ref, ybuf_ref, *, tile, alpha):
    i = pl.program_id(1)
    nt = pl.num_programs(1)
    gw = zg_ref.shape[-1] // 2
    cw = zc_ref.shape[-1] // 2

    z = _gelu_tanh(zg_ref[0])
    u, v = z[:, :gw], z[:, gw:]
    vn = (_ln(v) * gln_g_ref[...] + gln_b_ref[...]).astype(BF16)
    grp = lax.broadcasted_iota(jnp.int32, (CHUNK, gw), 1) // (gw // GMLP_GROUPS)
    sg = []
    for n in range(tile // CHUNK):
        vc = vn[n * CHUNK:(n + 1) * CHUNK]
        s = jnp.zeros((CHUNK, gw), F32)
        for g in range(GMLP_GROUPS):
            s = jnp.where(grp == g, jnp.dot(ws_ref[g], vc, preferred_element_type=F32), s)
        sg.append(s + bs_ref[...])
    y_g = u * jnp.concatenate(sg, axis=0)

    def glu(zz):
        return zz[:, :cw] * jax.nn.sigmoid(zz[:, cw:])

    ybuf_ref[0:CONV_HALO] = glu(zp_ref[0]) * (i > 0).astype(F32)
    ybuf_ref[CONV_HALO:CONV_HALO + tile] = glu(zc_ref[0])
    ybuf_ref[CONV_HALO + tile:2 * CONV_HALO + tile] = glu(zn_ref[0]) * (i < nt - 1).astype(F32)
    base = CONV_HALO - CONV_K // 2
    conv = jnp.zeros((tile, cw), F32)
    for kk in range(CONV_K):
        conv = conv + ybuf_ref[base + kk:base + kk + tile, :] * wdw_ref[kk:kk + 1, :]
    conv = conv + bdw_ref[...]
    t = _ln(conv) * cln_g_ref[...] + cln_b_ref[...]
    t = t * jax.nn.sigmoid(t)
    y_c = jnp.dot(t.astype(BF16), wpw_ref[...], preferred_element_type=F32) + bpw_ref[...]

    aw = o_ref.shape[-1]
    y = (jnp.dot(o_ref[0], wout_ref[0:aw, :], preferred_element_type=F32)
         + jnp.dot(y_g.astype(BF16), wout_ref[aw:aw + gw, :], preferred_element_type=F32)
         + jnp.dot(y_c.astype(BF16), wout_ref[aw + gw:, :], preferred_element_type=F32))
    x1 = _ln(alpha * x_ref[0] + g1_ref[0] * y) * ln1g_ref[...] + ln1b_ref[...]
    x1_ref[0] = x1
    h2_ref[0] = _ln(x1) * (1.0 + sc2_ref[0]) + sh2_ref[0]


def _mixers(x, o, zg, zc, g1, sh2, sc2, prm, *, tile, alpha):
    B, L, D = x.shape
    nt = L // tile
    hb = tile // CONV_HALO
    nhb = L // CONV_HALO
    gw2, cw2 = zg.shape[-1], zc.shape[-1]
    tmap = lambda b, i: (b, i, 0)
    bmap = lambda b, i: (b, 0, 0)
    c2 = lambda b, i: (0, 0)
    c3 = lambda b, i: (0, 0, 0)
    in_specs = [pl.BlockSpec((1, tile, D), tmap),
                pl.BlockSpec((1, tile, o.shape[-1]), tmap),
                pl.BlockSpec((1, tile, gw2), tmap),
                pl.BlockSpec((1, tile, cw2), tmap),
                pl.BlockSpec((1, CONV_HALO, cw2), lambda b, i: (b, jnp.maximum(i * hb - 1, 0), 0)),
                pl.BlockSpec((1, CONV_HALO, cw2), lambda b, i: (b, jnp.minimum((i + 1) * hb, nhb - 1), 0)),
                pl.BlockSpec((1, 1, D), bmap), pl.BlockSpec((1, 1, D), bmap), pl.BlockSpec((1, 1, D), bmap)]
    for a in prm:
        in_specs.append(pl.BlockSpec(a.shape, c2 if a.ndim == 2 else c3))
    return pl.pallas_call(
        functools.partial(_mix_kernel, tile=tile, alpha=alpha),
        grid=(B, nt),
        in_specs=in_specs,
        out_specs=[pl.BlockSpec((1, tile, D), tmap), pl.BlockSpec((1, tile, D), tmap)],
        out_shape=[jax.ShapeDtypeStruct((B, L, D), F32), jax.ShapeDtypeStruct((B, L, D), F32)],
        scratch_shapes=[pltpu.VMEM((tile + 2 * CONV_HALO, cw2 // 2), F32)],
        compiler_params=_cparams(("arbitrary", "arbitrary")),
        name="mixers",
    )(x, o, zg, zc, zc, zc, g1, sh2, sc2, *prm)


def _oddeven_merge_sort_pairs(n):
    pairs = []
    p = 1
    while p < n:
        k = p
        while k >= 1:
            for j in range(k % p, n - k, 2 * k):
                for i in range(min(k, n - j - k)):
                    if (i + j) // (2 * p) == (i + j + k) // (2 * p):
                        pairs.append((i + j, i + j + k))
            k //= 2
        p *= 2
    return pairs


_SORT16 = _oddeven_merge_sort_pairs(16)
_BITONIC16 = [(i, i + d) for d in (8, 4, 2, 1) for i in range(16) if (i & d) == 0]


def _before(a, ia, b, ib):
    return (a > b) | ((a == b) & (ia < ib))


def _cmpx(vals, keys, pay, i, j):
    keep = _before(vals[i], keys[i], vals[j], keys[j])
    vals[i], vals[j] = jnp.where(keep, vals[i], vals[j]), jnp.where(keep, vals[j], vals[i])
    keys[i], keys[j] = jnp.where(keep, keys[i], keys[j]), jnp.where(keep, keys[j], keys[i])
    if pay is not None:
        pay[i], pay[j] = jnp.where(keep, pay[i], pay[j]), jnp.where(keep, pay[j], pay[i])


def _top16_merge(av, ak, ap, bv, bk, bp):
    n = len(av)
    ov, ok, op = [], [], []
    for i in range(n):
        keep = _before(av[i], ak[i], bv[n - 1 - i], bk[n - 1 - i])
        ov.append(jnp.where(keep, av[i], bv[n - 1 - i]))
        ok.append(jnp.where(keep, ak[i], bk[n - 1 - i]))
        if ap is not None:
            op.append(jnp.where(keep, ap[i], bp[n - 1 - i]))
    return ov, ok, (op if ap is not None else None)


def _stage2_candidates():
    return [(a, b) for a in range(PEER_TOPK) for b in range(PEER_TOPK) if (a + 1) * (b + 1) <= PEER_TOPK]


def _retr_kernel(h_ref, wq_ref, keys_ref, gate_ref, eidx_ref, q_s, sv_s, si_s, *, tile):
    q_s[...] = jnp.dot(h_ref[...].astype(BF16), wq_ref[...], preferred_element_type=F32)
    sub = lax.broadcasted_iota(jnp.int32, (8, tile), 0)
    sv_s[...] = jnp.zeros(sv_s.shape, F32)
    si_s[...] = jnp.zeros(si_s.shape, jnp.int32)

    def stage1(hp, _):
        qh = q_s[:, pl.ds(pl.multiple_of(hp * PEER_HALF, PEER_HALF), PEER_HALF)].astype(BF16)
        st = lax.dot_general(keys_ref[hp], qh, (((1,), (1,)), ((), ())),
                             preferred_element_type=F32)
        st = st.reshape(N_KEYS // 8, 8, tile)
        vals = [st[i] for i in range(16)]
        keys = [sub + 8 * i for i in range(16)]
        for (i, j) in _SORT16:
            _cmpx(vals, keys, None, i, j)
        for shift in (4, 2, 1):
            bv = [pltpu.roll(t, shift, axis=0) for t in vals]
            bk = [pltpu.roll(t, shift, axis=0) for t in keys]
            vals, keys, _ = _top16_merge(vals, keys, None, bv, bk, None)
            for (i, j) in _BITONIC16:
                _cmpx(vals, keys, None, i, j)
        head = hp // 2
        half = hp % 2
        for i in range(16):
            sv_s[half, i] = jnp.where(sub == head, vals[i], sv_s[half, i])
            si_s[half, i] = jnp.where(sub == head, keys[i], si_s[half, i])
        return 0

    lax.fori_loop(0, 2 * PEER_HEADS, stage1, 0)

    cands = _stage2_candidates()
    n_pad = 64
    vals, poss, exps = [], [], []
    for (a, b) in cands:
        vals.append(sv_s[0, a] + sv_s[1, b])
        poss.append(jnp.full((8, tile), a * PEER_TOPK + b, jnp.int32))
        exps.append(si_s[0, a] * N_KEYS + si_s[1, b])
    for t in range(n_pad - len(cands)):
        vals.append(jnp.full((8, tile), -jnp.inf, F32))
        poss.append(jnp.full((8, tile), PEER_TOPK * PEER_TOPK + t, jnp.int32))
        exps.append(jnp.zeros((8, tile), jnp.int32))
    groups = []
    for g in range(n_pad // 16):
        gv, gk, gp = vals[16 * g:16 * g + 16], poss[16 * g:16 * g + 16], exps[16 * g:16 * g + 16]
        for (i, j) in _SORT16:
            _cmpx(gv, gk, gp, i, j)
        groups.append((gv, gk, gp))
    while len(groups) > 1:
        nxt = []
        for g in range(0, len(groups), 2):
            mv, mk, mp = _top16_merge(*groups[g], *groups[g + 1])
            if len(groups) > 2:
                for (i, j) in _BITONIC16:
                    _cmpx(mv, mk, mp, i, j)
            nxt.append((mv, mk, mp))
        groups = nxt
    best, _, experts = groups[0]

    mx = functools.reduce(jnp.maximum, best)
    ex = [jnp.exp(b - mx) for b in best]
    inv = 1.0 / functools.reduce(lambda a, b: a + b, ex)
    gate_ref[...] = jnp.concatenate([e * inv for e in ex], axis=0).T
    eidx_ref[...] = jnp.concatenate(experts, axis=0).T


def _retrieval(h2, wq_bf, keys_bf, *, tile):
    N, D = h2.shape
    QW = wq_bf.shape[1]
    return pl.pallas_call(
        functools.partial(_retr_kernel, tile=tile),
        grid=(N // tile,),
        in_specs=[pl.BlockSpec((tile, D), lambda i: (i, 0)),
                  pl.BlockSpec((D, QW), lambda i: (0, 0)),
                  pl.BlockSpec(keys_bf.shape, lambda i: (0, 0, 0))],
        out_specs=[pl.BlockSpec((tile, PEER_TOPK * PEER_HEADS), lambda i: (i, 0)),
                   pl.BlockSpec((tile, PEER_TOPK * PEER_HEADS), lambda i: (i, 0))],
        out_shape=[jax.ShapeDtypeStruct((N, PEER_TOPK * PEER_HEADS), F32),
                   jax.ShapeDtypeStruct((N, PEER_TOPK * PEER_HEADS), jnp.int32)],
        scratch_shapes=[pltpu.VMEM((tile, QW), F32),
                        pltpu.VMEM((2, PEER_TOPK, PEER_HEADS, tile), F32),
                        pltpu.VMEM((2, PEER_TOPK, PEER_HEADS, tile), jnp.int32)],
        compiler_params=_cparams(("arbitrary",)),
        name="retrieval",
    )(h2, wq_bf, keys_bf)


def _expert_token(r, h_row, gate_col):
    u = pltpu.bitcast(r & jnp.uint32(0xFFFF0000), F32)
    v = pltpu.bitcast(r << 16, F32)
    d = jnp.sum(u * h_row, axis=-1, keepdims=True)
    w = jnp.broadcast_to(gate_col * _gelu_tanh(d), r.shape)
    return jnp.sum(v * w, axis=0, keepdims=True)


def _expert_kernel(*refs, gtok, alpha, dense):
    if dense:
        (idx0_ref, idxn_ref, tab_ref, h_ref, gate_ref, x1_ref, g2_ref, rows_ref, hd_ref, gated_ref, x1d_ref, g2d_ref,
         lng_ref, lnb_ref, o_ref, od_ref, buf_a, buf_b, f_ref, fd_ref, sem_ref) = refs
    else:
        (idx0_ref, idxn_ref, tab_ref, h_ref, gate_ref, x1_ref, g2_ref, lng_ref, lnb_ref,
         o_ref, buf_a, buf_b, f_ref, fd_ref, sem_ref) = refs
    i = pl.program_id(0)
    n_steps = pl.num_programs(0)
    per_tok = PEER_HEADS * PEER_TOPK
    d_model = h_ref.shape[-1]

    def start_token(idx_ref, buf, sem_row, j):
        for g in range(per_tok // 8):
            for t in range(8):
                e = idx_ref[0, 0, j * per_tok + g * 8 + t]
                pltpu.make_async_copy(tab_ref.at[e], buf.at[j, g, pl.ds(t, 1), :],
                                      sem_ref.at[sem_row, j]).start(priority=t % 2)

    def wait_token(buf, other, sem_row, j):
        pltpu.make_async_copy(other.at[j], buf.at[j], sem_ref.at[sem_row, j]).wait()

    def token(j, cur, nxt, cur_row):
        wait_token(cur, nxt, cur_row, j)
        start_token(idxn_ref, nxt, 1 - cur_row, j)
        f_ref[j:j + 1, :] = _expert_token(cur[j].reshape(per_tok, d_model), h_ref[j:j + 1, :],
                                          gate_t[:, j:j + 1])
        if dense:
            fd_ref[j:j + 1, :] = _expert_token(rows_ref[j * per_tok:(j + 1) * per_tok, :], hd_ref[j:j + 1, :],
                                               gated_t[:, j:j + 1])

    @pl.when(i == 0)
    def _():
        def prime(j, _):
            start_token(idx0_ref, buf_a, 0, j)
            return 0
        lax.fori_loop(0, gtok, prime, 0)

    gate_t = gate_ref[...].T
    gated_t = gated_ref[...].T if dense else None

    @pl.when(i % 2 == 0)
    def _():
        for j in range(gtok):
            token(j, buf_a, buf_b, 0)

    @pl.when(i % 2 == 1)
    def _():
        for j in range(gtok):
            token(j, buf_b, buf_a, 1)

    @pl.when((i == n_steps - 1) & (i % 2 == 0))
    def _():
        lax.fori_loop(0, gtok, lambda j, c: (wait_token(buf_b, buf_a, 1, j), 0)[1], 0)

    @pl.when((i == n_steps - 1) & (i % 2 == 1))
    def _():
        lax.fori_loop(0, gtok, lambda j, c: (wait_token(buf_a, buf_b, 0, j), 0)[1], 0)

    o_ref[...] = _ln(alpha * x1_ref[...] + g2_ref[0] * f_ref[...]) * lng_ref[...] + lnb_ref[...]

    if dense:
        od_ref[...] = _ln(alpha * x1d_ref[...] + g2d_ref[0] * fd_ref[...]) * lng_ref[...] + lnb_ref[...]


def _experts(eidx, gate, tab, h2, x1, g2, lng, lnb, *, tok0, n_tok, gtok, seq, alpha, dense_rows=None,
             dense_tok0=0):
    D = h2.shape[1]
    S = n_tok // gtok
    off = tok0 // gtok
    doff = dense_tok0 // gtok
    per_tok = PEER_HEADS * PEER_TOPK
    rows = gtok * per_tok
    spb = seq // gtok
    smem = pltpu.MemorySpace.SMEM
    dense = dense_rows is not None
    tok_spec = lambda o: pl.BlockSpec((gtok, D), lambda i: (i + o, 0))
    gate_spec = lambda o: pl.BlockSpec((gtok, per_tok), lambda i: (i + o, 0))
    g2_spec = lambda o: pl.BlockSpec((1, 1, D), lambda i: ((i + o) // spb, 0, 0))
    row_spec = pl.BlockSpec((1, D), lambda i: (0, 0))
    in_specs = [pl.BlockSpec((1, 1, rows), lambda i: (off, 0, 0), memory_space=smem),
                pl.BlockSpec((1, 1, rows), lambda i: (jnp.minimum(i + 1, S - 1) + off, 0, 0), memory_space=smem),
                pl.BlockSpec(memory_space=pl.ANY),
                tok_spec(off), gate_spec(off), tok_spec(off), g2_spec(off)]
    args = [eidx, eidx, tab, h2, gate, x1, g2]
    out_spec = pl.BlockSpec((gtok, D), lambda i: (i, 0))
    out_shape = jax.ShapeDtypeStruct((n_tok, D), F32)
    if dense:
        in_specs += [pl.BlockSpec((rows, D), lambda i: (i, 0)), tok_spec(doff), gate_spec(doff), tok_spec(doff),
                     g2_spec(doff)]
        args += [dense_rows, h2, gate, x1, g2]
    return pl.pallas_call(
        functools.partial(_expert_kernel, gtok=gtok, alpha=alpha, dense=dense),
        grid=(S,),
        in_specs=in_specs + [row_spec, row_spec],
        out_specs=[out_spec, out_spec] if dense else out_spec,
        out_shape=[out_shape, out_shape] if dense else out_shape,
        scratch_shapes=[pltpu.VMEM((gtok, per_tok // 8, 8, D), jnp.uint32),
                        pltpu.VMEM((gtok, per_tok // 8, 8, D), jnp.uint32),
                        pltpu.VMEM((gtok, D), F32),
                        pltpu.VMEM((gtok, D), F32),
                        pltpu.SemaphoreType.DMA((2, gtok))],
        compiler_params=_cparams(("arbitrary",)),
        name="experts",
    )(*args, lng, lnb)


def _expert_dense_kernel(rows_ref, h_ref, gate_ref, x1_ref, g2_ref, lng_ref, lnb_ref, o_ref, f_ref, *, gtok, alpha):
    per_tok = PEER_HEADS * PEER_TOPK
    gate_t = gate_ref[...].T
    for j in range(gtok):
        f_ref[j:j + 1, :] = _expert_token(rows_ref[j * per_tok:(j + 1) * per_tok, :], h_ref[j:j + 1, :],
                                          gate_t[:, j:j + 1])
    o_ref[...] = _ln(alpha * x1_ref[...] + g2_ref[0] * f_ref[...]) * lng_ref[...] + lnb_ref[...]


def _experts_dense(rows, gate, h2, x1, g2, lng, lnb, *, tok0, n_tok, row_tok0, gtok, seq, alpha):
    D = h2.shape[1]
    per_tok = PEER_HEADS * PEER_TOPK
    off = tok0 // gtok
    steps_per_batch = seq // gtok
    return pl.pallas_call(
        functools.partial(_expert_dense_kernel, gtok=gtok, alpha=alpha),
        grid=(n_tok // gtok,),
        in_specs=[pl.BlockSpec((gtok * per_tok, D), lambda i: (i + row_tok0 // gtok, 0)),
                  pl.BlockSpec((gtok, D), lambda i: (i + off, 0)),
                  pl.BlockSpec((gtok, per_tok), lambda i: (i + off, 0)),
                  pl.BlockSpec((gtok, D), lambda i: (i + off, 0)),
                  pl.BlockSpec((1, 1, D), lambda i: ((i + off) // steps_per_batch, 0, 0)),
                  pl.BlockSpec((1, D), lambda i: (0, 0)),
                  pl.BlockSpec((1, D), lambda i: (0, 0))],
        out_specs=pl.BlockSpec((gtok, D), lambda i: (i, 0)),
        out_shape=jax.ShapeDtypeStruct((n_tok, D), F32),
        scratch_shapes=[pltpu.VMEM((gtok, D), F32)],
        compiler_params=_cparams(("arbitrary",)),
        name="experts_dense",
    )(rows, h2, gate, x1, g2, lng, lnb)


SC_WORKERS = 32
SC_IDX_BLOCK = 128
SC_CHUNK = 32
SC_SHARE_DEN = 16
SC_SHARE_FIRST = 5
SC_SHARE_LATER = 6


def _sc_gather_rows(tab, idx2d, row0, n_rows):
    from jax.experimental.pallas import tpu_sc as plsc
    D = tab.shape[1]
    steps = n_rows // SC_WORKERS
    n_chunks = SC_IDX_BLOCK // SC_CHUNK
    mesh = plsc.VectorSubcoreMesh(core_axis_name="core", subcore_axis_name="subcore")

    @pl.kernel(out_type=jax.ShapeDtypeStruct((n_rows * SC_IDX_BLOCK, D), tab.dtype), mesh=mesh,
               scratch_types=[pltpu.VMEM((1, SC_IDX_BLOCK), jnp.int32),
                              pltpu.VMEM((2, SC_CHUNK, D), tab.dtype),
                              pltpu.SemaphoreType.DMA((2,))])
    def k(x_hbm, i_hbm, o_hbm, i_vmem, buf, wsem):
        wid = lax.axis_index("core") * (SC_WORKERS // 2) + lax.axis_index("subcore")

        def write(row, c):
            return pltpu.make_async_copy(buf.at[c % 2], o_hbm.at[pl.ds(row * SC_IDX_BLOCK + c * SC_CHUNK, SC_CHUNK)],
                                         wsem.at[c % 2])

        @pl.loop(0, steps)
        def _(st):
            row = wid * steps + st
            pltpu.sync_copy(i_hbm.at[pl.ds(row0 + row, 1)], i_vmem)
            for c in range(n_chunks):
                if c >= 2:
                    write(row, c - 2).wait()
                pltpu.sync_copy(x_hbm.at[i_vmem.at[0, pl.ds(c * SC_CHUNK, SC_CHUNK)]], buf.at[c % 2])
                write(row, c).start()
            for c in range(n_chunks - 2, n_chunks):
                write(row, c).wait()
    return k(tab, idx2d)


def _rope_tables(L):
    half = HEAD_DIM // 4
    t = np.arange(L)
    freqs = ROPE_THETA ** (-np.arange(half, dtype=np.float32) / half)
    lane = np.arange(HEAD_DIM)
    pos = np.where(lane[None, :] < HEAD_DIM // 2, (t // GRID_W)[:, None], (t % GRID_W)[:, None]).astype(np.float32)
    ang = pos * freqs[lane % half][None, :].astype(np.float32)
    cos, sin = np.cos(ang), np.sin(ang)
    first = ((lane % (2 * half)) < half)[None, :]
    slo = np.where(first, -sin, 0.0)
    shi = np.where(first, 0.0, sin)
    tile = lambda a: jnp.asarray(np.tile(a, (1, N_KV_HEADS)), F32)
    return tile(cos), tile(slo), tile(shi)


def _pack_tables(u, v):
    ub = lax.bitcast_convert_type(u.astype(BF16), jnp.uint16).astype(jnp.uint32)
    vb = lax.bitcast_convert_type(v.astype(BF16), jnp.uint16).astype(jnp.uint32)
    return (ub << 16) | vb


def _pick(n, prefs):
    for p in prefs:
        if n % p == 0:
            return p
    return n


def kernel(x, c, ctx, c_ctx, w_mod, b_mod, w_in, q_norm_g, k_norm_g, gmlp_ln_g, gmlp_ln_b, gmlp_w_s, gmlp_b_s,
           conv_w_dw, conv_b_dw, conv_ln_g, conv_ln_b, conv_w_pw, conv_b_pw, w_out, ln1_g, ln1_b, ln2_g, ln2_b,
           peer_w_q, peer_sub_keys, peer_u, peer_v):
    B, L, D = x.shape
    LC = ctx.shape[1]
    depth = w_mod.shape[0]
    alpha = (2 * depth) ** 0.25
    gw = gmlp_ln_g.shape[-1]

    cos, slo, shi = _rope_tables(L)
    ones_c = jnp.ones((LC, KV_WIDTH), F32)
    zeros_c = jnp.zeros((LC, KV_WIDTH), F32)
    gidx = np.arange(ATTN_WIDTH) // HEAD_DIM
    gmat = jnp.asarray((gidx[:, None] == gidx[None, :]) / HEAD_DIM, BF16)

    cond = jnp.concatenate([c, c_ctx[None, :], jnp.zeros((-(B + 1) % 8, D), F32)], axis=0)

    t_in = _pick(L, (512, 256, 128))
    t_mix = _pick(L, (256, 128))
    t_inc = _pick(LC, (512, 256, 128))
    t_mixc = _pick(LC, (256, 128))
    tk_lat = _pick(L + LC, (768, 512, 384, 256, 128))
    tk_ctx = _pick(LC, (512, 256, 128))
    gtok = 16

    def retrieve(h2, wq_bf, keys_bf):
        n = h2.shape[0] * h2.shape[1]
        return _retrieval(h2.reshape(n, D), wq_bf, keys_bf, tile=_pick(n, (512, 256, 128)))

    def peer(h2, x1, g2, gate, eidx, l, seq, tab, n_sc):
        n = h2.shape[0] * h2.shape[1]
        h2f, x1f = h2.reshape(n, D), x1.reshape(n, D)
        lng, lnb = ln2_g[l].reshape(1, D), ln2_b[l].reshape(1, D)
        eidx3 = eidx.reshape(n // gtok, 1, -1)
        common = dict(gtok=gtok, seq=seq, alpha=alpha)
        args = (gate, tab, h2f, x1f, g2, lng, lnb)
        if n_sc == 0:
            return _experts(eidx3, *args, tok0=0, n_tok=n, **common).reshape(h2.shape)
        n_self = n - n_sc
        rows = _sc_gather_rows(tab.reshape(-1, D), eidx, n_self, n_sc)
        m = min(n_self, n_sc)
        both = _experts(eidx3, *args, tok0=0, n_tok=m, dense_rows=rows, dense_tok0=n_self, **common)
        outs_self, outs_sc = [both[0]], [both[1]]
        if n_self > m:
            outs_self.append(_experts(eidx3, *args, tok0=m, n_tok=n_self - m, **common))
        if n_sc > m:
            outs_sc.append(_experts_dense(rows, gate, h2f, x1f, g2, lng, lnb, tok0=n_self + m, n_tok=n_sc - m,
                                          row_tok0=m, **common))
        return jnp.concatenate(outs_self + outs_sc, axis=0).reshape(h2.shape)

    def sc_share(n, num):
        align = SC_WORKERS * gtok
        return n * num // SC_SHARE_DEN // align * align if n >= 8192 else 0

    n_groups = 2 if B % 2 == 0 else 1
    bg = B // n_groups
    groups = [slice(g * bg, (g + 1) * bg) for g in range(n_groups)]
    x_grp = [x[bs] for bs in groups]
    x_ctx = ctx
    for l in range(depth):
        last = l == depth - 1
        mod = _modulation(cond, w_mod[l], b_mod[l])
        m_lat = [mod[:B, k * D:(k + 1) * D].reshape(B, 1, D) for k in range(6)]
        m_ctx = [jnp.broadcast_to(mod[B:B + 1, k * D:(k + 1) * D].reshape(1, 1, D), (B, 1, D)) for k in range(6)]

        w_in_bf = w_in[l].astype(BF16)
        gq = jnp.tile(q_norm_g[l], N_Q_HEADS).reshape(1, ATTN_WIDTH)
        gk = jnp.tile(k_norm_g[l], N_KV_HEADS).reshape(1, KV_WIDTH)
        mix_prm = (gmlp_ln_g[l].reshape(1, gw), gmlp_ln_b[l].reshape(1, gw),
                   gmlp_w_s[l].astype(BF16),
                   jnp.repeat(gmlp_b_s[l].T, gw // GMLP_GROUPS, axis=1),
                   conv_w_dw[l], conv_b_dw[l].reshape(1, -1), conv_ln_g[l].reshape(1, -1),
                   conv_ln_b[l].reshape(1, -1), conv_w_pw[l].astype(BF16), conv_b_pw[l].reshape(1, -1),
                   w_out[l].astype(BF16), ln1_g[l].reshape(1, D), ln1_b[l].reshape(1, D))
        wq_bf = peer_w_q[l].astype(BF16)
        keys_bf = peer_sub_keys[l].reshape(2 * PEER_HEADS, N_KEYS, PEER_HALF).astype(BF16)
        tab = _pack_tables(peer_u[l], peer_v[l]).reshape(-1, 1, D)

        q_c, kt_c, v_c, zg_c, zc_c = _inproj(x_ctx, m_ctx[0], m_ctx[1], w_in_bf, gq, gk, ones_c, zeros_c, zeros_c,
                                              gmat, tile=t_inc)
        staged = []
        for g, bs in enumerate(groups):
            xg = x_grp[g]
            q_l, kt_l, v_l, zg_l, zc_l = _inproj(xg, m_lat[0][bs], m_lat[1][bs], w_in_bf, gq, gk, cos, slo, shi, gmat,
                                                  tile=t_in)
            kt_all = jnp.concatenate([kt_c[bs], kt_l], axis=3)
            v_all = jnp.concatenate([v_c[bs], v_l], axis=1)
            o_lat = _attention(q_l, kt_all, v_all, tq=256, tk=tk_lat)
            x1, h2 = _mixers(xg, o_lat, zg_l, zc_l, m_lat[2][bs], m_lat[3][bs], m_lat[4][bs], mix_prm, tile=t_mix,
                             alpha=alpha)
            staged.append((h2, x1) + tuple(retrieve(h2, wq_bf, keys_bf)))
        for g, bs in enumerate(groups):
            h2, x1, gate, eidx = staged[g]
            share = SC_SHARE_FIRST if g == 0 and n_groups > 1 else SC_SHARE_LATER
            x_grp[g] = peer(h2, x1, m_lat[5][bs], gate, eidx, l, L, tab, sc_share(bg * L, share))

        if not last:
            o_ctx = _attention(q_c, kt_c, v_c, tq=128, tk=tk_ctx)
            x1c, h2c = _mixers(x_ctx, o_ctx, zg_c, zc_c, m_ctx[2], m_ctx[3], m_ctx[4], mix_prm, tile=t_mixc,
                               alpha=alpha)
            gate_c, eidx_c = retrieve(h2c, wq_bf, keys_bf)
            x_ctx = peer(h2c, x1c, m_ctx[5], gate_c, eidx_c, l, LC, tab, 0)
    return jnp.concatenate(x_grp, axis=0)
```
